```python
import jax, jax.numpy as jnp
from jax import lax
import numpy as np

D_MODEL = 1024
BATCH = 8
SEQ = 8192
DEPTH = 1

EPS = 1e-6
ROPE_THETA = 500000.0
Q_BLOCK = 128
MLA_HEADS = 8
MLA_Q_LORA = 256
MLA_KV_LORA = 128
MLA_NOPE = 64
MLA_ROPE = 32
MLA_V = 64
DSA_HEADS = 8
DSA_HEAD_DIM = 64
DSA_ROT = DSA_HEAD_DIM // 4
IDX_HEADS = 8
IDX_DIM = 64
IDX_ROT = IDX_DIM // 4
TOPK_MAX = 256
N_EXPERTS = 32
TOP_K = 4
D_FF = 1024
SWIGLU_LIMIT = 7.0
SWIGLU_ALPHA = 1.702
EXPERT_BLOCK = 256
IN_SPLITS = (MLA_Q_LORA, MLA_KV_LORA, MLA_ROPE,
             DSA_HEADS * DSA_HEAD_DIM, DSA_HEAD_DIM, DSA_HEAD_DIM,
             IDX_HEADS * IDX_DIM, IDX_DIM, IDX_HEADS,
             D_MODEL, D_MODEL)
D_IN = 256 + 128 + 32 + 512 + 64 + 64 + 512 + 64 + 8 + D_MODEL + D_MODEL

kernel_name = 'hybrid_mla_dsa_gated_moe_block'


def rms_norm(x, g):
    xf = x.astype(jnp.float32)
    y = xf * lax.rsqrt(jnp.mean(xf * xf, axis=-1, keepdims=True) + EPS)
    return (y * g.astype(jnp.float32)).astype(x.dtype)


def rope_partial(x, pos, rot_dim):
    half = rot_dim // 2
    inv_freq = ROPE_THETA ** (-jnp.arange(half, dtype=jnp.float32) / half)
    ang = pos.astype(jnp.float32)[:, :, None, None] * inv_freq
    cos, sin = jnp.cos(ang), jnp.sin(ang)
    x1 = x[..., :half].astype(jnp.float32)
    x2 = x[..., half:rot_dim].astype(jnp.float32)
    rot = jnp.concatenate([x1 * cos - x2 * sin, x2 * cos + x1 * sin], axis=-1).astype(x.dtype)
    return jnp.concatenate([rot, x[..., rot_dim:]], axis=-1)


def split_cols(z, sizes):
    outs, off = [], 0
    for s in sizes:
        outs.append(z[..., off:off + s])
        off += s
    return outs


def to_blocks(a, nb):
    return a.reshape(a.shape[0], nb, Q_BLOCK, *a.shape[2:]).swapaxes(0, 1)


def from_blocks(o):
    nb, b = o.shape[0], o.shape[1]
    return o.swapaxes(0, 1).reshape(b, nb * Q_BLOCK, -1)


def mla_branch(c_q, c_kv, k_pe, pos, q_norm_g, w_uq, kv_norm_g, w_ukv):
    B, S, _ = c_q.shape
    nb = S // Q_BLOCK
    q = (rms_norm(c_q, q_norm_g) @ w_uq).reshape(B, S, MLA_HEADS, MLA_NOPE + MLA_ROPE)
    q_nope = q[..., :MLA_NOPE]
    q_pe = rope_partial(q[..., MLA_NOPE:], pos, MLA_ROPE)
    kv = (rms_norm(c_kv, kv_norm_g) @ w_ukv).reshape(B, S, MLA_HEADS, MLA_NOPE + MLA_V)
    k_nope, v = kv[..., :MLA_NOPE], kv[..., MLA_NOPE:]
    k_rot = rope_partial(k_pe[:, :, None, :], pos, MLA_ROPE)[:, :, 0]
    key_pos = jnp.arange(S)
    scale = (MLA_NOPE + MLA_ROPE) ** -0.5

    def attend_block(args):
        i, qn, qp = args
        s = (jnp.einsum('bqhd,bkhd->bhqk', qn, k_nope)
             + jnp.einsum('bqhr,bkr->bhqk', qp, k_rot)).astype(jnp.float32) * scale
        q_pos = i * Q_BLOCK + jnp.arange(Q_BLOCK)
        s = jnp.where(key_pos[None, :] <= q_pos[:, None], s, -jnp.inf)
        p = jax.nn.softmax(s, axis=-1).astype(v.dtype)
        return jnp.einsum('bhqk,bkhd->bqhd', p, v)

    out = lax.map(attend_block, (jnp.arange(nb), to_blocks(q_nope, nb), to_blocks(q_pe, nb)))
    return from_blocks(out)


def dsa_branch(q, k, v, q_idx, k_idx, w_idx, pos):
    B, S, _ = q.shape
    nb = S // Q_BLOCK
    topk = min(TOPK_MAX, S // 4)
    q = rope_partial(q.reshape(B, S, DSA_HEADS, DSA_HEAD_DIM), pos, DSA_ROT)
    k = rope_partial(k[:, :, None, :], pos, DSA_ROT)[:, :, 0]
    q_idx = rope_partial(q_idx.reshape(B, S, IDX_HEADS, IDX_DIM), pos, IDX_ROT)
    k_idx = rope_partial(k_idx[:, :, None, :], pos, IDX_ROT)[:, :, 0]
    w_idx = w_idx.astype(jnp.float32) * IDX_HEADS ** -0.5
    key_pos = jnp.arange(S)
    gather_rows = jax.vmap(lambda table, idx: table[idx])

    def attend_block(args):
        i, qb, qib, wib = args
        q_pos = i * Q_BLOCK + jnp.arange(Q_BLOCK)
        causal = key_pos[None, :] <= q_pos[:, None]
        rel = jax.nn.relu(jnp.einsum('bqhd,bkd->bqhk', qib, k_idx).astype(jnp.float32) * IDX_DIM ** -0.5)
        score = jnp.einsum('bqh,bqhk->bqk', wib, rel)
        score = jnp.where(causal[None], score, -jnp.inf)
        _, sel = lax.top_k(score, topk)
        valid = sel <= q_pos[None, :, None]
        k_sel = gather_rows(k, sel)
        v_sel = gather_rows(v, sel)
        s = jnp.einsum('bqhd,bqkd->bhqk', qb, k_sel).astype(jnp.float32) * DSA_HEAD_DIM ** -0.5
        s = jnp.where(valid[:, None], s, -jnp.inf)
        p = jax.nn.softmax(s, axis=-1).astype(v.dtype)
        return jnp.einsum('bhqk,bqkd->bqhd', p, v_sel)

    out = lax.map(attend_block, (jnp.arange(nb), to_blocks(q, nb), to_blocks(q_idx, nb), to_blocks(w_idx, nb)))
    return from_blocks(out)


def moe_ffn(h, router_w, router_b, w_gu, b_gu, w_d, b_d):
    B, S, D = h.shape
    t = h.reshape(B * S, D)
    n_tok = t.shape[0]
    n_assign = n_tok * TOP_K
    n_blk = -(-n_assign // EXPERT_BLOCK) + N_EXPERTS
    logits = (t @ router_w + router_b).astype(jnp.float32)
    top_logit, top_e = lax.top_k(logits, TOP_K)
    gate = jax.nn.softmax(top_logit, axis=-1)
    flat_e = top_e.reshape(-1)
    order = jnp.argsort(flat_e)
    e_sorted = flat_e[order]
    tok_sorted = order // TOP_K
    gate_sorted = gate.reshape(-1)[order]
    counts = jnp.bincount(flat_e, length=N_EXPERTS)
    padded = (counts + EXPERT_BLOCK - 1) // EXPERT_BLOCK * EXPERT_BLOCK
    pad_end = jnp.cumsum(padded)
    pad_start = pad_end - padded
    sort_start = jnp.cumsum(counts) - counts
    dest = pad_start[e_sorted] + jnp.arange(n_assign) - sort_start[e_sorted]
    xbuf = jnp.zeros((n_blk * EXPERT_BLOCK, D), t.dtype).at[dest].set(t[tok_sorted])
    blk_e = jnp.minimum(jnp.searchsorted(pad_end, jnp.arange(n_blk) * EXPERT_BLOCK, side='right'), N_EXPERTS - 1)

    def expert_block(args):
        xb, e = args
        gu = xb @ w_gu[e] + b_gu[e]
        glu = jnp.minimum(gu[:, :D_FF], SWIGLU_LIMIT)
        lin = jnp.clip(gu[:, D_FF:], -SWIGLU_LIMIT, SWIGLU_LIMIT)
        act = glu * jax.nn.sigmoid(SWIGLU_ALPHA * glu) * (lin + 1.0)
        return act @ w_d[e] + b_d[e]

    ybuf = lax.map(expert_block, (xbuf.reshape(n_blk, EXPERT_BLOCK, D), blk_e))
    y = ybuf.reshape(-1, D)[dest] * gate_sorted[:, None].astype(t.dtype)
    out = jax.ops.segment_sum(y, tok_sorted, num_segments=n_tok)
    return out.reshape(B, S, D)


def setup_inputs(seed: int = 0) -> dict:
    key = jax.random.key(seed)
    ks = jax.random.split(key, 20)
    f32 = jnp.float32
    L = DEPTH

    def nrm(k, shape, scale):
        return jax.random.normal(k, shape, f32) * scale

    def gain(k, shape):
        return 1.0 + 0.1 * jax.random.normal(k, shape, f32)

    return {
        'x': nrm(ks[0], (BATCH, SEQ, D_MODEL), 1.0),
        'positions': jnp.tile(jnp.arange(SEQ, dtype=jnp.int32)[None, :], (BATCH, 1)),
        'attn_norm_g': gain(ks[1], (L, D_MODEL)),
        'w_in': nrm(ks[2], (L, D_MODEL, D_IN), D_MODEL ** -0.5),
        'mla_q_norm_g': gain(ks[3], (L, MLA_Q_LORA)),
        'mla_w_uq': nrm(ks[4], (L, MLA_Q_LORA, MLA_HEADS * (MLA_NOPE + MLA_ROPE)), MLA_Q_LORA ** -0.5),
        'mla_kv_norm_g': gain(ks[5], (L, MLA_KV_LORA)),
        'mla_w_ukv': nrm(ks[6], (L, MLA_KV_LORA, MLA_HEADS * (MLA_NOPE + MLA_V)), MLA_KV_LORA ** -0.5),
        'w_branch_mla': nrm(ks[7], (L, MLA_HEADS * MLA_V, D_MODEL), (MLA_HEADS * MLA_V) ** -0.5),
        'w_branch_dsa': nrm(ks[8], (L, DSA_HEADS * DSA_HEAD_DIM, D_MODEL), (DSA_HEADS * DSA_HEAD_DIM) ** -0.5),
        'w_out': nrm(ks[9], (L, D_MODEL, D_MODEL), D_MODEL ** -0.5),
        'ffn_norm_g': gain(ks[10], (L, D_MODEL)),
        'router_w': nrm(ks[11], (L, D_MODEL, N_EXPERTS), D_MODEL ** -0.5),
        'router_b': nrm(ks[12], (L, N_EXPERTS), 0.01),
        'w_gate_up': nrm(ks[13], (L, N_EXPERTS, D_MODEL, 2 * D_FF), D_MODEL ** -0.5),
        'b_gate_up': nrm(ks[14], (L, N_EXPERTS, 2 * D_FF), 0.02),
        'w_down': nrm(ks[15], (L, N_EXPERTS, D_FF, D_MODEL), D_FF ** -0.5),
        'b_down': nrm(ks[16], (L, N_EXPERTS, D_MODEL), 0.02),
        'final_norm_g': gain(ks[17], (D_MODEL,)),
    }


def reference(x, positions, attn_norm_g, w_in, mla_q_norm_g, mla_w_uq, mla_kv_norm_g, mla_w_ukv,
              w_branch_mla, w_branch_dsa, w_out, ffn_norm_g, router_w, router_b,
              w_gate_up, b_gate_up, w_down, b_down, final_norm_g):
    for l in range(DEPTH):
        h = rms_norm(x, attn_norm_g[l])
        z = h @ w_in[l]
        (c_q, c_kv, k_pe, q_b, k_b, v_b, q_idx, k_idx, w_idx, gate_a, gate_b) = split_cols(z, IN_SPLITS)
        y_a = mla_branch(c_q, c_kv, k_pe, positions, mla_q_norm_g[l], mla_w_uq[l], mla_kv_norm_g[l], mla_w_ukv[l])
        y_b = dsa_branch(q_b, k_b, v_b, q_idx, k_idx, w_idx, positions)
        merged = (jax.nn.sigmoid(gate_a) * (y_a @ w_branch_mla[l])
                  + jax.nn.sigmoid(gate_b) * (y_b @ w_branch_dsa[l]))
        x = x + merged @ w_out[l]
        x = x + moe_ffn(rms_norm(x, ffn_norm_g[l]), router_w[l], router_b[l],
                        w_gate_up[l], b_gate_up[l], w_down[l], b_down[l])
    return rms_norm(x, final_norm_g)
```

```python
import functools
import math

import jax
import jax.numpy as jnp
from jax import lax
from jax.experimental import pallas as pl
from jax.experimental.pallas import tpu as pltpu

F32 = jnp.float32
BF16 = jnp.bfloat16

LANES = 128

D_MODEL = 1024
EPS = 1e-6
ROPE_THETA = 500000.0
MLA_HEADS = 8
MLA_Q_LORA = 256
MLA_KV_LORA = 128
MLA_NOPE = 64
MLA_ROPE = 32
MLA_V = 64
DSA_HEADS = 8
DSA_HEAD_DIM = 64
DSA_ROT = 16
IDX_HEADS = 8
IDX_DIM = 64
TOPK_MAX = 256
N_EXPERTS = 32
TOP_K = 4
D_FF = 1024
SWIGLU_LIMIT = 7.0
SWIGLU_ALPHA = 1.702

VMEM_LIMIT = 56 * 1024 * 1024

_OFF_CQ = 0
_OFF_CKV = _OFF_CQ + MLA_Q_LORA
_OFF_KPE = _OFF_CKV + MLA_KV_LORA
_OFF_QB = _OFF_KPE + MLA_ROPE
_OFF_KB = _OFF_QB + DSA_HEADS * DSA_HEAD_DIM
_OFF_VB = _OFF_KB + DSA_HEAD_DIM
_OFF_QI = _OFF_VB + DSA_HEAD_DIM
_OFF_KI = _OFF_QI + IDX_HEADS * IDX_DIM
_OFF_WI = _OFF_KI + IDX_DIM
_OFF_GA = _OFF_WI + IDX_HEADS
_OFF_GB = _OFF_GA + D_MODEL
_D_IN = _OFF_GB + D_MODEL


def _cparams(sem):
    return pltpu.CompilerParams(dimension_semantics=sem, vmem_limit_bytes=VMEM_LIMIT)


def _rope_group(xg, c, slo, shi, shift):
    return (xg * c + pltpu.roll(xg, LANES - shift, 1) * slo
            + pltpu.roll(xg, shift, 1) * shi)


def _in_proj_body(x_ref, g_ref, w1_ref, w2_ref, w3_ref, w5_ref, qng_ref, wuq_ref,
                  kvng_ref, wuk_ref, wuv_ref, cm_ref, slm_ref, shm_ref, cd_ref,
                  sld_ref, shd_ref,
                  qm_ref, km_ref, vm_ref, qd_ref, qi_ref, kd_ref, ki_ref, vd_ref,
                  wi_ref, ga_ref, gb_ref):
    x = x_ref[...]
    h = x * lax.rsqrt(jnp.mean(x * x, axis=-1, keepdims=True) + EPS) * g_ref[...]
    hb = h.astype(BF16)

    cm, slm, shm = cm_ref[...], slm_ref[...], shm_ref[...]
    cd, sld, shd = cd_ref[...], sld_ref[...], shd_ref[...]

    z1 = jnp.dot(hb, w1_ref[...], preferred_element_type=F32)
    cq = z1[:, 0:MLA_Q_LORA]
    cqn = cq * lax.rsqrt(jnp.mean(cq * cq, axis=-1, keepdims=True) + EPS) * qng_ref[...]
    q = jnp.dot(cqn.astype(BF16), wuq_ref[...], preferred_element_type=F32)
    q_scale = (MLA_NOPE + MLA_ROPE) ** -0.5
    for j in range(MLA_HEADS):
        qg = _rope_group(q[:, j * LANES:(j + 1) * LANES], cm, slm, shm, MLA_ROPE // 2)
        qm_ref[:, j * LANES:(j + 1) * LANES] = (qg * q_scale).astype(BF16)

    ckv = z1[:, MLA_Q_LORA:MLA_Q_LORA + MLA_KV_LORA]
    ckvn = (ckv * lax.rsqrt(jnp.mean(ckv * ckv, axis=-1, keepdims=True) + EPS)
            * kvng_ref[...]).astype(BF16)
    kpe = _rope_group(z1[:, MLA_Q_LORA + MLA_KV_LORA:], cm, slm, shm, MLA_ROPE // 2)
    kn = jnp.dot(ckvn, wuk_ref[...], preferred_element_type=F32)
    for j in range(MLA_HEADS):
        km_ref[:, j * LANES:(j + 1) * LANES] = (kn[:, j * LANES:(j + 1) * LANES] + kpe).astype(BF16)
    vm_ref[...] = jnp.dot(ckvn, wuv_ref[...], preferred_element_type=F32).astype(BF16)

    z2 = jnp.dot(hb, w2_ref[...], preferred_element_type=F32)
    d_scale = DSA_HEAD_DIM ** -0.5
    for j in range(DSA_HEADS // 2):
        g = _rope_group(z2[:, j * LANES:(j + 1) * LANES], cd, sld, shd, DSA_ROT // 2) * d_scale
        qd_ref[2 * j] = g[:, :DSA_HEAD_DIM].astype(BF16)
        qd_ref[2 * j + 1] = g[:, DSA_HEAD_DIM:].astype(BF16)
    base = DSA_HEADS * DSA_HEAD_DIM
    for j in range(IDX_HEADS // 2):
        g = _rope_group(z2[:, base + j * LANES:base + (j + 1) * LANES], cd, sld, shd, DSA_ROT // 2)
        qi_ref[2 * j] = g[:, :IDX_DIM].astype(BF16)
        qi_ref[2 * j + 1] = g[:, IDX_DIM:].astype(BF16)

    z3 = jnp.dot(hb, w3_ref[...], preferred_element_type=F32)
    kb = _rope_group(z3[:, :LANES], cd, sld, shd, DSA_ROT // 2)
    kd_ref[...] = kb[:, :DSA_HEAD_DIM].astype(BF16)
    ki_ref[...] = kb[:, DSA_HEAD_DIM:].astype(BF16)
    vd_ref[...] = z3[:, LANES:LANES + DSA_HEAD_DIM].astype(BF16)
    w_scale = IDX_HEADS ** -0.5 * IDX_DIM ** -0.5
    wi_ref[...] = z3[:, LANES + DSA_HEAD_DIM:LANES + DSA_HEAD_DIM + IDX_HEADS] * w_scale

    z5 = jnp.dot(hb, w5_ref[...], preferred_element_type=F32)
    ga_ref[...] = jax.nn.sigmoid(z5[:, :D_MODEL]).astype(BF16)
    gb_ref[...] = jax.nn.sigmoid(z5[:, D_MODEL:]).astype(BF16)


def _rope_tables(pos, rot_dim, lane_of_x1, period):
    half = rot_dim // 2
    inv_freq = ROPE_THETA ** (-jnp.arange(half, dtype=F32) / half)
    ang = pos.astype(F32)[:, None] * inv_freq
    cos, sin = jnp.cos(ang), jnp.sin(ang)
    n = pos.shape[0]
    reps = LANES // period
    pad_l = lane_of_x1
    pad_r = period - lane_of_x1 - rot_dim

    def pattern(a, b, fill):
        blk = jnp.concatenate([jnp.full((n, pad_l), fill, F32), a, b,
                               jnp.full((n, pad_r), fill, F32)], axis=1)
        return jnp.tile(blk, (1, reps))

    zeros = jnp.zeros_like(sin)
    c = pattern(cos, cos, 1.0)
    slo = pattern(-sin, zeros, 0.0)
    shi = pattern(zeros, sin, 0.0)
    return c, slo, shi


def _head_cols(w, n_heads, widths, total):
    k = w.shape[0]
    per = sum(widths)
    w = w.reshape(k, n_heads, per)
    return jnp.pad(w, ((0, 0), (0, 0), (0, total - per))).reshape(k, n_heads * total)


def _in_proj(x2, pos, attn_norm_g, w_in, q_norm_g, w_uq, kv_norm_g, w_ukv, tm):
    n = x2.shape[0]
    d = D_MODEL
    zc = lambda k: jnp.zeros((d, k), F32)
    w1 = jnp.concatenate([w_in[:, _OFF_CQ:_OFF_KPE], zc(MLA_NOPE), w_in[:, _OFF_KPE:_OFF_QB],
                          zc(LANES - MLA_NOPE - MLA_ROPE)], axis=1).astype(BF16)
    w2 = jnp.concatenate([w_in[:, _OFF_QB:_OFF_KB], w_in[:, _OFF_QI:_OFF_KI]], axis=1).astype(BF16)
    w3 = jnp.concatenate([w_in[:, _OFF_KB:_OFF_VB], w_in[:, _OFF_KI:_OFF_WI],
                          w_in[:, _OFF_VB:_OFF_QI], w_in[:, _OFF_WI:_OFF_GA],
                          zc(LANES - DSA_HEAD_DIM - IDX_HEADS)], axis=1).astype(BF16)
    w5 = w_in[:, _OFF_GA:].astype(BF16)
    wuq = _head_cols(w_uq, MLA_HEADS, (MLA_NOPE, MLA_ROPE), LANES).astype(BF16)
    w_ukv3 = w_ukv.reshape(MLA_KV_LORA, MLA_HEADS, MLA_NOPE + MLA_V)
    wuk = jnp.pad(w_ukv3[:, :, :MLA_NOPE], ((0, 0), (0, 0), (0, LANES - MLA_NOPE))
                  ).reshape(MLA_KV_LORA, MLA_HEADS * LANES).astype(BF16)
    wuv = w_ukv3[:, :, MLA_NOPE:].reshape(MLA_KV_LORA, MLA_HEADS * MLA_V).astype(BF16)
    cm, slm, shm = _rope_tables(pos, MLA_ROPE, MLA_NOPE, LANES)
    cd, sld, shd = _rope_tables(pos, DSA_ROT, 0, DSA_HEAD_DIM)

    row = lambda w_: pl.BlockSpec((tm, w_), lambda i: (i, 0))
    full = lambda a: pl.BlockSpec(a.shape, lambda i: (0,) * a.ndim)
    hm = pl.BlockSpec((DSA_HEADS, tm, DSA_HEAD_DIM), lambda i: (0, i, 0))
    g2 = attn_norm_g.reshape(1, d)
    qng = q_norm_g.reshape(1, -1)
    kvng = kv_norm_g.reshape(1, -1)
    consts = (g2, w1, w2, w3, w5, qng, wuq, kvng, wuk, wuv)
    out_shape = (
        jax.ShapeDtypeStruct((n, MLA_HEADS * LANES), BF16),
        jax.ShapeDtypeStruct((n, MLA_HEADS * LANES), BF16),
        jax.ShapeDtypeStruct((n, MLA_HEADS * MLA_V), BF16),
        jax.ShapeDtypeStruct((DSA_HEADS, n, DSA_HEAD_DIM), BF16),
        jax.ShapeDtypeStruct((IDX_HEADS, n, IDX_DIM), BF16),
        jax.ShapeDtypeStruct((n, DSA_HEAD_DIM), BF16),
        jax.ShapeDtypeStruct((n, IDX_DIM), BF16),
        jax.ShapeDtypeStruct((n, DSA_HEAD_DIM), BF16),
        jax.ShapeDtypeStruct((n, IDX_HEADS), F32),
        jax.ShapeDtypeStruct((n, D_MODEL), BF16),
        jax.ShapeDtypeStruct((n, D_MODEL), BF16),
    )
    out_specs = (row(MLA_HEADS * LANES), row(MLA_HEADS * LANES), row(MLA_HEADS * MLA_V),
                 hm, hm, row(DSA_HEAD_DIM), row(IDX_DIM), row(DSA_HEAD_DIM), row(IDX_HEADS),
                 row(D_MODEL), row(D_MODEL))
    return pl.pallas_call(
        _in_proj_body,
        grid=(n // tm,),
        in_specs=[row(d)] + [full(a) for a in consts] + [row(LANES)] * 6,
        out_specs=out_specs,
        out_shape=out_shape,
        compiler_params=_cparams(("parallel",)),
        name="in_proj",
    )(x2, *consts, cm, slm, shm, cd, sld, shd)


NEG_BIG = -1e30
_NT = (((1,), (1,)), ((), ()))


def _mla_body(q_ref, k_ref, v_ref, o_ref, *, tq, tk):
    qi = pl.program_id(2)
    n_sub = tq // tk
    outs = []
    for hh in range(2):
        q = q_ref[:, hh * LANES:(hh + 1) * LANES]

        def step(j, carry, masked, q=q, hh=hh):
            m, l, acc = carry
            start = pl.multiple_of(j * tk, tk)
            ks = k_ref[pl.ds(start, tk), hh * LANES:(hh + 1) * LANES]
            vs = v_ref[pl.ds(start, tk), :]
            s = lax.dot_general(q, ks, _NT, preferred_element_type=F32)
            if masked:
                row = qi * tq + lax.broadcasted_iota(jnp.int32, (tq, tk), 0)
                col = j * tk + lax.broadcasted_iota(jnp.int32, (tq, tk), 1)
                s = jnp.where(col <= row, s, NEG_BIG)
            m_new = jnp.maximum(m, jnp.max(s, axis=-1, keepdims=True))
            alpha = jnp.exp(m - m_new)
            p = jnp.exp(s - m_new)
            l = alpha * l + jnp.sum(p, axis=-1, keepdims=True)
            acc = alpha * acc + jnp.dot(p.astype(BF16), vs, preferred_element_type=F32)
            return m_new, l, acc

        carry = (jnp.full((tq, 1), NEG_BIG, F32), jnp.zeros((tq, 1), F32),
                 jnp.zeros((tq, LANES), F32))
        carry = lax.fori_loop(0, qi * n_sub, functools.partial(step, masked=False), carry)
        for dd in range(n_sub):
            carry = step(qi * n_sub + dd, carry, True)
        m, l, acc = carry
        outs.append(acc / l)
    lane = lax.broadcasted_iota(jnp.int32, (tq, LANES), 1)
    o_ref[...] = jnp.where(lane < MLA_V, outs[0], outs[1]).astype(BF16)


def _mla_attention(qm, km, vm, b, s, tq, tk):
    n = b * s
    nq = s // tq
    return pl.pallas_call(
        functools.partial(_mla_body, tq=tq, tk=tk),
        grid=(b, MLA_HEADS // 2, nq),
        in_specs=[
            pl.BlockSpec((tq, 2 * LANES), lambda bi, hp, qi: (bi * nq + qi, hp)),
            pl.BlockSpec((s, 2 * LANES), lambda bi, hp, qi: (bi, hp)),
            pl.BlockSpec((s, 2 * MLA_V), lambda bi, hp, qi: (bi, hp)),
        ],
        out_specs=pl.BlockSpec((tq, 2 * MLA_V), lambda bi, hp, qi: (bi * nq + qi, hp)),
        out_shape=jax.ShapeDtypeStruct((n, MLA_HEADS * MLA_V), BF16),
        compiler_params=_cparams(("parallel", "parallel", "arbitrary")),
        name="mla_attention",
    )(qm, km, vm)


INT_MIN = -2 ** 31
KEY_NEG_INF = (0xFF800000 - 2 ** 32) ^ 0x7FFFFFFF


def _sortable_key(score):
    bits = pltpu.bitcast(score, jnp.int32)
    return bits ^ ((bits >> 31) & 0x7FFFFFFF)


def _dsa_body(qi_ref, qd_ref, ki_ref, kd_ref, vd_ref, w_ref, o_ref, keys_ref, *, tq, tk, topk):
    qb = pl.program_id(1)
    n_tiles = (qb * tq + tq + tk - 1) // tk
    q_pos = qb * tq + lax.broadcasted_iota(jnp.int32, (tq, 1), 0)

    qidx = qi_ref[...].reshape(IDX_HEADS * tq, IDX_DIM)
    w = w_ref[...]
    wcols = [jnp.broadcast_to(w[:, h:h + 1], (tq, tk)) for h in range(IDX_HEADS)]

    def score_tile(j, _):
        start = pl.multiple_of(j * tk, tk)
        kt = ki_ref[pl.ds(start, tk), :]
        sh = lax.dot_general(qidx, kt, _NT, preferred_element_type=F32)
        sc = jnp.zeros((tq, tk), F32)
        for h in range(IDX_HEADS):
            sc = sc + jnp.maximum(sh[h * tq:(h + 1) * tq], 0.0) * wcols[h]
        col = j * tk + lax.broadcasted_iota(jnp.int32, (tq, tk), 1)
        sc = jnp.where(col <= q_pos, sc, -jnp.inf)
        keys_ref[:, pl.ds(start, tk)] = _sortable_key(sc)
        return 0

    lax.fori_loop(0, n_tiles, score_tile, 0)

    def count_ge(cand):
        def body(j, cnt):
            start = pl.multiple_of(j * tk, tk)
            kt = keys_ref[:, pl.ds(start, tk)]
            hit = (kt >= cand).astype(jnp.int32)
            for c in range(tk // LANES):
                cnt = cnt + hit[:, c * LANES:(c + 1) * LANES]
            return cnt
        cnt = lax.fori_loop(0, n_tiles, body, jnp.zeros((tq, LANES), jnp.int32))
        return jnp.sum(cnt, axis=-1, keepdims=True)

    def bit_step(i, t):
        cand = t + jnp.left_shift(jnp.int32(1), 31 - i)
        return jnp.where(count_ge(cand) >= topk, cand, t)

    thr = lax.fori_loop(0, 32, bit_step, jnp.full((tq, 1), INT_MIN, jnp.int32))
    thr = jnp.maximum(thr, KEY_NEG_INF + 1)

    qd = qd_ref[...].reshape(DSA_HEADS * tq, DSA_HEAD_DIM)

    def attn_tile(j, carry):
        m, l, acc = carry
        start = pl.multiple_of(j * tk, tk)
        kt = kd_ref[pl.ds(start, tk), :]
        vt = vd_ref[pl.ds(start, tk), :]
        sel = keys_ref[:, pl.ds(start, tk)] >= thr
        s = lax.dot_general(qd, kt, _NT, preferred_element_type=F32)
        s = jnp.where(sel[None], s.reshape(DSA_HEADS, tq, tk), NEG_BIG)
        m_new = jnp.maximum(m, jnp.max(s, axis=-1, keepdims=True))
        alpha = jnp.exp(m - m_new)
        p = jnp.exp(s - m_new)
        l = alpha * l + jnp.sum(p, axis=-1, keepdims=True)
        pv = jnp.dot(p.reshape(DSA_HEADS * tq, tk).astype(BF16), vt, preferred_element_type=F32)
        acc = alpha * acc + pv.reshape(DSA_HEADS, tq, DSA_HEAD_DIM)
        return m_new, l, acc

    init = (jnp.full((DSA_HEADS, tq, 1), NEG_BIG, F32), jnp.zeros((DSA_HEADS, tq, 1), F32),
            jnp.zeros((DSA_HEADS, tq, DSA_HEAD_DIM), F32))
    m, l, acc = lax.fori_loop(0, n_tiles, attn_tile, init)
    out = acc / l
    for h in range(DSA_HEADS):
        o_ref[:, h * DSA_HEAD_DIM:(h + 1) * DSA_HEAD_DIM] = out[h].astype(BF16)


def _dsa_attention(qi, qd, ki, kd, vd, wi, b, s, tq, tk):
    n = b * s
    nq = s // tq
    topk = min(TOPK_MAX, s // 4)
    hm = pl.BlockSpec((DSA_HEADS, tq, DSA_HEAD_DIM), lambda bi, qb: (0, bi * nq + qb, 0))
    kv = pl.BlockSpec((s, DSA_HEAD_DIM), lambda bi, qb: (bi, 0))
    return pl.pallas_call(
        functools.partial(_dsa_body, tq=tq, tk=tk, topk=topk),
        grid=(b, nq),
        in_specs=[hm, hm, kv, kv, kv,
                  pl.BlockSpec((tq, IDX_HEADS), lambda bi, qb: (bi * nq + qb, 0))],
        out_specs=pl.BlockSpec((tq, DSA_HEADS * DSA_HEAD_DIM), lambda bi, qb: (bi * nq + qb, 0)),
        out_shape=jax.ShapeDtypeStruct((n, DSA_HEADS * DSA_HEAD_DIM), BF16),
        scratch_shapes=[pltpu.VMEM((tq, s), jnp.int32)],
        compiler_params=_cparams(("parallel", "arbitrary")),
        name="dsa_attention",
    )(qi, qd, ki, kd, vd, wi)


HALF_D = D_MODEL // 2


def _pack_bf16_pairs(y):
    r = pltpu.bitcast(y.astype(BF16).astype(F32), jnp.uint32)
    return r[:, :HALF_D] | (r[:, HALF_D:] >> 16)


def _unpack_bf16_pairs(p):
    hi = pltpu.bitcast(p & jnp.uint32(0xFFFF0000), F32)
    lo = pltpu.bitcast(p << 16, F32)
    return jnp.concatenate([hi, lo], axis=1)


def _split_bf16(a):
    hi = a.astype(BF16)
    lo = (a - hi.astype(F32)).astype(BF16)
    return hi, lo


def _merge_body(x_ref, ya_ref, yb_ref, ga_ref, gb_ref, wa_ref, wb_ref, wo_ref, g_ref,
                rwh_ref, rwl_ref, rb_ref,
                x2_ref, hp_ref, e_ref, gate_ref, rank_ref, cnt_ref, carry_ref, *, tm):
    i = pl.program_id(0)

    @pl.when(i == 0)
    def _():
        carry_ref[...] = jnp.zeros_like(carry_ref)

    ma = jnp.dot(ya_ref[...], wa_ref[...], preferred_element_type=F32)
    mb = jnp.dot(yb_ref[...], wb_ref[...], preferred_element_type=F32)
    merged = ga_ref[...].astype(F32) * ma + gb_ref[...].astype(F32) * mb
    x2 = x_ref[...] + jnp.dot(merged.astype(BF16), wo_ref[...], preferred_element_type=F32)
    x2_ref[...] = x2
    h = x2 * lax.rsqrt(jnp.mean(x2 * x2, axis=-1, keepdims=True) + EPS) * g_ref[...]
    hp_ref[...] = _pack_bf16_pairs(h)

    hh, hl = _split_bf16(h)
    logits = (jnp.dot(hh, rwh_ref[...], preferred_element_type=F32)
              + jnp.dot(hh, rwl_ref[...], preferred_element_type=F32)
              + jnp.dot(hl, rwh_ref[...], preferred_element_type=F32)) + rb_ref[...]

    lane = lax.broadcasted_iota(jnp.int32, (tm, LANES), 1)
    work = logits
    experts, vals = [], []
    onehot = jnp.zeros((tm, LANES), F32)
    for _ in range(TOP_K):
        mx = jnp.max(work, axis=-1, keepdims=True)
        idx = jnp.min(jnp.where(work == mx, lane, LANES), axis=-1, keepdims=True)
        hit = lane == idx
        experts.append(idx)
        vals.append(mx)
        onehot = onehot + hit.astype(F32)
        work = jnp.where(hit, -jnp.inf, work)
    ex = [jnp.exp(v - vals[0]) for v in vals]
    denom = ex[0] + ex[1] + ex[2] + ex[3]
    for r in range(TOP_K):
        e_ref[:, r:r + 1] = experts[r]
        gate_ref[:, r:r + 1] = ex[r] / denom

    rr = lax.broadcasted_iota(jnp.int32, (tm, tm), 0)
    cc = lax.broadcasted_iota(jnp.int32, (tm, tm), 1)
    lower = (cc < rr).astype(BF16)
    prefix = jnp.dot(lower, onehot.astype(BF16), preferred_element_type=F32) + carry_ref[0:1, :]
    for r in range(TOP_K):
        rk = jnp.sum(jnp.where(lane == experts[r], prefix, 0.0), axis=-1, keepdims=True)
        rank_ref[:, r:r + 1] = rk.astype(jnp.int32)
    total = carry_ref[0:1, :] + jnp.sum(onehot, axis=0, keepdims=True)
    carry_ref[...] = jnp.broadcast_to(total, carry_ref.shape)
    cnt_ref[...] = jnp.broadcast_to(total, cnt_ref.shape)


def _merge_router(x2d, y_a, y_b, ga, gb, w_ba, w_bb, w_out, ffn_norm_g, router_w, router_b, tm):
    n, d = x2d.shape
    rw = jnp.pad(router_w, ((0, 0), (0, LANES - N_EXPERTS)))
    rwh = rw.astype(BF16)
    rwl = (rw - rwh.astype(F32)).astype(BF16)
    rb = jnp.pad(router_b, (0, LANES - N_EXPERTS), constant_values=NEG_BIG).reshape(1, LANES)
    consts = (w_ba.astype(BF16), w_bb.astype(BF16), w_out.astype(BF16), ffn_norm_g.reshape(1, d),
              rwh, rwl, rb)
    row = lambda w_: pl.BlockSpec((tm, w_), lambda i: (i, 0))
    full = lambda a: pl.BlockSpec(a.shape, lambda i: (0,) * a.ndim)
    out_shape = (
        jax.ShapeDtypeStruct((n, d), F32),
        jax.ShapeDtypeStruct((n, HALF_D), jnp.uint32),
        jax.ShapeDtypeStruct((n, TOP_K), jnp.int32),
        jax.ShapeDtypeStruct((n, TOP_K), F32),
        jax.ShapeDtypeStruct((n, TOP_K), jnp.int32),
        jax.ShapeDtypeStruct((8, LANES), F32),
    )
    return pl.pallas_call(
        functools.partial(_merge_body, tm=tm),
        grid=(n // tm,),
        in_specs=[row(d), row(HALF_D), row(HALF_D), row(d), row(d)] + [full(a) for a in consts],
        out_specs=(row(d), row(HALF_D), row(TOP_K), row(TOP_K), row(TOP_K),
                   pl.BlockSpec((8, LANES), lambda i: (0, 0))),
        out_shape=out_shape,
        scratch_shapes=[pltpu.VMEM((8, LANES), F32)],
        compiler_params=_cparams(("arbitrary",)),
        name="merge_router",
    )(x2d, y_a, y_b, ga, gb, *consts)


def _dispatch_body(dest_ref, hp_ref, xbuf_in_ref, xbuf_ref, sem, *, tm):
    del xbuf_in_ref

    def row_copy(i):
        return pltpu.make_async_copy(hp_ref.at[pl.ds(i // TOP_K, 1)],
                                     xbuf_ref.at[pl.ds(dest_ref[0, 0, i], 1)], sem)

    def start(i, _):
        row_copy(i).start()
        return 0

    def wait(i, _):
        row_copy(i).wait()
        return 0

    lax.fori_loop(0, tm * TOP_K, start, 0)
    lax.fori_loop(0, tm * TOP_K, wait, 0)


def _dispatch(hp, dest, n_rows, tm):
    n = hp.shape[0]
    dest3 = dest.reshape(n // tm, 1, tm * TOP_K)
    xbuf0 = jnp.zeros((n_rows, HALF_D), jnp.uint32)
    return pl.pallas_call(
        functools.partial(_dispatch_body, tm=tm),
        grid=(n // tm,),
        in_specs=[pl.BlockSpec((1, 1, tm * TOP_K), lambda i: (i, 0, 0), memory_space=pltpu.SMEM),
                  pl.BlockSpec((tm, HALF_D), lambda i: (i, 0)),
                  pl.BlockSpec(memory_space=pl.ANY)],
        out_specs=pl.BlockSpec(memory_space=pl.ANY),
        out_shape=jax.ShapeDtypeStruct((n_rows, HALF_D), jnp.uint32),
        scratch_shapes=[pltpu.SemaphoreType.DMA],
        input_output_aliases={2: 0},
        compiler_params=_cparams(("arbitrary",)),
        name="moe_dispatch",
    )(dest3, hp, xbuf0)


def _expert_body(blk_e_ref, n_used_ref, x_ref, wg_ref, wl_ref, bg_ref, bl_ref, wd_ref, bd_ref,
                 y_ref):
    del blk_e_ref
    i = pl.program_id(0)

    @pl.when(i < n_used_ref[0])
    def _():
        xb = _unpack_bf16_pairs(x_ref[...]).astype(BF16)
        glu = jnp.dot(xb, wg_ref[0], preferred_element_type=F32) + bg_ref[0]
        lin = jnp.dot(xb, wl_ref[0], preferred_element_type=F32) + bl_ref[0]
        glu = jnp.minimum(glu, SWIGLU_LIMIT)
        lin = jnp.clip(lin, -SWIGLU_LIMIT, SWIGLU_LIMIT)
        act = glu * jax.nn.sigmoid(SWIGLU_ALPHA * glu) * (lin + 1.0)
        y = jnp.dot(act.astype(BF16), wd_ref[0], preferred_element_type=F32) + bd_ref[0]
        y_ref[...] = _pack_bf16_pairs(y)

    @pl.when(i >= n_used_ref[0])
    def _():
        y_ref[...] = jnp.zeros_like(y_ref)


def _experts(xbuf, blk_e, n_used, w_gate_up, b_gate_up, w_down, b_down, tb):
    n_rows = xbuf.shape[0]
    d = D_MODEL
    wgu = w_gate_up.astype(BF16)
    wd = w_down.astype(BF16)
    bgu = b_gate_up.reshape(N_EXPERTS, 1, 2 * D_FF)
    bd = b_down.reshape(N_EXPERTS, 1, d)
    grid_spec = pltpu.PrefetchScalarGridSpec(
        num_scalar_prefetch=2,
        grid=(n_rows // tb,),
        in_specs=[
            pl.BlockSpec((tb, HALF_D), lambda i, be, nu: (i, 0)),
            pl.BlockSpec((1, d, D_FF), lambda i, be, nu: (be[i], 0, 0)),
            pl.BlockSpec((1, d, D_FF), lambda i, be, nu: (be[i], 0, 1)),
            pl.BlockSpec((1, 1, D_FF), lambda i, be, nu: (be[i], 0, 0)),
            pl.BlockSpec((1, 1, D_FF), lambda i, be, nu: (be[i], 0, 1)),
            pl.BlockSpec((1, D_FF, d), lambda i, be, nu: (be[i], 0, 0)),
            pl.BlockSpec((1, 1, d), lambda i, be, nu: (be[i], 0, 0)),
        ],
        out_specs=pl.BlockSpec((tb, HALF_D), lambda i, be, nu: (i, 0)),
    )
    return pl.pallas_call(
        _expert_body,
        grid_spec=grid_spec,
        out_shape=jax.ShapeDtypeStruct((n_rows, HALF_D), jnp.uint32),
        compiler_params=_cparams(("arbitrary",)),
        name="moe_experts",
    )(blk_e, n_used, xbuf, wgu, wgu, bgu, bgu, wd, bd)


def _combine_body(dest_ref, x2_ref, gate_ref, g_ref, ybuf_ref, o_ref, buf_ref, sem, *, tm):
    def row_copy(i):
        return pltpu.make_async_copy(ybuf_ref.at[pl.ds(dest_ref[0, 0, i], 1)],
                                     buf_ref.at[i % TOP_K, pl.ds(i // TOP_K, 1)], sem)

    def start(i, _):
        row_copy(i).start()
        return 0

    def wait(i, _):
        row_copy(i).wait()
        return 0

    lax.fori_loop(0, tm * TOP_K, start, 0)
    lax.fori_loop(0, tm * TOP_K, wait, 0)

    gate = gate_ref[...]
    out = x2_ref[...]
    for r in range(TOP_K):
        out = out + gate[:, r:r + 1] * _unpack_bf16_pairs(buf_ref[r])
    o_ref[...] = out * lax.rsqrt(jnp.mean(out * out, axis=-1, keepdims=True) + EPS) * g_ref[...]


def _combine(x2, gate, dest, ybuf, final_norm_g, tm):
    n, d = x2.shape
    dest3 = dest.reshape(n // tm, 1, tm * TOP_K)
    return pl.pallas_call(
        functools.partial(_combine_body, tm=tm),
        grid=(n // tm,),
        in_specs=[pl.BlockSpec((1, 1, tm * TOP_K), lambda i: (i, 0, 0), memory_space=pltpu.SMEM),
                  pl.BlockSpec((tm, d), lambda i: (i, 0)),
                  pl.BlockSpec((tm, TOP_K), lambda i: (i, 0)),
                  pl.BlockSpec((1, d), lambda i: (0, 0)),
                  pl.BlockSpec(memory_space=pl.ANY)],
        out_specs=pl.BlockSpec((tm, d), lambda i: (i, 0)),
        out_shape=jax.ShapeDtypeStruct((n, d), F32),
        scratch_shapes=[pltpu.VMEM((TOP_K, tm, HALF_D), jnp.uint32), pltpu.SemaphoreType.DMA],
        compiler_params=_cparams(("arbitrary",)),
        name="moe_combine",
    )(dest3, x2, gate, final_norm_g.reshape(1, d), ybuf)


def _moe(x2, hp, top_e, gate, rank, counts, w_gate_up, b_gate_up, w_down, b_down, final_norm_g,
         tb, tm):
    n = x2.shape[0]
    n_assign = n * TOP_K
    n_blk = -(-n_assign // tb) + N_EXPERTS
    cnt = counts[0, :N_EXPERTS].astype(jnp.int32)
    padded = (cnt + tb - 1) // tb * tb
    pad_end = jnp.cumsum(padded)
    pad_start = pad_end - padded
    dest = pad_start[top_e] + rank
    blk_e = jnp.minimum(jnp.searchsorted(pad_end, jnp.arange(n_blk, dtype=jnp.int32) * tb,
                                         side='right'), N_EXPERTS - 1).astype(jnp.int32)
    n_used = (pad_end[-1:] // tb).astype(jnp.int32)
    xbuf = _dispatch(hp, dest, n_blk * tb, tm)
    ybuf = _experts(xbuf, blk_e, n_used, w_gate_up, b_gate_up, w_down, b_down, tb)
    return _combine(x2, gate, dest, ybuf, final_norm_g, tm)


def kernel(x, positions, attn_norm_g, w_in, mla_q_norm_g, mla_w_uq, mla_kv_norm_g, mla_w_ukv,
           w_branch_mla, w_branch_dsa, w_out, ffn_norm_g, router_w, router_b,
           w_gate_up, b_gate_up, w_down, b_down, final_norm_g):
    b, s, d = x.shape
    n = b * s
    (qm, km, vm, qd, qi, kd, ki, vd, wi, ga, gb) = _in_proj(
        x.reshape(n, d), positions.reshape(n), attn_norm_g[0], w_in[0],
        mla_q_norm_g[0], mla_w_uq[0], mla_kv_norm_g[0], mla_w_ukv[0], tm=512)
    y_a = _mla_attention(qm, km, vm, b, s, tq=512, tk=512)
    y_b = _dsa_attention(qi, qd, ki, kd, vd, wi, b, s, tq=128, tk=512)
    x2, hp, top_e, gate, rank, counts = _merge_router(
        x.reshape(n, d), y_a, y_b, ga, gb, w_branch_mla[0], w_branch_dsa[0], w_out[0],
        ffn_norm_g[0], router_w[0], router_b[0], tm=512)
    out = _moe(x2, hp, top_e, gate, rank, counts, w_gate_up[0], b_gate_up[0], w_down[0],
               b_down[0], final_norm_g, tb=512, tm=256)
    return out.reshape(b, s, d)
```

```python
import functools
import math

import jax
import jax.numpy as jnp
from jax import lax
from jax.experimental import pallas as pl
from jax.experimental.pallas import tpu as pltpu

F32 = jnp.float32
BF16 = jnp.bfloat16

LANES = 128

D_MODEL = 1024
EPS = 1e-6
ROPE_THETA = 500000.0
MLA_HEADS = 8
MLA_Q_LORA = 256
MLA_KV_LORA = 128
MLA_NOPE = 64
MLA_ROPE = 32
MLA_V = 64
DSA_HEADS = 8
DSA_HEAD_DIM = 64
DSA_ROT = 16
IDX_HEADS = 8
IDX_DIM = 64
TOPK_MAX = 256
N_EXPERTS = 32
TOP_K = 4
D_FF = 1024
SWIGLU_LIMIT = 7.0
SWIGLU_ALPHA = 1.702

VMEM_LIMIT = 56 * 1024 * 1024

_OFF_CQ = 0
_OFF_CKV = _OFF_CQ + MLA_Q_LORA
_OFF_KPE = _OFF_CKV + MLA_KV_LORA
_OFF_QB = _OFF_KPE + MLA_ROPE
_OFF_KB = _OFF_QB + DSA_HEADS * DSA_HEAD_DIM
_OFF_VB = _OFF_KB + DSA_HEAD_DIM
_OFF_QI = _OFF_VB + DSA_HEAD_DIM
_OFF_KI = _OFF_QI + IDX_HEADS * IDX_DIM
_OFF_WI = _OFF_KI + IDX_DIM
_OFF_GA = _OFF_WI + IDX_HEADS
_OFF_GB = _OFF_GA + D_MODEL
_D_IN = _OFF_GB + D_MODEL


def _cparams(sem):
    return pltpu.CompilerParams(dimension_semantics=sem, vmem_limit_bytes=VMEM_LIMIT)


def _rope_group(xg, c, slo, shi, shift):
    return (xg * c + pltpu.roll(xg, LANES - shift, 1) * slo
            + pltpu.roll(xg, shift, 1) * shi)


def _in_proj_body(x_ref, g_ref, w1_ref, w2_ref, w3_ref, w5_ref, qng_ref, wuq_ref,
                  kvng_ref, wuk_ref, wuv_ref, cm_ref, slm_ref, shm_ref, cd_ref,
                  sld_ref, shd_ref,
                  qm_ref, km_ref, vm_ref, qd_ref, qi_ref, kd_ref, ki_ref, vd_ref,
                  wi_ref, ga_ref, gb_ref):
    x = x_ref[...]
    h = x * lax.rsqrt(jnp.mean(x * x, axis=-1, keepdims=True) + EPS) * g_ref[...]
    hb = h.astype(BF16)

    cm, slm, shm = cm_ref[...], slm_ref[...], shm_ref[...]
    cd, sld, shd = cd_ref[...], sld_ref[...], shd_ref[...]

    z1 = jnp.dot(hb, w1_ref[...], preferred_element_type=F32)
    cq = z1[:, 0:MLA_Q_LORA]
    cqn = cq * lax.rsqrt(jnp.mean(cq * cq, axis=-1, keepdims=True) + EPS) * qng_ref[...]
    q = jnp.dot(cqn.astype(BF16), wuq_ref[...], preferred_element_type=F32)
    q_scale = (MLA_NOPE + MLA_ROPE) ** -0.5
    for j in range(MLA_HEADS):
        qg = _rope_group(q[:, j * LANES:(j + 1) * LANES], cm, slm, shm, MLA_ROPE // 2)
        qm_ref[:, j * LANES:(j + 1) * LANES] = (qg * q_scale).astype(BF16)

    ckv = z1[:, MLA_Q_LORA:MLA_Q_LORA + MLA_KV_LORA]
    ckvn = (ckv * lax.rsqrt(jnp.mean(ckv * ckv, axis=-1, keepdims=True) + EPS)
            * kvng_ref[...]).astype(BF16)
    kpe = _rope_group(z1[:, MLA_Q_LORA + MLA_KV_LORA:], cm, slm, shm, MLA_ROPE // 2)
    kn = jnp.dot(ckvn, wuk_ref[...], preferred_element_type=F32)
    for j in range(MLA_HEADS):
        km_ref[:, j * LANES:(j + 1) * LANES] = (kn[:, j * LANES:(j + 1) * LANES] + kpe).astype(BF16)
    vm_ref[...] = jnp.dot(ckvn, wuv_ref[...], preferred_element_type=F32).astype(BF16)

    z2 = jnp.dot(hb, w2_ref[...], preferred_element_type=F32)
    d_scale = DSA_HEAD_DIM ** -0.5
    for j in range(DSA_HEADS // 2):
        g = _rope_group(z2[:, j * LANES:(j + 1) * LANES], cd, sld, shd, DSA_ROT // 2) * d_scale
        qd_ref[2 * j] = g[:, :DSA_HEAD_DIM].astype(BF16)
        qd_ref[2 * j + 1] = g[:, DSA_HEAD_DIM:].astype(BF16)
    base = DSA_HEADS * DSA_HEAD_DIM
    for j in range(IDX_HEADS // 2):
        g = _rope_group(z2[:, base + j * LANES:base + (j + 1) * LANES], cd, sld, shd, DSA_ROT // 2)
        qi_ref[2 * j] = g[:, :IDX_DIM].astype(BF16)
        qi_ref[2 * j + 1] = g[:, IDX_DIM:].astype(BF16)

    z3 = jnp.dot(hb, w3_ref[...], preferred_element_type=F32)
    kb = _rope_group(z3[:, :LANES], cd, sld, shd, DSA_ROT // 2)
    kd_ref[...] = kb[:, :DSA_HEAD_DIM].astype(BF16)
    ki_ref[...] = kb[:, DSA_HEAD_DIM:].astype(BF16)
    vd_ref[...] = z3[:, LANES:LANES + DSA_HEAD_DIM].astype(BF16)
    w_scale = IDX_HEADS ** -0.5 * IDX_DIM ** -0.5
    wi_ref[...] = z3[:, LANES + DSA_HEAD_DIM:LANES + DSA_HEAD_DIM + IDX_HEADS] * w_scale

    z5 = jnp.dot(hb, w5_ref[...], preferred_element_type=F32)
    ga_ref[...] = jax.nn.sigmoid(z5[:, :D_MODEL]).astype(BF16)
    gb_ref[...] = jax.nn.sigmoid(z5[:, D_MODEL:]).astype(BF16)


def _rope_tables(pos, rot_dim, lane_of_x1, period):
    half = rot_dim // 2
    inv_freq = ROPE_THETA ** (-jnp.arange(half, dtype=F32) / half)
    ang = pos.astype(F32)[:, None] * inv_freq
    cos, sin = jnp.cos(ang), jnp.sin(ang)
    n = pos.shape[0]
    reps = LANES // period
    pad_l = lane_of_x1
    pad_r = period - lane_of_x1 - rot_dim

    def pattern(a, b, fill):
        blk = jnp.concatenate([jnp.full((n, pad_l), fill, F32), a, b,
                               jnp.full((n, pad_r), fill, F32)], axis=1)
        return jnp.tile(blk, (1, reps))

    zeros = jnp.zeros_like(sin)
    c = pattern(cos, cos, 1.0)
    slo = pattern(-sin, zeros, 0.0)
    shi = pattern(zeros, sin, 0.0)
    return c, slo, shi


def _head_cols(w, n_heads, widths, total):
    k = w.shape[0]
    per = sum(widths)
    w = w.reshape(k, n_heads, per)
    return jnp.pad(w, ((0, 0), (0, 0), (0, total - per))).reshape(k, n_heads * total)


def _in_proj(x2, pos, attn_norm_g, w_in, q_norm_g, w_uq, kv_norm_g, w_ukv, tm):
    n = x2.shape[0]
    d = D_MODEL
    zc = lambda k: jnp.zeros((d, k), F32)
    w1 = jnp.concatenate([w_in[:, _OFF_CQ:_OFF_KPE], zc(MLA_NOPE), w_in[:, _OFF_KPE:_OFF_QB],
                          zc(LANES - MLA_NOPE - MLA_ROPE)], axis=1).astype(BF16)
    w2 = jnp.concatenate([w_in[:, _OFF_QB:_OFF_KB], w_in[:, _OFF_QI:_OFF_KI]], axis=1).astype(BF16)
    w3 = jnp.concatenate([w_in[:, _OFF_KB:_OFF_VB], w_in[:, _OFF_KI:_OFF_WI],
                          w_in[:, _OFF_VB:_OFF_QI], w_in[:, _OFF_WI:_OFF_GA],
                          zc(LANES - DSA_HEAD_DIM - IDX_HEADS)], axis=1).astype(BF16)
    w5 = w_in[:, _OFF_GA:].astype(BF16)
    wuq = _head_cols(w_uq, MLA_HEADS, (MLA_NOPE, MLA_ROPE), LANES).astype(BF16)
    w_ukv3 = w_ukv.reshape(MLA_KV_LORA, MLA_HEADS, MLA_NOPE + MLA_V)
    wuk = jnp.pad(w_ukv3[:, :, :MLA_NOPE], ((0, 0), (0, 0), (0, LANES - MLA_NOPE))
                  ).reshape(MLA_KV_LORA, MLA_HEADS * LANES).astype(BF16)
    wuv = w_ukv3[:, :, MLA_NOPE:].reshape(MLA_KV_LORA, MLA_HEADS * MLA_V).astype(BF16)
    cm, slm, shm = _rope_tables(pos, MLA_ROPE, MLA_NOPE, LANES)
    cd, sld, shd = _rope_tables(pos, DSA_ROT, 0, DSA_HEAD_DIM)

    row = lambda w_: pl.BlockSpec((tm, w_), lambda i: (i, 0))
    full = lambda a: pl.BlockSpec(a.shape, lambda i: (0,) * a.ndim)
    hm = pl.BlockSpec((DSA_HEADS, tm, DSA_HEAD_DIM), lambda i: (0, i, 0))
    g2 = attn_norm_g.reshape(1, d)
    qng = q_norm_g.reshape(1, -1)
    kvng = kv_norm_g.reshape(1, -1)
    consts = (g2, w1, w2, w3, w5, qng, wuq, kvng, wuk, wuv)
    out_shape = (
        jax.ShapeDtypeStruct((n, MLA_HEADS * LANES), BF16),
        jax.ShapeDtypeStruct((n, MLA_HEADS * LANES), BF16),
        jax.ShapeDtypeStruct((n, MLA_HEADS * MLA_V), BF16),
        jax.ShapeDtypeStruct((DSA_HEADS, n, DSA_HEAD_DIM), BF16),
        jax.ShapeDtypeStruct((IDX_HEADS, n, IDX_DIM), BF16),
        jax.ShapeDtypeStruct((n, DSA_HEAD_DIM), BF16),
        jax.ShapeDtypeStruct((n, IDX_DIM), BF16),
        jax.ShapeDtypeStruct((n, DSA_HEAD_DIM), BF16),
        jax.ShapeDtypeStruct((n, IDX_HEADS), F32),
        jax.ShapeDtypeStruct((n, D_MODEL), BF16),
        jax.ShapeDtypeStruct((n, D_MODEL), BF16),
    )
    out_specs = (row(MLA_HEADS * LANES), row(MLA_HEADS * LANES), row(MLA_HEADS * MLA_V),
                 hm, hm, row(DSA_HEAD_DIM), row(IDX_DIM), row(DSA_HEAD_DIM), row(IDX_HEADS),
                 row(D_MODEL), row(D_MODEL))
    return pl.pallas_call(
        _in_proj_body,
        grid=(n // tm,),
        in_specs=[row(d)] + [full(a) for a in consts] + [row(LANES)] * 6,
        out_specs=out_specs,
        out_shape=out_shape,
        compiler_params=_cparams(("parallel",)),
        name="in_proj",
    )(x2, *consts, cm, slm, shm, cd, sld, shd)


NEG_BIG = -1e30
_NT = (((1,), (1,)), ((), ()))


def _mla_body(q_ref, k_ref, v_ref, o_ref, *, tq, tk):
    qi = pl.program_id(2)
    n_sub = tq // tk
    outs = []
    for hh in range(2):
        q = q_ref[:, hh * LANES:(hh + 1) * LANES]

        def step(j, carry, masked, q=q, hh=hh):
            m, l, acc = carry
            start = pl.multiple_of(j * tk, tk)
            ks = k_ref[pl.ds(start, tk), hh * LANES:(hh + 1) * LANES]
            vs = v_ref[pl.ds(start, tk), :]
            s = lax.dot_general(q, ks, _NT, preferred_element_type=F32)
            if masked:
                row = qi * tq + lax.broadcasted_iota(jnp.int32, (tq, tk), 0)
                col = j * tk + lax.broadcasted_iota(jnp.int32, (tq, tk), 1)
                s = jnp.where(col <= row, s, NEG_BIG)
            m_new = jnp.maximum(m, jnp.max(s, axis=-1, keepdims=True))
            alpha = jnp.exp(m - m_new)
            p = jnp.exp(s - m_new)
            l = alpha * l + jnp.sum(p, axis=-1, keepdims=True)
            acc = alpha * acc + jnp.dot(p.astype(BF16), vs, preferred_element_type=F32)
            return m_new, l, acc

        carry = (jnp.full((tq, 1), NEG_BIG, F32), jnp.zeros((tq, 1), F32),
                 jnp.zeros((tq, LANES), F32))
        carry = lax.fori_loop(0, qi * n_sub, functools.partial(step, masked=False), carry)
        for dd in range(n_sub):
            carry = step(qi * n_sub + dd, carry, True)
        m, l, acc = carry
        outs.append(acc / l)
    lane = lax.broadcasted_iota(jnp.int32, (tq, LANES), 1)
    o_ref[...] = jnp.where(lane < MLA_V, outs[0], outs[1]).astype(BF16)


def _mla_attention(qm, km, vm, b, s, tq, tk):
    n = b * s
    nq = s // tq
    return pl.pallas_call(
        functools.partial(_mla_body, tq=tq, tk=tk),
        grid=(b, MLA_HEADS // 2, nq),
        in_specs=[
            pl.BlockSpec((tq, 2 * LANES), lambda bi, hp, qi: (bi * nq + qi, hp)),
            pl.BlockSpec((s, 2 * LANES), lambda bi, hp, qi: (bi, hp)),
            pl.BlockSpec((s, 2 * MLA_V), lambda bi, hp, qi: (bi, hp)),
        ],
        out_specs=pl.BlockSpec((tq, 2 * MLA_V), lambda bi, hp, qi: (bi * nq + qi, hp)),
        out_shape=jax.ShapeDtypeStruct((n, MLA_HEADS * MLA_V), BF16),
        compiler_params=_cparams(("parallel", "parallel", "arbitrary")),
        name="mla_attention",
    )(qm, km, vm)


INT_MIN = -2 ** 31
KEY_NEG_INF = (0xFF800000 - 2 ** 32) ^ 0x7FFFFFFF


def _sortable_key(score):
    bits = pltpu.bitcast(score, jnp.int32)
    return bits ^ ((bits >> 31) & 0x7FFFFFFF)


def _dsa_body(qi_ref, qd_ref, ki_ref, kd_ref, vd_ref, w_ref, o_ref, keys_ref, *, tq, tk, topk):
    qb = pl.program_id(1)
    n_tiles = (qb * tq + tq + tk - 1) // tk
    q_pos = qb * tq + lax.broadcasted_iota(jnp.int32, (tq, 1), 0)

    qidx = qi_ref[...].reshape(IDX_HEADS * tq, IDX_DIM)
    w = w_ref[...]
    wcols = [jnp.broadcast_to(w[:, h:h + 1], (tq, tk)) for h in range(IDX_HEADS)]

    def score_tile(j, _):
        start = pl.multiple_of(j * tk, tk)
        kt = ki_ref[pl.ds(start, tk), :]
        sh = lax.dot_general(qidx, kt, _NT, preferred_element_type=F32)
        sc = jnp.zeros((tq, tk), F32)
        for h in range(IDX_HEADS):
            sc = sc + jnp.maximum(sh[h * tq:(h + 1) * tq], 0.0) * wcols[h]
        col = j * tk + lax.broadcasted_iota(jnp.int32, (tq, tk), 1)
        sc = jnp.where(col <= q_pos, sc, -jnp.inf)
        keys_ref[:, pl.ds(start, tk)] = _sortable_key(sc)
        return 0

    lax.fori_loop(0, n_tiles, score_tile, 0)

    def count_ge(cand):
        def body(j, cnt):
            start = pl.multiple_of(j * tk, tk)
            kt = keys_ref[:, pl.ds(start, tk)]
            hit = (kt >= cand).astype(jnp.int32)
            for c in range(tk // LANES):
                cnt = cnt + hit[:, c * LANES:(c + 1) * LANES]
            return cnt
        cnt = lax.fori_loop(0, n_tiles, body, jnp.zeros((tq, LANES), jnp.int32))
        return jnp.sum(cnt, axis=-1, keepdims=True)

    def bit_step(i, t):
        cand = t + jnp.left_shift(jnp.int32(1), 31 - i)
        return jnp.where(count_ge(cand) >= topk, cand, t)

    thr = lax.fori_loop(0, 32, bit_step, jnp.full((tq, 1), INT_MIN, jnp.int32))
    thr = jnp.maximum(thr, KEY_NEG_INF + 1)

    qd = qd_ref[...].reshape(DSA_HEADS * tq, DSA_HEAD_DIM)

    def attn_tile(j, carry):
        m, l, acc = carry
        start = pl.multiple_of(j * tk, tk)
        kt = kd_ref[pl.ds(start, tk), :]
        vt = vd_ref[pl.ds(start, tk), :]
        sel = keys_ref[:, pl.ds(start, tk)] >= thr
        s = lax.dot_general(qd, kt, _NT, preferred_element_type=F32)
        s = jnp.where(sel[None], s.reshape(DSA_HEADS, tq, tk), NEG_BIG)
        m_new = jnp.maximum(m, jnp.max(s, axis=-1, keepdims=True))
        alpha = jnp.exp(m - m_new)
        p = jnp.exp(s - m_new)
        l = alpha * l + jnp.sum(p, axis=-1, keepdims=True)
        pv = jnp.dot(p.reshape(DSA_HEADS * tq, tk).astype(BF16), vt, preferred_element_type=F32)
        acc = alpha * acc + pv.reshape(DSA_HEADS, tq, DSA_HEAD_DIM)
        return m_new, l, acc

    init = (jnp.full((DSA_HEADS, tq, 1), NEG_BIG, F32), jnp.zeros((DSA_HEADS, tq, 1), F32),
            jnp.zeros((DSA_HEADS, tq, DSA_HEAD_DIM), F32))
    m, l, acc = lax.fori_loop(0, n_tiles, attn_tile, init)
    out = acc / l
    for h in range(DSA_HEADS):
        o_ref[:, h * DSA_HEAD_DIM:(h + 1) * DSA_HEAD_DIM] = out[h].astype(BF16)


def _dsa_attention(qi, qd, ki, kd, vd, wi, b, s, tq, tk):
    n = b * s
    nq = s // tq
    topk = min(TOPK_MAX, s // 4)
    hm = pl.BlockSpec((DSA_HEADS, tq, DSA_HEAD_DIM), lambda bi, qb: (0, bi * nq + qb, 0))
    kv = pl.BlockSpec((s, DSA_HEAD_DIM), lambda bi, qb: (bi, 0))
    return pl.pallas_call(
        functools.partial(_dsa_body, tq=tq, tk=tk, topk=topk),
        grid=(b, nq),
        in_specs=[hm, hm, kv, kv, kv,
                  pl.BlockSpec((tq, IDX_HEADS), lambda bi, qb: (bi * nq + qb, 0))],
        out_specs=pl.BlockSpec((tq, DSA_HEADS * DSA_HEAD_DIM), lambda bi, qb: (bi * nq + qb, 0)),
        out_shape=jax.ShapeDtypeStruct((n, DSA_HEADS * DSA_HEAD_DIM), BF16),
        scratch_shapes=[pltpu.VMEM((tq, s), jnp.int32)],
        compiler_params=_cparams(("parallel", "arbitrary")),
        name="dsa_attention",
    )(qi, qd, ki, kd, vd, wi)


HALF_D = D_MODEL // 2
RUN_ALIGN = 8


def _pack_bf16_pairs(y):
    r = pltpu.bitcast(y.astype(BF16).astype(F32), jnp.uint32)
    return r[:, :HALF_D] | (r[:, HALF_D:] >> 16)


def _unpack_bf16_pairs(p):
    hi = pltpu.bitcast(p & jnp.uint32(0xFFFF0000), F32)
    lo = pltpu.bitcast(p << 16, F32)
    return jnp.concatenate([hi, lo], axis=1)


def _split_bf16(a):
    hi = a.astype(BF16)
    lo = (a - hi.astype(F32)).astype(BF16)
    return hi, lo


def _merge_body(x_ref, ya_ref, yb_ref, ga_ref, gb_ref, wa_ref, wb_ref, wo_ref, g_ref,
                rwh_ref, rwl_ref, rb_ref, x2_ref, h_ref, logit_ref):
    ma = jnp.dot(ya_ref[...], wa_ref[...], preferred_element_type=F32)
    mb = jnp.dot(yb_ref[...], wb_ref[...], preferred_element_type=F32)
    merged = ga_ref[...].astype(F32) * ma + gb_ref[...].astype(F32) * mb
    x2 = x_ref[...] + jnp.dot(merged.astype(BF16), wo_ref[...], preferred_element_type=F32)
    x2_ref[...] = x2
    h = x2 * lax.rsqrt(jnp.mean(x2 * x2, axis=-1, keepdims=True) + EPS) * g_ref[...]
    hh, hl = _split_bf16(h)
    h_ref[...] = hh
    logit_ref[...] = (jnp.dot(hh, rwh_ref[...], preferred_element_type=F32)
                      + jnp.dot(hh, rwl_ref[...], preferred_element_type=F32)
                      + jnp.dot(hl, rwh_ref[...], preferred_element_type=F32)) + rb_ref[...]


def _merge(x2d, y_a, y_b, ga, gb, w_ba, w_bb, w_out, ffn_norm_g, router_w, router_b, tm):
    n, d = x2d.shape
    rw = jnp.pad(router_w, ((0, 0), (0, LANES - N_EXPERTS)))
    rwh = rw.astype(BF16)
    rwl = (rw - rwh.astype(F32)).astype(BF16)
    rb = jnp.pad(router_b, (0, LANES - N_EXPERTS), constant_values=NEG_BIG).reshape(1, LANES)
    consts = (w_ba.astype(BF16), w_bb.astype(BF16), w_out.astype(BF16), ffn_norm_g.reshape(1, d),
              rwh, rwl, rb)
    row = lambda w_: pl.BlockSpec((tm, w_), lambda i: (i, 0))
    full = lambda a: pl.BlockSpec(a.shape, lambda i: (0,) * a.ndim)
    out_shape = (
        jax.ShapeDtypeStruct((n, d), F32),
        jax.ShapeDtypeStruct((n, d), BF16),
        jax.ShapeDtypeStruct((n, LANES), F32),
    )
    return pl.pallas_call(
        _merge_body,
        grid=(n // tm,),
        in_specs=[row(d), row(HALF_D), row(HALF_D), row(d), row(d)] + [full(a) for a in consts],
        out_specs=(row(d), row(d), row(LANES)),
        out_shape=out_shape,
        compiler_params=_cparams(("parallel",)),
        name="merge",
    )(x2d, y_a, y_b, ga, gb, *consts)


def _router_body(logit_ref, upper_ref, e_ref, gate_ref, lp_ref, bc_ref, carry_out_ref, cnt_ref,
                 carry_ref, *, tm):
    i = pl.program_id(0)

    @pl.when(i == 0)
    def _():
        carry_ref[...] = jnp.zeros_like(carry_ref)

    lane = lax.broadcasted_iota(jnp.int32, (tm, LANES), 1)
    work = logit_ref[...]
    experts, vals = [], []
    onehot = jnp.zeros((tm, LANES), F32)
    for _ in range(TOP_K):
        mx = jnp.max(work, axis=-1, keepdims=True)
        idx = jnp.min(jnp.where(work == mx, lane, LANES), axis=-1, keepdims=True)
        hit = lane == idx
        experts.append(idx)
        vals.append(mx)
        onehot = onehot + hit.astype(F32)
        work = jnp.where(hit, -jnp.inf, work)
    ex = [jnp.exp(v - vals[0]) for v in vals]
    denom = ex[0] + ex[1] + ex[2] + ex[3]
    for r in range(TOP_K):
        e_ref[:, r:r + 1] = experts[r]
        gate_ref[:, r:r + 1] = ex[r] / denom

    rr = lax.broadcasted_iota(jnp.int32, (tm, tm), 0)
    cc = lax.broadcasted_iota(jnp.int32, (tm, tm), 1)
    lower = (cc < rr).astype(BF16)
    prefix = jnp.dot(lower, onehot.astype(BF16), preferred_element_type=F32)
    bc = jnp.sum(onehot, axis=0, keepdims=True)
    bc = jnp.floor((bc + (RUN_ALIGN - 1)) * (1.0 / RUN_ALIGN)) * RUN_ALIGN
    bc8 = jnp.broadcast_to(bc, (8, LANES))
    boff = jnp.dot(bc8.astype(BF16), upper_ref[...], preferred_element_type=F32)[0:1, :]
    local = prefix + boff
    for r in range(TOP_K):
        lp = jnp.sum(jnp.where(lane == experts[r], local, 0.0), axis=-1, keepdims=True)
        lp_ref[:, r:r + 1] = lp.astype(jnp.int32)
    bc_ref[0] = bc8
    carry_out_ref[0] = carry_ref[...]
    total = carry_ref[...] + bc8
    carry_ref[...] = total
    cnt_ref[...] = total


def _router(logits, tm):
    assert tm <= 256
    n = logits.shape[0]
    nblk = n // tm
    upper = (jnp.arange(LANES)[:, None] < jnp.arange(LANES)[None, :]).astype(BF16)
    row = lambda w_: pl.BlockSpec((tm, w_), lambda i: (i, 0))
    blk = pl.BlockSpec((1, 8, LANES), lambda i: (i, 0, 0))
    out_shape = (
        jax.ShapeDtypeStruct((n, TOP_K), jnp.int32),
        jax.ShapeDtypeStruct((n, TOP_K), F32),
        jax.ShapeDtypeStruct((n, TOP_K), jnp.int32),
        jax.ShapeDtypeStruct((nblk, 8, LANES), F32),
        jax.ShapeDtypeStruct((nblk, 8, LANES), F32),
        jax.ShapeDtypeStruct((8, LANES), F32),
    )
    return pl.pallas_call(
        functools.partial(_router_body, tm=tm),
        grid=(nblk,),
        in_specs=[row(LANES), pl.BlockSpec((LANES, LANES), lambda i: (0, 0))],
        out_specs=(row(TOP_K), row(TOP_K), row(TOP_K), blk, blk,
                   pl.BlockSpec((8, LANES), lambda i: (0, 0))),
        out_shape=out_shape,
        scratch_shapes=[pltpu.VMEM((8, LANES), F32)],
        compiler_params=_cparams(("arbitrary",)),
        name="router",
    )(logits, upper)


def _local_rows(tm):
    return TOP_K * tm + N_EXPERTS * RUN_ALIGN


def _for_each_run_chunk(tbl_ref, tm, fn):
    sizes = [s for s in (1 << k for k in range(tm.bit_length())) if RUN_ALIGN <= s <= tm]

    def per_expert(e, _):
        length = tbl_ref[0, 0, e]
        src = tbl_ref[0, 0, N_EXPERTS + e]
        dst = tbl_ref[0, 0, 2 * N_EXPERTS + e]
        for size in sizes:

            @pl.when((length & size) != 0)
            def _(size=size):
                off = length & (size - 1)
                fn(pl.multiple_of(src + off, RUN_ALIGN), pl.multiple_of(dst + off, RUN_ALIGN), size)
        return 0

    lax.fori_loop(0, N_EXPERTS, per_expert, 0)


def _dispatch_body(tbl_ref, lpt_ref, h_ref, xbuf_in_ref, xbuf_ref, sorted_ref, sem, *, tm):
    del xbuf_in_ref
    h = h_ref[...]
    lpt = lpt_ref[0]
    chunk = tm
    for c in range(_local_rows(tm) // chunk):
        r_idx = c * chunk + lax.broadcasted_iota(jnp.int32, (chunk, tm), 0)
        sel = jnp.zeros((chunk, tm), F32)
        for r in range(TOP_K):
            sel = sel + (r_idx == lpt[r:r + 1, :]).astype(F32)
        rows = jnp.dot(sel.astype(BF16), h, preferred_element_type=F32)
        sorted_ref[c * chunk:(c + 1) * chunk, :] = _pack_bf16_pairs(rows)

    def copy(local_row, global_row, size):
        return pltpu.make_async_copy(sorted_ref.at[pl.ds(local_row, size)],
                                     xbuf_ref.at[pl.ds(global_row, size)], sem)

    _for_each_run_chunk(tbl_ref, tm, lambda s, d, n: copy(s, d, n).start())
    _for_each_run_chunk(tbl_ref, tm, lambda s, d, n: copy(s, d, n).wait())


def _dispatch(h2, lpt, tbl, n_rows, tm):
    n = h2.shape[0]
    xbuf0 = jnp.zeros((n_rows, HALF_D), jnp.uint32)
    return pl.pallas_call(
        functools.partial(_dispatch_body, tm=tm),
        grid=(n // tm,),
        in_specs=[pl.BlockSpec((1, 1, LANES), lambda i: (i, 0, 0), memory_space=pltpu.SMEM),
                  pl.BlockSpec((1, TOP_K, tm), lambda i: (i, 0, 0)),
                  pl.BlockSpec((tm, D_MODEL), lambda i: (i, 0)),
                  pl.BlockSpec(memory_space=pl.ANY)],
        out_specs=pl.BlockSpec(memory_space=pl.ANY),
        out_shape=jax.ShapeDtypeStruct((n_rows, HALF_D), jnp.uint32),
        scratch_shapes=[pltpu.VMEM((_local_rows(tm), HALF_D), jnp.uint32),
                        pltpu.SemaphoreType.DMA],
        input_output_aliases={3: 0},
        compiler_params=_cparams(("arbitrary",)),
        name="moe_dispatch",
    )(tbl, lpt, h2, xbuf0)


def _expert_body(blk_e_ref, n_used_ref, x_ref, wg_ref, wl_ref, bg_ref, bl_ref, wd_ref, bd_ref,
                 y_ref):
    del blk_e_ref
    i = pl.program_id(0)

    @pl.when(i < n_used_ref[0])
    def _():
        xb = _unpack_bf16_pairs(x_ref[...]).astype(BF16)
        glu = jnp.dot(xb, wg_ref[0], preferred_element_type=F32) + bg_ref[0]
        lin = jnp.dot(xb, wl_ref[0], preferred_element_type=F32) + bl_ref[0]
        glu = jnp.minimum(glu, SWIGLU_LIMIT)
        lin = jnp.clip(lin, -SWIGLU_LIMIT, SWIGLU_LIMIT)
        act = glu * jax.nn.sigmoid(SWIGLU_ALPHA * glu) * (lin + 1.0)
        y = jnp.dot(act.astype(BF16), wd_ref[0], preferred_element_type=F32) + bd_ref[0]
        y_ref[...] = _pack_bf16_pairs(y)

    @pl.when(i >= n_used_ref[0])
    def _():
        y_ref[...] = jnp.zeros_like(y_ref)


def _experts(xbuf, blk_e, n_used, w_gate_up, b_gate_up, w_down, b_down, tb):
    n_rows = xbuf.shape[0]
    d = D_MODEL
    wgu = w_gate_up.astype(BF16)
    wd = w_down.astype(BF16)
    bgu = b_gate_up.reshape(N_EXPERTS, 1, 2 * D_FF)
    bd = b_down.reshape(N_EXPERTS, 1, d)
    grid_spec = pltpu.PrefetchScalarGridSpec(
        num_scalar_prefetch=2,
        grid=(n_rows // tb,),
        in_specs=[
            pl.BlockSpec((tb, HALF_D), lambda i, be, nu: (i, 0)),
            pl.BlockSpec((1, d, D_FF), lambda i, be, nu: (be[i], 0, 0)),
            pl.BlockSpec((1, d, D_FF), lambda i, be, nu: (be[i], 0, 1)),
            pl.BlockSpec((1, 1, D_FF), lambda i, be, nu: (be[i], 0, 0)),
            pl.BlockSpec((1, 1, D_FF), lambda i, be, nu: (be[i], 0, 1)),
            pl.BlockSpec((1, D_FF, d), lambda i, be, nu: (be[i], 0, 0)),
            pl.BlockSpec((1, 1, d), lambda i, be, nu: (be[i], 0, 0)),
        ],
        out_specs=pl.BlockSpec((tb, HALF_D), lambda i, be, nu: (i, 0)),
    )
    return pl.pallas_call(
        _expert_body,
        grid_spec=grid_spec,
        out_shape=jax.ShapeDtypeStruct((n_rows, HALF_D), jnp.uint32),
        compiler_params=_cparams(("arbitrary",)),
        name="moe_experts",
    )(blk_e, n_used, xbuf, wgu, wgu, bgu, bgu, wd, bd)


def _combine_body(tbl_ref, x2_ref, gate_ref, lp_ref, g_ref, ybuf_ref, o_ref, ys_ref, sem, *, tm):
    n_local = _local_rows(tm)
    ys_ref[TOP_K * tm:, :] = jnp.zeros((n_local - TOP_K * tm, HALF_D), jnp.uint32)

    def copy(local_row, global_row, size):
        return pltpu.make_async_copy(ybuf_ref.at[pl.ds(global_row, size)],
                                     ys_ref.at[pl.ds(local_row, size)], sem)

    _for_each_run_chunk(tbl_ref, tm, lambda s, d, n: copy(s, d, n).start())
    _for_each_run_chunk(tbl_ref, tm, lambda s, d, n: copy(s, d, n).wait())

    gate = gate_ref[...]
    lp = lp_ref[...]
    col = lax.broadcasted_iota(jnp.int32, (tm, n_local), 1)
    gmat = jnp.zeros((tm, n_local), F32)
    for r in range(TOP_K):
        gmat = gmat + jnp.where(col == lp[:, r:r + 1], gate[:, r:r + 1], 0.0)
    g_hi, g_lo = _split_bf16(gmat)
    ys = _unpack_bf16_pairs(ys_ref[...]).astype(BF16)
    out = (x2_ref[...] + jnp.dot(g_hi, ys, preferred_element_type=F32)
           + jnp.dot(g_lo, ys, preferred_element_type=F32))
    o_ref[...] = out * lax.rsqrt(jnp.mean(out * out, axis=-1, keepdims=True) + EPS) * g_ref[...]


def _combine(x2, gate, lp, tbl, ybuf, final_norm_g, tm):
    n, d = x2.shape
    return pl.pallas_call(
        functools.partial(_combine_body, tm=tm),
        grid=(n // tm,),
        in_specs=[pl.BlockSpec((1, 1, LANES), lambda i: (i, 0, 0), memory_space=pltpu.SMEM),
                  pl.BlockSpec((tm, d), lambda i: (i, 0)),
                  pl.BlockSpec((tm, TOP_K), lambda i: (i, 0)),
                  pl.BlockSpec((tm, TOP_K), lambda i: (i, 0)),
                  pl.BlockSpec((1, d), lambda i: (0, 0)),
                  pl.BlockSpec(memory_space=pl.ANY)],
        out_specs=pl.BlockSpec((tm, d), lambda i: (i, 0)),
        out_shape=jax.ShapeDtypeStruct((n, d), F32),
        scratch_shapes=[pltpu.VMEM((_local_rows(tm), HALF_D), jnp.uint32),
                        pltpu.SemaphoreType.DMA],
        compiler_params=_cparams(("arbitrary",)),
        name="moe_combine",
    )(tbl, x2, gate, lp, final_norm_g.reshape(1, d), ybuf)


def _moe(x2, h2, logits, w_gate_up, b_gate_up, w_down, b_down, final_norm_g, tb, tm):
    n = x2.shape[0]
    nblk = n // tm
    max_rows = n * TOP_K + nblk * N_EXPERTS * (RUN_ALIGN - 1)
    n_blk = -(-max_rows // tb) + N_EXPERTS
    top_e, gate, lp, bcount, before, counts = _router(logits, tm)
    del top_e
    cnt = counts[0, :N_EXPERTS].astype(jnp.int32)
    padded = (cnt + tb - 1) // tb * tb
    pad_end = jnp.cumsum(padded)
    pad_start = pad_end - padded
    blk_first = jnp.arange(n_blk, dtype=jnp.int32) * tb
    blk_e = jnp.minimum(jnp.sum((pad_end[None, :] <= blk_first[:, None]).astype(jnp.int32), axis=1),
                        N_EXPERTS - 1)
    n_used = (pad_end[-1:] // tb).astype(jnp.int32)
    run_len = bcount[:, 0, :N_EXPERTS].astype(jnp.int32)
    run_src = jnp.cumsum(run_len, axis=1) - run_len
    run_dst = pad_start[None, :] + before[:, 0, :N_EXPERTS].astype(jnp.int32)
    tbl = jnp.concatenate([run_len, run_src, run_dst,
                           jnp.zeros((nblk, LANES - 3 * N_EXPERTS), jnp.int32)], axis=1)
    tbl = tbl.reshape(nblk, 1, LANES)
    lpt = lp.reshape(nblk, tm, TOP_K).transpose(0, 2, 1)
    xbuf = _dispatch(h2, lpt, tbl, n_blk * tb, tm)
    ybuf = _experts(xbuf, blk_e, n_used, w_gate_up, b_gate_up, w_down, b_down, tb)
    return _combine(x2, gate, lp, tbl, ybuf, final_norm_g, tm)


def kernel(x, positions, attn_norm_g, w_in, mla_q_norm_g, mla_w_uq, mla_kv_norm_g, mla_w_ukv,
           w_branch_mla, w_branch_dsa, w_out, ffn_norm_g, router_w, router_b,
           w_gate_up, b_gate_up, w_down, b_down, final_norm_g):
    b, s, d = x.shape
    n = b * s
    (qm, km, vm, qd, qi, kd, ki, vd, wi, ga, gb) = _in_proj(
        x.reshape(n, d), positions.reshape(n), attn_norm_g[0], w_in[0],
        mla_q_norm_g[0], mla_w_uq[0], mla_kv_norm_g[0], mla_w_ukv[0], tm=512)
    y_a = _mla_attention(qm, km, vm, b, s, tq=512, tk=512)
    y_b = _dsa_attention(qi, qd, ki, kd, vd, wi, b, s, tq=128, tk=512)
    x2, h2, logits = _merge(
        x.reshape(n, d), y_a, y_b, ga, gb, w_branch_mla[0], w_branch_dsa[0], w_out[0],
        ffn_norm_g[0], router_w[0], router_b[0], tm=512)
    out = _moe(x2, h2, logits, w_gate_up[0], b_gate_up[0], w_down[0], b_down[0], final_norm_g,
               tb=512, tm=256)
    return out.reshape(b, s, d)
```

```python
import functools
import math

import jax
import jax.numpy as jnp
from jax import lax
from jax.experimental import pallas as pl
from jax.experimental.pallas import tpu as pltpu

F32 = jnp.float32
BF16 = jnp.bfloat16

LANES = 128
LOG2_E = 1.4426950408889634

D_MODEL = 1024
EPS = 1e-6
ROPE_THETA = 500000.0
MLA_HEADS = 8
MLA_Q_LORA = 256
MLA_KV_LORA = 128
MLA_NOPE = 64
MLA_ROPE = 32
MLA_V = 64
DSA_HEADS = 8
DSA_HEAD_DIM = 64
DSA_ROT = 16
IDX_HEADS = 8
IDX_DIM = 64
TOPK_MAX = 256
N_EXPERTS = 32
TOP_K = 4
D_FF = 1024
SWIGLU_LIMIT = 7.0
SWIGLU_ALPHA = 1.702

VMEM_LIMIT = 56 * 1024 * 1024

_OFF_CQ = 0
_OFF_CKV = _OFF_CQ + MLA_Q_LORA
_OFF_KPE = _OFF_CKV + MLA_KV_LORA
_OFF_QB = _OFF_KPE + MLA_ROPE
_OFF_KB = _OFF_QB + DSA_HEADS * DSA_HEAD_DIM
_OFF_VB = _OFF_KB + DSA_HEAD_DIM
_OFF_QI = _OFF_VB + DSA_HEAD_DIM
_OFF_KI = _OFF_QI + IDX_HEADS * IDX_DIM
_OFF_WI = _OFF_KI + IDX_DIM
_OFF_GA = _OFF_WI + IDX_HEADS
_OFF_GB = _OFF_GA + D_MODEL
_D_IN = _OFF_GB + D_MODEL


def _cparams(sem):
    return pltpu.CompilerParams(dimension_semantics=sem, vmem_limit_bytes=VMEM_LIMIT)


def _rope_group(xg, c, slo, shi, shift):
    return (xg * c + pltpu.roll(xg, LANES - shift, 1) * slo
            + pltpu.roll(xg, shift, 1) * shi)


def _in_proj_body(x_ref, g_ref, w1_ref, w2_ref, w3_ref, w5_ref, qng_ref, wuq_ref,
                  kvng_ref, wuk_ref, wuv_ref, cm_ref, slm_ref, shm_ref, cd_ref,
                  sld_ref, shd_ref,
                  qm_ref, km_ref, vm_ref, qd_ref, qi_ref, kd_ref, ki_ref, vd_ref,
                  wi_ref, ga_ref, gb_ref):
    x = x_ref[...]
    h = x * lax.rsqrt(jnp.mean(x * x, axis=-1, keepdims=True) + EPS) * g_ref[...]
    hb = h.astype(BF16)

    cm, slm, shm = cm_ref[...], slm_ref[...], shm_ref[...]
    cd, sld, shd = cd_ref[...], sld_ref[...], shd_ref[...]

    z1 = jnp.dot(hb, w1_ref[...], preferred_element_type=F32)
    cq = z1[:, 0:MLA_Q_LORA]
    cqn = cq * lax.rsqrt(jnp.mean(cq * cq, axis=-1, keepdims=True) + EPS) * qng_ref[...]
    q = jnp.dot(cqn.astype(BF16), wuq_ref[...], preferred_element_type=F32)
    q_scale = (MLA_NOPE + MLA_ROPE) ** -0.5 * LOG2_E
    for j in range(MLA_HEADS):
        qg = _rope_group(q[:, j * LANES:(j + 1) * LANES], cm, slm, shm, MLA_ROPE // 2)
        qm_ref[:, j * LANES:(j + 1) * LANES] = (qg * q_scale).astype(BF16)

    ckv = z1[:, MLA_Q_LORA:MLA_Q_LORA + MLA_KV_LORA]
    ckvn = (ckv * lax.rsqrt(jnp.mean(ckv * ckv, axis=-1, keepdims=True) + EPS)
            * kvng_ref[...]).astype(BF16)
    kpe = _rope_group(z1[:, MLA_Q_LORA + MLA_KV_LORA:], cm, slm, shm, MLA_ROPE // 2)
    kn = jnp.dot(ckvn, wuk_ref[...], preferred_element_type=F32)
    for j in range(MLA_HEADS):
        km_ref[:, j * LANES:(j + 1) * LANES] = (kn[:, j * LANES:(j + 1) * LANES] + kpe).astype(BF16)
    vm_ref[...] = jnp.dot(ckvn, wuv_ref[...], preferred_element_type=F32).astype(BF16)

    z2 = jnp.dot(hb, w2_ref[...], preferred_element_type=F32)
    d_scale = DSA_HEAD_DIM ** -0.5 * LOG2_E
    for j in range(DSA_HEADS // 2):
        g = _rope_group(z2[:, j * LANES:(j + 1) * LANES], cd, sld, shd, DSA_ROT // 2) * d_scale
        qd_ref[2 * j] = g[:, :DSA_HEAD_DIM].astype(BF16)
        qd_ref[2 * j + 1] = g[:, DSA_HEAD_DIM:].astype(BF16)
    base = DSA_HEADS * DSA_HEAD_DIM
    for j in range(IDX_HEADS // 2):
        g = _rope_group(z2[:, base + j * LANES:base + (j + 1) * LANES], cd, sld, shd, DSA_ROT // 2)
        qi_ref[2 * j] = g[:, :IDX_DIM].astype(BF16)
        qi_ref[2 * j + 1] = g[:, IDX_DIM:].astype(BF16)

    z3 = jnp.dot(hb, w3_ref[...], preferred_element_type=F32)
    kb = _rope_group(z3[:, :LANES], cd, sld, shd, DSA_ROT // 2)
    kd_ref[...] = kb[:, :DSA_HEAD_DIM].astype(BF16)
    ki_ref[...] = kb[:, DSA_HEAD_DIM:].astype(BF16)
    zv = z3[:, LANES:]
    vlane = lax.broadcasted_iota(jnp.int32, zv.shape, 1)
    vd_ref[...] = jnp.where(vlane < DSA_HEAD_DIM, zv,
                            jnp.where(vlane == DSA_HEAD_DIM, 1.0, 0.0)).astype(BF16)
    w_scale = IDX_HEADS ** -0.5 * IDX_DIM ** -0.5
    wi_ref[...] = z3[:, LANES + DSA_HEAD_DIM:LANES + DSA_HEAD_DIM + IDX_HEADS] * w_scale

    z5 = jnp.dot(hb, w5_ref[...], preferred_element_type=F32)
    ga_ref[...] = jax.nn.sigmoid(z5[:, :D_MODEL]).astype(BF16)
    gb_ref[...] = jax.nn.sigmoid(z5[:, D_MODEL:]).astype(BF16)


def _rope_tables(pos, rot_dim, lane_of_x1, period):
    half = rot_dim // 2
    inv_freq = ROPE_THETA ** (-jnp.arange(half, dtype=F32) / half)
    ang = pos.astype(F32)[:, None] * inv_freq
    cos, sin = jnp.cos(ang), jnp.sin(ang)
    n = pos.shape[0]
    reps = LANES // period
    pad_l = lane_of_x1
    pad_r = period - lane_of_x1 - rot_dim

    def pattern(a, b, fill):
        blk = jnp.concatenate([jnp.full((n, pad_l), fill, F32), a, b,
                               jnp.full((n, pad_r), fill, F32)], axis=1)
        return jnp.tile(blk, (1, reps))

    zeros = jnp.zeros_like(sin)
    c = pattern(cos, cos, 1.0)
    slo = pattern(-sin, zeros, 0.0)
    shi = pattern(zeros, sin, 0.0)
    return c, slo, shi


def _head_cols(w, n_heads, widths, total):
    k = w.shape[0]
    per = sum(widths)
    w = w.reshape(k, n_heads, per)
    return jnp.pad(w, ((0, 0), (0, 0), (0, total - per))).reshape(k, n_heads * total)


def _in_proj(x2, pos, attn_norm_g, w_in, q_norm_g, w_uq, kv_norm_g, w_ukv, tm):
    n = x2.shape[0]
    d = D_MODEL
    zc = lambda k: jnp.zeros((d, k), F32)
    w1 = jnp.concatenate([w_in[:, _OFF_CQ:_OFF_KPE], zc(MLA_NOPE), w_in[:, _OFF_KPE:_OFF_QB],
                          zc(LANES - MLA_NOPE - MLA_ROPE)], axis=1).astype(BF16)
    w2 = jnp.concatenate([w_in[:, _OFF_QB:_OFF_KB], w_in[:, _OFF_QI:_OFF_KI]], axis=1).astype(BF16)
    w3 = jnp.concatenate([w_in[:, _OFF_KB:_OFF_VB], w_in[:, _OFF_KI:_OFF_WI],
                          w_in[:, _OFF_VB:_OFF_QI], w_in[:, _OFF_WI:_OFF_GA],
                          zc(LANES - DSA_HEAD_DIM - IDX_HEADS)], axis=1).astype(BF16)
    w5 = w_in[:, _OFF_GA:].astype(BF16)
    wuq = _head_cols(w_uq, MLA_HEADS, (MLA_NOPE, MLA_ROPE), LANES).astype(BF16)
    w_ukv3 = w_ukv.reshape(MLA_KV_LORA, MLA_HEADS, MLA_NOPE + MLA_V)
    wuk = jnp.pad(w_ukv3[:, :, :MLA_NOPE], ((0, 0), (0, 0), (0, LANES - MLA_NOPE))
                  ).reshape(MLA_KV_LORA, MLA_HEADS * LANES).astype(BF16)
    wuv = w_ukv3[:, :, MLA_NOPE:].reshape(MLA_KV_LORA, MLA_HEADS * MLA_V).astype(BF16)
    cm, slm, shm = _rope_tables(pos, MLA_ROPE, MLA_NOPE, LANES)
    cd, sld, shd = _rope_tables(pos, DSA_ROT, 0, DSA_HEAD_DIM)

    row = lambda w_: pl.BlockSpec((tm, w_), lambda i: (i, 0))
    full = lambda a: pl.BlockSpec(a.shape, lambda i: (0,) * a.ndim)
    hm = pl.BlockSpec((DSA_HEADS, tm, DSA_HEAD_DIM), lambda i: (0, i, 0))
    g2 = attn_norm_g.reshape(1, d)
    qng = q_norm_g.reshape(1, -1)
    kvng = kv_norm_g.reshape(1, -1)
    consts = (g2, w1, w2, w3, w5, qng, wuq, kvng, wuk, wuv)
    out_shape = (
        jax.ShapeDtypeStruct((n, MLA_HEADS * LANES), BF16),
        jax.ShapeDtypeStruct((n, MLA_HEADS * LANES), BF16),
        jax.ShapeDtypeStruct((n, MLA_HEADS * MLA_V), BF16),
        jax.ShapeDtypeStruct((DSA_HEADS, n, DSA_HEAD_DIM), BF16),
        jax.ShapeDtypeStruct((IDX_HEADS, n, IDX_DIM), BF16),
        jax.ShapeDtypeStruct((n, DSA_HEAD_DIM), BF16),
        jax.ShapeDtypeStruct((n, IDX_DIM), BF16),
        jax.ShapeDtypeStruct((n, LANES), BF16),
        jax.ShapeDtypeStruct((n, IDX_HEADS), F32),
        jax.ShapeDtypeStruct((n, D_MODEL), BF16),
        jax.ShapeDtypeStruct((n, D_MODEL), BF16),
    )
    out_specs = (row(MLA_HEADS * LANES), row(MLA_HEADS * LANES), row(MLA_HEADS * MLA_V),
                 hm, hm, row(DSA_HEAD_DIM), row(IDX_DIM), row(LANES), row(IDX_HEADS),
                 row(D_MODEL), row(D_MODEL))
    return pl.pallas_call(
        _in_proj_body,
        grid=(n // tm,),
        in_specs=[row(d)] + [full(a) for a in consts] + [row(LANES)] * 6,
        out_specs=out_specs,
        out_shape=out_shape,
        compiler_params=_cparams(("parallel",)),
        name="in_proj",
    )(x2, *consts, cm, slm, shm, cd, sld, shd)


NEG_BIG = -1e30
_NT = (((1,), (1,)), ((), ()))


def _mla_body(q_ref, k_ref, v_ref, o_ref, *, tq, tk):
    qi = pl.program_id(2)
    n_sub = tq // tk
    qs = [q_ref[:, hh * LANES:(hh + 1) * LANES] for hh in range(2)]

    def step(j, carry, masked):
        start = pl.multiple_of(j * tk, tk)
        vs = v_ref[pl.ds(start, tk), :]
        new = []
        for hh in range(2):
            m, l, acc = carry[hh]
            ks = k_ref[pl.ds(start, tk), hh * LANES:(hh + 1) * LANES]
            s = lax.dot_general(qs[hh], ks, _NT, preferred_element_type=F32)
            if masked:
                row = qi * tq + lax.broadcasted_iota(jnp.int32, (tq, tk), 0)
                col = j * tk + lax.broadcasted_iota(jnp.int32, (tq, tk), 1)
                s = jnp.where(col <= row, s, NEG_BIG)
            m_new = jnp.maximum(m, jnp.max(s, axis=-1, keepdims=True))
            alpha = jnp.exp2(m - m_new)
            p = jnp.exp2(s - m_new)
            l = alpha * l + jnp.sum(p, axis=-1, keepdims=True)
            acc = alpha * acc + jnp.dot(p.astype(BF16), vs, preferred_element_type=F32)
            new.append((m_new, l, acc))
        return tuple(new)

    one = (jnp.full((tq, 1), NEG_BIG, F32), jnp.zeros((tq, 1), F32), jnp.zeros((tq, LANES), F32))
    carry = lax.fori_loop(0, qi * n_sub, functools.partial(step, masked=False), (one, one))
    for dd in range(n_sub):
        carry = step(qi * n_sub + dd, carry, True)
    outs = [acc / l for (_, l, acc) in carry]
    lane = lax.broadcasted_iota(jnp.int32, (tq, LANES), 1)
    o_ref[...] = jnp.where(lane < MLA_V, outs[0], outs[1]).astype(BF16)


def _mla_attention(qm, km, vm, b, s, tq, tk):
    n = b * s
    nq = s // tq
    return pl.pallas_call(
        functools.partial(_mla_body, tq=tq, tk=tk),
        grid=(b, MLA_HEADS // 2, nq),
        in_specs=[
            pl.BlockSpec((tq, 2 * LANES), lambda bi, hp, qi: (bi * nq + qi, hp)),
            pl.BlockSpec((s, 2 * LANES), lambda bi, hp, qi: (bi, hp)),
            pl.BlockSpec((s, 2 * MLA_V), lambda bi, hp, qi: (bi, hp)),
        ],
        out_specs=pl.BlockSpec((tq, 2 * MLA_V), lambda bi, hp, qi: (bi * nq + qi, hp)),
        out_shape=jax.ShapeDtypeStruct((n, MLA_HEADS * MLA_V), BF16),
        compiler_params=_cparams(("parallel", "parallel", "arbitrary")),
        name="mla_attention",
    )(qm, km, vm)


INT_MIN = -2 ** 31
KEY_NEG_INF = (0xFF800000 - 2 ** 32) ^ 0x7FFFFFFF


def _sortable_key(score):
    bits = pltpu.bitcast(score, jnp.int32)
    return bits ^ ((bits >> 31) & 0x7FFFFFFF)


def _dsa_body(qi_ref, qd_ref, ki_ref, kd_ref, vd_ref, w_ref, o_ref, keys_ref, s_ref, p_ref, *,
              tq, tk, tka, topk):
    qb = pl.program_id(1)
    n_tiles = (qb * tq + tq + tk - 1) // tk
    q_pos = qb * tq + lax.broadcasted_iota(jnp.int32, (tq, 1), 0)

    qidx = qi_ref[...].reshape(IDX_HEADS * tq, IDX_DIM)
    w = w_ref[...]
    wcols = [jnp.broadcast_to(w[:, h:h + 1], (tq, tk)) for h in range(IDX_HEADS)]

    def score_tile(j, _):
        start = pl.multiple_of(j * tk, tk)
        kt = ki_ref[pl.ds(start, tk), :]
        sh = lax.dot_general(qidx, kt, _NT, preferred_element_type=F32)
        sc = jnp.zeros((tq, tk), F32)
        for h in range(IDX_HEADS):
            sc = sc + jnp.maximum(sh[h * tq:(h + 1) * tq], 0.0) * wcols[h]
        col = j * tk + lax.broadcasted_iota(jnp.int32, (tq, tk), 1)
        sc = jnp.where(col <= q_pos, sc, -jnp.inf)
        keys_ref[:, pl.ds(start, tk)] = _sortable_key(sc)
        return 0

    lax.fori_loop(0, n_tiles, score_tile, 0)

    def count_ge(cand):
        def body(j, cnt):
            start = pl.multiple_of(j * tk, tk)
            kt = keys_ref[:, pl.ds(start, tk)]
            hit = (kt >= cand).astype(jnp.int32)
            for c in range(tk // LANES):
                cnt = cnt + hit[:, c * LANES:(c + 1) * LANES]
            return cnt
        cnt = lax.fori_loop(0, n_tiles, body, jnp.zeros((tq, LANES), jnp.int32))
        return jnp.sum(cnt, axis=-1, keepdims=True)

    def group_max(j, carry):
        g0, g1 = carry
        start = pl.multiple_of(j * tk, tk)
        kt = keys_ref[:, pl.ds(start, tk)]
        for c in range(0, tk // LANES, 2):
            g0 = jnp.maximum(g0, kt[:, c * LANES:(c + 1) * LANES])
            g1 = jnp.maximum(g1, kt[:, (c + 1) * LANES:(c + 2) * LANES])
        return g0, g1

    lowest = jnp.full((tq, LANES), INT_MIN, jnp.int32)
    g0, g1 = lax.fori_loop(0, n_tiles, group_max, (lowest, lowest))
    hi = jnp.max(jnp.maximum(g0, g1), axis=-1, keepdims=True)
    lo = jnp.min(jnp.minimum(g0, g1), axis=-1, keepdims=True)
    need = q_pos + 1 > topk
    lo = jnp.where(need, lo, KEY_NEG_INF + 1)
    hi = jnp.where(need, hi, KEY_NEG_INF + 1)

    def n_active(lo, hi, cnt_lo):
        return jnp.max(((lo < hi) & (cnt_lo != topk)).astype(jnp.int32))

    def bisect(carry):
        lo, hi, cnt_lo, _ = carry
        mid = (lo | hi) - ((lo ^ hi) >> 1)
        cnt = count_ge(mid)
        ok = cnt >= topk
        lo = jnp.where(ok, mid, lo)
        cnt_lo = jnp.where(ok, cnt, cnt_lo)
        hi = jnp.where(ok, hi, mid - 1)
        return lo, hi, cnt_lo, n_active(lo, hi, cnt_lo)

    unknown = jnp.full((tq, 1), -1, jnp.int32)
    thr, _, _, _ = lax.while_loop(lambda c: c[3] > 0, bisect,
                                  (lo, hi, unknown, n_active(lo, hi, unknown)))
    thr = jnp.maximum(thr, KEY_NEG_INF + 1)

    qd = qd_ref[...].reshape(DSA_HEADS * tq, DSA_HEAD_DIM)

    def attn_tile(j, carry):
        m, acc = carry
        start = pl.multiple_of(j * tka, tka)
        kt = kd_ref[pl.ds(start, tka), :]
        vt = vd_ref[pl.ds(start, tka), :]
        sel = keys_ref[:, pl.ds(start, tka)] >= thr
        s = lax.dot_general(qd, kt, _NT, preferred_element_type=F32)
        s = jnp.where(sel[None], s.reshape(DSA_HEADS, tq, tka), NEG_BIG)
        m_new = jnp.maximum(m, jnp.max(s, axis=-1, keepdims=True))
        p = jnp.exp2(s - m_new).astype(BF16)
        pv = jnp.dot(p.reshape(DSA_HEADS * tq, tka), vt, preferred_element_type=F32)
        acc = jnp.exp2(m - m_new) * acc + pv.reshape(DSA_HEADS, tq, LANES)
        return m_new, acc

    init = (jnp.full((DSA_HEADS, tq, 1), NEG_BIG, F32), jnp.zeros((DSA_HEADS, tq, LANES), F32))
    m, acc = lax.fori_loop(0, n_tiles * (tk // tka), attn_tile, init)
    for h in range(DSA_HEADS):
        out = acc[h, :, :DSA_HEAD_DIM] / acc[h, :, DSA_HEAD_DIM:DSA_HEAD_DIM + 1]
        o_ref[:, h * DSA_HEAD_DIM:(h + 1) * DSA_HEAD_DIM] = out.astype(BF16)


def _dsa_attention(qi, qd, ki, kd, vd, wi, b, s, tq, tk, tka):
    assert tk % tka == 0 and tk % (2 * LANES) == 0
    n = b * s
    nq = s // tq
    topk = min(TOPK_MAX, s // 4)
    hm = pl.BlockSpec((DSA_HEADS, tq, DSA_HEAD_DIM), lambda bi, qb: (0, bi * nq + qb, 0))
    kv = pl.BlockSpec((s, DSA_HEAD_DIM), lambda bi, qb: (bi, 0))
    return pl.pallas_call(
        functools.partial(_dsa_body, tq=tq, tk=tk, tka=tka, topk=topk),
        grid=(b, nq),
        in_specs=[hm, hm, kv, kv, pl.BlockSpec((s, LANES), lambda bi, qb: (bi, 0)),
                  pl.BlockSpec((tq, IDX_HEADS), lambda bi, qb: (bi * nq + qb, 0))],
        out_specs=pl.BlockSpec((tq, DSA_HEADS * DSA_HEAD_DIM), lambda bi, qb: (bi * nq + qb, 0)),
        out_shape=jax.ShapeDtypeStruct((n, DSA_HEADS * DSA_HEAD_DIM), BF16),
        scratch_shapes=[pltpu.VMEM((tq, s), jnp.int32),
                        pltpu.VMEM((DSA_HEADS * tq, tka), F32),
                        pltpu.VMEM((DSA_HEADS * tq, tka), BF16)],
        compiler_params=_cparams(("parallel", "arbitrary")),
        name="dsa_attention",
    )(qi, qd, ki, kd, vd, wi)


HALF_D = D_MODEL // 2
RUN_ALIGN = 8


def _pack_bf16_pairs(y):
    r = pltpu.bitcast(y.astype(BF16).astype(F32), jnp.uint32)
    return r[:, :HALF_D] | (r[:, HALF_D:] >> 16)


def _unpack_bf16_pairs(p):
    hi = pltpu.bitcast(p & jnp.uint32(0xFFFF0000), F32)
    lo = pltpu.bitcast(p << 16, F32)
    return jnp.concatenate([hi, lo], axis=1)


def _split_bf16(a):
    hi = a.astype(BF16)
    lo = (a - hi.astype(F32)).astype(BF16)
    return hi, lo


def _merge_body(x_ref, ya_ref, yb_ref, ga_ref, gb_ref, wa_ref, wb_ref, wo_ref, g_ref,
                rwh_ref, rwl_ref, rb_ref, x2_ref, h_ref, logit_ref):
    ma = jnp.dot(ya_ref[...], wa_ref[...], preferred_element_type=F32)
    mb = jnp.dot(yb_ref[...], wb_ref[...], preferred_element_type=F32)
    merged = ga_ref[...].astype(F32) * ma + gb_ref[...].astype(F32) * mb
    x2 = x_ref[...] + jnp.dot(merged.astype(BF16), wo_ref[...], preferred_element_type=F32)
    x2_ref[...] = x2
    h = x2 * lax.rsqrt(jnp.mean(x2 * x2, axis=-1, keepdims=True) + EPS) * g_ref[...]
    hh, hl = _split_bf16(h)
    h_ref[...] = hh
    logit_ref[...] = (jnp.dot(hh, rwh_ref[...], preferred_element_type=F32)
                      + jnp.dot(hh, rwl_ref[...], preferred_element_type=F32)
                      + jnp.dot(hl, rwh_ref[...], preferred_element_type=F32)) + rb_ref[...]


def _merge(x2d, y_a, y_b, ga, gb, w_ba, w_bb, w_out, ffn_norm_g, router_w, router_b, tm):
    n, d = x2d.shape
    rw = jnp.pad(router_w, ((0, 0), (0, LANES - N_EXPERTS)))
    rwh = rw.astype(BF16)
    rwl = (rw - rwh.astype(F32)).astype(BF16)
    rb = jnp.pad(router_b, (0, LANES - N_EXPERTS), constant_values=NEG_BIG).reshape(1, LANES)
    consts = (w_ba.astype(BF16), w_bb.astype(BF16), w_out.astype(BF16), ffn_norm_g.reshape(1, d),
              rwh, rwl, rb)
    row = lambda w_: pl.BlockSpec((tm, w_), lambda i: (i, 0))
    full = lambda a: pl.BlockSpec(a.shape, lambda i: (0,) * a.ndim)
    out_shape = (
        jax.ShapeDtypeStruct((n, d), F32),
        jax.ShapeDtypeStruct((n, d), BF16),
        jax.ShapeDtypeStruct((n, LANES), F32),
    )
    return pl.pallas_call(
        _merge_body,
        grid=(n // tm,),
        in_specs=[row(d), row(HALF_D), row(HALF_D), row(d), row(d)] + [full(a) for a in consts],
        out_specs=(row(d), row(d), row(LANES)),
        out_shape=out_shape,
        compiler_params=_cparams(("parallel",)),
        name="merge",
    )(x2d, y_a, y_b, ga, gb, *consts)


def _router_body(logit_ref, upper_ref, e_ref, gate_ref, lp_ref, bc_ref, carry_out_ref, cnt_ref,
                 carry_ref, *, tm):
    i = pl.program_id(0)

    @pl.when(i == 0)
    def _():
        carry_ref[...] = jnp.zeros_like(carry_ref)

    lane = lax.broadcasted_iota(jnp.int32, (tm, LANES), 1)
    work = logit_ref[...]
    experts, vals = [], []
    onehot = jnp.zeros((tm, LANES), F32)
    for _ in range(TOP_K):
        mx = jnp.max(work, axis=-1, keepdims=True)
        idx = jnp.min(jnp.where(work == mx, lane, LANES), axis=-1, keepdims=True)
        hit = lane == idx
        experts.append(idx)
        vals.append(mx)
        onehot = onehot + hit.astype(F32)
        work = jnp.where(hit, -jnp.inf, work)
    ex = [jnp.exp(v - vals[0]) for v in vals]
    denom = ex[0] + ex[1] + ex[2] + ex[3]
    for r in range(TOP_K):
        e_ref[:, r:r + 1] = experts[r]
        gate_ref[:, r:r + 1] = ex[r] / denom

    rr = lax.broadcasted_iota(jnp.int32, (tm, tm), 0)
    cc = lax.broadcasted_iota(jnp.int32, (tm, tm), 1)
    lower = (cc < rr).astype(BF16)
    prefix = jnp.dot(lower, onehot.astype(BF16), preferred_element_type=F32)
    bc = jnp.sum(onehot, axis=0, keepdims=True)
    bc = jnp.floor((bc + (RUN_ALIGN - 1)) * (1.0 / RUN_ALIGN)) * RUN_ALIGN
    bc8 = jnp.broadcast_to(bc, (8, LANES))
    boff = jnp.dot(bc8.astype(BF16), upper_ref[...], preferred_element_type=F32)[0:1, :]
    local = prefix + boff
    for r in range(TOP_K):
        lp = jnp.sum(jnp.where(lane == experts[r], local, 0.0), axis=-1, keepdims=True)
        lp_ref[:, r:r + 1] = lp.astype(jnp.int32)
    bc_ref[0] = bc8
    carry_out_ref[0] = carry_ref[...]
    total = carry_ref[...] + bc8
    carry_ref[...] = total
    cnt_ref[...] = total


def _router(logits, tm):
    assert tm <= 256
    n = logits.shape[0]
    nblk = n // tm
    upper = (jnp.arange(LANES)[:, None] < jnp.arange(LANES)[None, :]).astype(BF16)
    row = lambda w_: pl.BlockSpec((tm, w_), lambda i: (i, 0))
    blk = pl.BlockSpec((1, 8, LANES), lambda i: (i, 0, 0))
    out_shape = (
        jax.ShapeDtypeStruct((n, TOP_K), jnp.int32),
        jax.ShapeDtypeStruct((n, TOP_K), F32),
        jax.ShapeDtypeStruct((n, TOP_K), jnp.int32),
        jax.ShapeDtypeStruct((nblk, 8, LANES), F32),
        jax.ShapeDtypeStruct((nblk, 8, LANES), F32),
        jax.ShapeDtypeStruct((8, LANES), F32),
    )
    return pl.pallas_call(
        functools.partial(_router_body, tm=tm),
        grid=(nblk,),
        in_specs=[row(LANES), pl.BlockSpec((LANES, LANES), lambda i: (0, 0))],
        out_specs=(row(TOP_K), row(TOP_K), row(TOP_K), blk, blk,
                   pl.BlockSpec((8, LANES), lambda i: (0, 0))),
        out_shape=out_shape,
        scratch_shapes=[pltpu.VMEM((8, LANES), F32)],
        compiler_params=_cparams(("arbitrary",)),
        name="router",
    )(logits, upper)


def _local_rows(tm):
    return TOP_K * tm + N_EXPERTS * RUN_ALIGN


def _for_each_run_chunk(tbl_ref, tm, fn):
    sizes = [s for s in (1 << k for k in range(tm.bit_length())) if RUN_ALIGN <= s <= tm]

    def per_expert(e, _):
        length = tbl_ref[0, 0, e]
        src = tbl_ref[0, 0, N_EXPERTS + e]
        dst = tbl_ref[0, 0, 2 * N_EXPERTS + e]
        for size in sizes:

            @pl.when((length & size) != 0)
            def _(size=size):
                off = length & (size - 1)
                fn(pl.multiple_of(src + off, RUN_ALIGN), pl.multiple_of(dst + off, RUN_ALIGN), size)
        return 0

    lax.fori_loop(0, N_EXPERTS, per_expert, 0)


def _dispatch_body(tbl_ref, lpt_ref, h_ref, xbuf_in_ref, xbuf_ref, sorted_ref, sem, *, tm):
    del xbuf_in_ref
    h = h_ref[...]
    lpt = lpt_ref[0]
    chunk = tm
    for c in range(_local_rows(tm) // chunk):
        r_idx = c * chunk + lax.broadcasted_iota(jnp.int32, (chunk, tm), 0)
        sel = jnp.zeros((chunk, tm), F32)
        for r in range(TOP_K):
            sel = sel + (r_idx == lpt[r:r + 1, :]).astype(F32)
        rows = jnp.dot(sel.astype(BF16), h, preferred_element_type=F32)
        sorted_ref[c * chunk:(c + 1) * chunk, :] = _pack_bf16_pairs(rows)

    def copy(local_row, global_row, size):
        return pltpu.make_async_copy(sorted_ref.at[pl.ds(local_row, size)],
                                     xbuf_ref.at[pl.ds(global_row, size)], sem)

    _for_each_run_chunk(tbl_ref, tm, lambda s, d, n: copy(s, d, n).start())
    _for_each_run_chunk(tbl_ref, tm, lambda s, d, n: copy(s, d, n).wait())


def _dispatch(h2, lpt, tbl, n_rows, tm):
    n = h2.shape[0]
    xbuf0 = jnp.zeros((n_rows, HALF_D), jnp.uint32)
    return pl.pallas_call(
        functools.partial(_dispatch_body, tm=tm),
        grid=(n // tm,),
        in_specs=[pl.BlockSpec((1, 1, LANES), lambda i: (i, 0, 0), memory_space=pltpu.SMEM),
                  pl.BlockSpec((1, TOP_K, tm), lambda i: (i, 0, 0)),
                  pl.BlockSpec((tm, D_MODEL), lambda i: (i, 0)),
                  pl.BlockSpec(memory_space=pl.ANY)],
        out_specs=pl.BlockSpec(memory_space=pl.ANY),
        out_shape=jax.ShapeDtypeStruct((n_rows, HALF_D), jnp.uint32),
        scratch_shapes=[pltpu.VMEM((_local_rows(tm), HALF_D), jnp.uint32),
                        pltpu.SemaphoreType.DMA],
        input_output_aliases={3: 0},
        compiler_params=_cparams(("arbitrary",)),
        name="moe_dispatch",
    )(tbl, lpt, h2, xbuf0)


def _expert_body(blk_e_ref, n_used_ref, x_ref, wg_ref, wl_ref, bg_ref, bl_ref, wd_ref, bd_ref,
                 y_ref):
    del blk_e_ref
    i = pl.program_id(0)

    @pl.when(i < n_used_ref[0])
    def _():
        xb = _unpack_bf16_pairs(x_ref[...]).astype(BF16)
        glu = jnp.dot(xb, wg_ref[0], preferred_element_type=F32) + bg_ref[0]
        lin = jnp.dot(xb, wl_ref[0], preferred_element_type=F32) + bl_ref[0]
        glu = jnp.minimum(glu, SWIGLU_LIMIT)
        lin = jnp.clip(lin, -SWIGLU_LIMIT, SWIGLU_LIMIT)
        act = glu * jax.nn.sigmoid(SWIGLU_ALPHA * glu) * (lin + 1.0)
        y = jnp.dot(act.astype(BF16), wd_ref[0], preferred_element_type=F32) + bd_ref[0]
        y_ref[...] = _pack_bf16_pairs(y)

    @pl.when(i >= n_used_ref[0])
    def _():
        y_ref[...] = jnp.zeros_like(y_ref)


def _experts(xbuf, blk_e, n_used, w_gate_up, b_gate_up, w_down, b_down, tb):
    n_rows = xbuf.shape[0]
    d = D_MODEL
    wgu = w_gate_up.astype(BF16)
    wd = w_down.astype(BF16)
    bgu = b_gate_up.reshape(N_EXPERTS, 1, 2 * D_FF)
    bd = b_down.reshape(N_EXPERTS, 1, d)
    grid_spec = pltpu.PrefetchScalarGridSpec(
        num_scalar_prefetch=2,
        grid=(n_rows // tb,),
        in_specs=[
            pl.BlockSpec((tb, HALF_D), lambda i, be, nu: (i, 0)),
            pl.BlockSpec((1, d, D_FF), lambda i, be, nu: (be[i], 0, 0)),
            pl.BlockSpec((1, d, D_FF), lambda i, be, nu: (be[i], 0, 1)),
            pl.BlockSpec((1, 1, D_FF), lambda i, be, nu: (be[i], 0, 0)),
            pl.BlockSpec((1, 1, D_FF), lambda i, be, nu: (be[i], 0, 1)),
            pl.BlockSpec((1, D_FF, d), lambda i, be, nu: (be[i], 0, 0)),
            pl.BlockSpec((1, 1, d), lambda i, be, nu: (be[i], 0, 0)),
        ],
        out_specs=pl.BlockSpec((tb, HALF_D), lambda i, be, nu: (i, 0)),
    )
    return pl.pallas_call(
        _expert_body,
        grid_spec=grid_spec,
        out_shape=jax.ShapeDtypeStruct((n_rows, HALF_D), jnp.uint32),
        compiler_params=_cparams(("arbitrary",)),
        name="moe_experts",
    )(blk_e, n_used, xbuf, wgu, wgu, bgu, bgu, wd, bd)


def _combine_body(tbl_ref, x2_ref, gate_ref, lp_ref, g_ref, ybuf_ref, o_ref, ys_ref, sem, *, tm):
    n_local = _local_rows(tm)
    ys_ref[TOP_K * tm:, :] = jnp.zeros((n_local - TOP_K * tm, HALF_D), jnp.uint32)

    def copy(local_row, global_row, size):
        return pltpu.make_async_copy(ybuf_ref.at[pl.ds(global_row, size)],
                                     ys_ref.at[pl.ds(local_row, size)], sem)

    _for_each_run_chunk(tbl_ref, tm, lambda s, d, n: copy(s, d, n).start())
    _for_each_run_chunk(tbl_ref, tm, lambda s, d, n: copy(s, d, n).wait())

    gate = gate_ref[...]
    lp = lp_ref[...]
    col = lax.broadcasted_iota(jnp.int32, (tm, n_local), 1)
    gmat = jnp.zeros((tm, n_local), F32)
    for r in range(TOP_K):
        gmat = gmat + jnp.where(col == lp[:, r:r + 1], gate[:, r:r + 1], 0.0)
    g_hi, g_lo = _split_bf16(gmat)
    ys = _unpack_bf16_pairs(ys_ref[...]).astype(BF16)
    out = (x2_ref[...] + jnp.dot(g_hi, ys, preferred_element_type=F32)
           + jnp.dot(g_lo, ys, preferred_element_type=F32))
    o_ref[...] = out * lax.rsqrt(jnp.mean(out * out, axis=-1, keepdims=True) + EPS) * g_ref[...]


def _combine(x2, gate, lp, tbl, ybuf, final_norm_g, tm):
    n, d = x2.shape
    return pl.pallas_call(
        functools.partial(_combine_body, tm=tm),
        grid=(n // tm,),
        in_specs=[pl.BlockSpec((1, 1, LANES), lambda i: (i, 0, 0), memory_space=pltpu.SMEM),
                  pl.BlockSpec((tm, d), lambda i: (i, 0)),
                  pl.BlockSpec((tm, TOP_K), lambda i: (i, 0)),
                  pl.BlockSpec((tm, TOP_K), lambda i: (i, 0)),
                  pl.BlockSpec((1, d), lambda i: (0, 0)),
                  pl.BlockSpec(memory_space=pl.ANY)],
        out_specs=pl.BlockSpec((tm, d), lambda i: (i, 0)),
        out_shape=jax.ShapeDtypeStruct((n, d), F32),
        scratch_shapes=[pltpu.VMEM((_local_rows(tm), HALF_D), jnp.uint32),
                        pltpu.SemaphoreType.DMA],
        compiler_params=_cparams(("arbitrary",)),
        name="moe_combine",
    )(tbl, x2, gate, lp, final_norm_g.reshape(1, d), ybuf)


def _moe(x2, h2, logits, w_gate_up, b_gate_up, w_down, b_down, final_norm_g, tb, tm):
    n = x2.shape[0]
    nblk = n // tm
    max_rows = n * TOP_K + nblk * N_EXPERTS * (RUN_ALIGN - 1)
    n_blk = -(-max_rows // tb) + N_EXPERTS
    top_e, gate, lp, bcount, before, counts = _router(logits, tm)
    del top_e
    cnt = counts[0, :N_EXPERTS].astype(jnp.int32)
    padded = (cnt + tb - 1) // tb * tb
    pad_end = jnp.cumsum(padded)
    pad_start = pad_end - padded
    blk_first = jnp.arange(n_blk, dtype=jnp.int32) * tb
    blk_e = jnp.minimum(jnp.sum((pad_end[None, :] <= blk_first[:, None]).astype(jnp.int32), axis=1),
                        N_EXPERTS - 1)
    n_used = (pad_end[-1:] // tb).astype(jnp.int32)
    run_len = bcount[:, 0, :N_EXPERTS].astype(jnp.int32)
    run_src = jnp.cumsum(run_len, axis=1) - run_len
    run_dst = pad_start[None, :] + before[:, 0, :N_EXPERTS].astype(jnp.int32)
    tbl = jnp.concatenate([run_len, run_src, run_dst,
                           jnp.zeros((nblk, LANES - 3 * N_EXPERTS), jnp.int32)], axis=1)
    tbl = tbl.reshape(nblk, 1, LANES)
    lpt = lp.reshape(nblk, tm, TOP_K).transpose(0, 2, 1)
    xbuf = _dispatch(h2, lpt, tbl, n_blk * tb, tm)
    ybuf = _experts(xbuf, blk_e, n_used, w_gate_up, b_gate_up, w_down, b_down, tb)
    return _combine(x2, gate, lp, tbl, ybuf, final_norm_g, tm)


def kernel(x, positions, attn_norm_g, w_in, mla_q_norm_g, mla_w_uq, mla_kv_norm_g, mla_w_ukv,
           w_branch_mla, w_branch_dsa, w_out, ffn_norm_g, router_w, router_b,
           w_gate_up, b_gate_up, w_down, b_down, final_norm_g):
    b, s, d = x.shape
    n = b * s
    (qm, km, vm, qd, qi, kd, ki, vd, wi, ga, gb) = _in_proj(
        x.reshape(n, d), positions.reshape(n), attn_norm_g[0], w_in[0],
        mla_q_norm_g[0], mla_w_uq[0], mla_kv_norm_g[0], mla_w_ukv[0], tm=512)
    y_a = _mla_attention(qm, km, vm, b, s, tq=1024, tk=1024)
    y_b = _dsa_attention(qi, qd, ki, kd, vd, wi, b, s, tq=128, tk=1024, tka=1024)
    x2, h2, logits = _merge(
        x.reshape(n, d), y_a, y_b, ga, gb, w_branch_mla[0], w_branch_dsa[0], w_out[0],
        ffn_norm_g[0], router_w[0], router_b[0], tm=512)
    out = _moe(x2, h2, logits, w_gate_up[0], b_gate_up[0], w_down[0], b_down[0], final_norm_g,
               tb=512, tm=256)
    return out.reshape(b, s, d)
```

```python
import functools
import math

import jax
import jax.numpy as jnp
from jax import lax
from jax.experimental import pallas as pl
from jax.experimental.pallas import tpu as pltpu

F32 = jnp.float32
BF16 = jnp.bfloat16

LANES = 128
LOG2_E = 1.4426950408889634

D_MODEL = 1024
EPS = 1e-6
ROPE_THETA = 500000.0
MLA_HEADS = 8
MLA_Q_LORA = 256
MLA_KV_LORA = 128
MLA_NOPE = 64
MLA_ROPE = 32
MLA_V = 64
DSA_HEADS = 8
DSA_HEAD_DIM = 64
DSA_ROT = 16
IDX_HEADS = 8
IDX_DIM = 64
TOPK_MAX = 256
N_EXPERTS = 32
TOP_K = 4
D_FF = 1024
SWIGLU_LIMIT = 7.0
SWIGLU_ALPHA = 1.702

VMEM_LIMIT = 56 * 1024 * 1024

_OFF_CQ = 0
_OFF_CKV = _OFF_CQ + MLA_Q_LORA
_OFF_KPE = _OFF_CKV + MLA_KV_LORA
_OFF_QB = _OFF_KPE + MLA_ROPE
_OFF_KB = _OFF_QB + DSA_HEADS * DSA_HEAD_DIM
_OFF_VB = _OFF_KB + DSA_HEAD_DIM
_OFF_QI = _OFF_VB + DSA_HEAD_DIM
_OFF_KI = _OFF_QI + IDX_HEADS * IDX_DIM
_OFF_WI = _OFF_KI + IDX_DIM
_OFF_GA = _OFF_WI + IDX_HEADS
_OFF_GB = _OFF_GA + D_MODEL
_D_IN = _OFF_GB + D_MODEL


def _cparams(sem):
    return pltpu.CompilerParams(dimension_semantics=sem, vmem_limit_bytes=VMEM_LIMIT)


def _rope_group(xg, c, slo, shi, shift):
    return (xg * c + pltpu.roll(xg, LANES - shift, 1) * slo
            + pltpu.roll(xg, shift, 1) * shi)


def _in_proj_body(x_ref, g_ref, w1_ref, w2_ref, w3_ref, w5_ref, qng_ref, wuq_ref,
                  kvng_ref, wuk_ref, wuv_ref, cm_ref, slm_ref, shm_ref, cd_ref,
                  sld_ref, shd_ref,
                  qm_ref, km_ref, vm_ref, qd_ref, qi_ref, kd_ref, ki_ref, vd_ref,
                  wi_ref, ga_ref, gb_ref):
    x = x_ref[...]
    h = x * lax.rsqrt(jnp.mean(x * x, axis=-1, keepdims=True) + EPS) * g_ref[...]
    hb = h.astype(BF16)

    cm, slm, shm = cm_ref[...], slm_ref[...], shm_ref[...]
    cd, sld, shd = cd_ref[...], sld_ref[...], shd_ref[...]

    z1 = jnp.dot(hb, w1_ref[...], preferred_element_type=F32)
    cq = z1[:, 0:MLA_Q_LORA]
    cqn = cq * lax.rsqrt(jnp.mean(cq * cq, axis=-1, keepdims=True) + EPS) * qng_ref[...]
    q = jnp.dot(cqn.astype(BF16), wuq_ref[...], preferred_element_type=F32)
    q_scale = (MLA_NOPE + MLA_ROPE) ** -0.5 * LOG2_E
    for j in range(MLA_HEADS):
        qg = _rope_group(q[:, j * LANES:(j + 1) * LANES], cm, slm, shm, MLA_ROPE // 2)
        qm_ref[:, j * LANES:(j + 1) * LANES] = (qg * q_scale).astype(BF16)

    ckv = z1[:, MLA_Q_LORA:MLA_Q_LORA + MLA_KV_LORA]
    ckvn = (ckv * lax.rsqrt(jnp.mean(ckv * ckv, axis=-1, keepdims=True) + EPS)
            * kvng_ref[...]).astype(BF16)
    kpe = _rope_group(z1[:, MLA_Q_LORA + MLA_KV_LORA:], cm, slm, shm, MLA_ROPE // 2)
    kn = jnp.dot(ckvn, wuk_ref[...], preferred_element_type=F32)
    for j in range(MLA_HEADS):
        km_ref[:, j * LANES:(j + 1) * LANES] = (kn[:, j * LANES:(j + 1) * LANES] + kpe).astype(BF16)
    vm_ref[...] = jnp.dot(ckvn, wuv_ref[...], preferred_element_type=F32).astype(BF16)

    z2 = jnp.dot(hb, w2_ref[...], preferred_element_type=F32)
    d_scale = DSA_HEAD_DIM ** -0.5 * LOG2_E
    for j in range(DSA_HEADS // 2):
        g = _rope_group(z2[:, j * LANES:(j + 1) * LANES], cd, sld, shd, DSA_ROT // 2) * d_scale
        qd_ref[2 * j] = g[:, :DSA_HEAD_DIM].astype(BF16)
        qd_ref[2 * j + 1] = g[:, DSA_HEAD_DIM:].astype(BF16)
    base = DSA_HEADS * DSA_HEAD_DIM
    for j in range(IDX_HEADS // 2):
        g = _rope_group(z2[:, base + j * LANES:base + (j + 1) * LANES], cd, sld, shd, DSA_ROT // 2)
        qi_ref[2 * j] = g[:, :IDX_DIM].astype(BF16)
        qi_ref[2 * j + 1] = g[:, IDX_DIM:].astype(BF16)

    z3 = jnp.dot(hb, w3_ref[...], preferred_element_type=F32)
    kb = _rope_group(z3[:, :LANES], cd, sld, shd, DSA_ROT // 2)
    kd_ref[...] = kb[:, :DSA_HEAD_DIM].astype(BF16)
    ki_ref[...] = kb[:, DSA_HEAD_DIM:].astype(BF16)
    zv = z3[:, LANES:]
    vlane = lax.broadcasted_iota(jnp.int32, zv.shape, 1)
    vd_ref[...] = jnp.where(vlane < DSA_HEAD_DIM, zv,
                            jnp.where(vlane == DSA_HEAD_DIM, 1.0, 0.0)).astype(BF16)
    w_scale = IDX_HEADS ** -0.5 * IDX_DIM ** -0.5
    wi_ref[...] = z3[:, LANES + DSA_HEAD_DIM:LANES + DSA_HEAD_DIM + IDX_HEADS] * w_scale

    z5 = jnp.dot(hb, w5_ref[...], preferred_element_type=F32)
    ga_ref[...] = jax.nn.sigmoid(z5[:, :D_MODEL]).astype(BF16)
    gb_ref[...] = jax.nn.sigmoid(z5[:, D_MODEL:]).astype(BF16)


def _rope_tables(pos, rot_dim, lane_of_x1, period):
    half = rot_dim // 2
    inv_freq = ROPE_THETA ** (-jnp.arange(half, dtype=F32) / half)
    ang = pos.astype(F32)[:, None] * inv_freq
    cos, sin = jnp.cos(ang), jnp.sin(ang)
    n = pos.shape[0]
    reps = LANES // period
    pad_l = lane_of_x1
    pad_r = period - lane_of_x1 - rot_dim

    def pattern(a, b, fill):
        blk = jnp.concatenate([jnp.full((n, pad_l), fill, F32), a, b,
                               jnp.full((n, pad_r), fill, F32)], axis=1)
        return jnp.tile(blk, (1, reps))

    zeros = jnp.zeros_like(sin)
    c = pattern(cos, cos, 1.0)
    slo = pattern(-sin, zeros, 0.0)
    shi = pattern(zeros, sin, 0.0)
    return c, slo, shi


def _head_cols(w, n_heads, widths, total):
    k = w.shape[0]
    per = sum(widths)
    w = w.reshape(k, n_heads, per)
    return jnp.pad(w, ((0, 0), (0, 0), (0, total - per))).reshape(k, n_heads * total)


def _in_proj(x2, pos, attn_norm_g, w_in, q_norm_g, w_uq, kv_norm_g, w_ukv, tm):
    n = x2.shape[0]
    d = D_MODEL
    zc = lambda k: jnp.zeros((d, k), F32)
    w1 = jnp.concatenate([w_in[:, _OFF_CQ:_OFF_KPE], zc(MLA_NOPE), w_in[:, _OFF_KPE:_OFF_QB],
                          zc(LANES - MLA_NOPE - MLA_ROPE)], axis=1).astype(BF16)
    w2 = jnp.concatenate([w_in[:, _OFF_QB:_OFF_KB], w_in[:, _OFF_QI:_OFF_KI]], axis=1).astype(BF16)
    w3 = jnp.concatenate([w_in[:, _OFF_KB:_OFF_VB], w_in[:, _OFF_KI:_OFF_WI],
                          w_in[:, _OFF_VB:_OFF_QI], w_in[:, _OFF_WI:_OFF_GA],
                          zc(LANES - DSA_HEAD_DIM - IDX_HEADS)], axis=1).astype(BF16)
    w5 = w_in[:, _OFF_GA:].astype(BF16)
    wuq = _head_cols(w_uq, MLA_HEADS, (MLA_NOPE, MLA_ROPE), LANES).astype(BF16)
    w_ukv3 = w_ukv.reshape(MLA_KV_LORA, MLA_HEADS, MLA_NOPE + MLA_V)
    wuk = jnp.pad(w_ukv3[:, :, :MLA_NOPE], ((0, 0), (0, 0), (0, LANES - MLA_NOPE))
                  ).reshape(MLA_KV_LORA, MLA_HEADS * LANES).astype(BF16)
    wuv = w_ukv3[:, :, MLA_NOPE:].reshape(MLA_KV_LORA, MLA_HEADS * MLA_V).astype(BF16)
    cm, slm, shm = _rope_tables(pos, MLA_ROPE, MLA_NOPE, LANES)
    cd, sld, shd = _rope_tables(pos, DSA_ROT, 0, DSA_HEAD_DIM)

    row = lambda w_: pl.BlockSpec((tm, w_), lambda i: (i, 0))
    full = lambda a: pl.BlockSpec(a.shape, lambda i: (0,) * a.ndim)
    hm = pl.BlockSpec((DSA_HEADS, tm, DSA_HEAD_DIM), lambda i: (0, i, 0))
    g2 = attn_norm_g.reshape(1, d)
    qng = q_norm_g.reshape(1, -1)
    kvng = kv_norm_g.reshape(1, -1)
    consts = (g2, w1, w2, w3, w5, qng, wuq, kvng, wuk, wuv)
    out_shape = (
        jax.ShapeDtypeStruct((n, MLA_HEADS * LANES), BF16),
        jax.ShapeDtypeStruct((n, MLA_HEADS * LANES), BF16),
        jax.ShapeDtypeStruct((n, MLA_HEADS * MLA_V), BF16),
        jax.ShapeDtypeStruct((DSA_HEADS, n, DSA_HEAD_DIM), BF16),
        jax.ShapeDtypeStruct((IDX_HEADS, n, IDX_DIM), BF16),
        jax.ShapeDtypeStruct((n, DSA_HEAD_DIM), BF16),
        jax.ShapeDtypeStruct((n, IDX_DIM), BF16),
        jax.ShapeDtypeStruct((n, LANES), BF16),
        jax.ShapeDtypeStruct((n, IDX_HEADS), F32),
        jax.ShapeDtypeStruct((n, D_MODEL), BF16),
        jax.ShapeDtypeStruct((n, D_MODEL), BF16),
    )
    out_specs = (row(MLA_HEADS * LANES), row(MLA_HEADS * LANES), row(MLA_HEADS * MLA_V),
                 hm, hm, row(DSA_HEAD_DIM), row(IDX_DIM), row(LANES), row(IDX_HEADS),
                 row(D_MODEL), row(D_MODEL))
    return pl.pallas_call(
        _in_proj_body,
        grid=(n // tm,),
        in_specs=[row(d)] + [full(a) for a in consts] + [row(LANES)] * 6,
        out_specs=out_specs,
        out_shape=out_shape,
        compiler_params=_cparams(("parallel",)),
        name="in_proj",
    )(x2, *consts, cm, slm, shm, cd, sld, shd)


NEG_BIG = -1e30
_NT = (((1,), (1,)), ((), ()))


def _mla_body(q_ref, k_ref, v_ref, o_ref, *, tq, tk):
    qi = pl.program_id(2)
    n_sub = tq // tk
    qs = [q_ref[:, hh * LANES:(hh + 1) * LANES] for hh in range(2)]

    def step(j, carry, masked):
        start = pl.multiple_of(j * tk, tk)
        vs = v_ref[pl.ds(start, tk), :]
        new = []
        for hh in range(2):
            m, l, acc = carry[hh]
            ks = k_ref[pl.ds(start, tk), hh * LANES:(hh + 1) * LANES]
            s = lax.dot_general(qs[hh], ks, _NT, preferred_element_type=F32)
            if masked:
                row = qi * tq + lax.broadcasted_iota(jnp.int32, (tq, tk), 0)
                col = j * tk + lax.broadcasted_iota(jnp.int32, (tq, tk), 1)
                s = jnp.where(col <= row, s, NEG_BIG)
            m_new = jnp.maximum(m, jnp.max(s, axis=-1, keepdims=True))
            alpha = jnp.exp2(m - m_new)
            p = jnp.exp2(s - m_new)
            l = alpha * l + jnp.sum(p, axis=-1, keepdims=True)
            acc = alpha * acc + jnp.dot(p.astype(BF16), vs, preferred_element_type=F32)
            new.append((m_new, l, acc))
        return tuple(new)

    one = (jnp.full((tq, 1), NEG_BIG, F32), jnp.zeros((tq, 1), F32), jnp.zeros((tq, LANES), F32))
    carry = lax.fori_loop(0, qi * n_sub, functools.partial(step, masked=False), (one, one))
    for dd in range(n_sub):
        carry = step(qi * n_sub + dd, carry, True)
    outs = [acc / l for (_, l, acc) in carry]
    lane = lax.broadcasted_iota(jnp.int32, (tq, LANES), 1)
    o_ref[...] = jnp.where(lane < MLA_V, outs[0], outs[1]).astype(BF16)


def _mla_attention(qm, km, vm, b, s, tq, tk):
    n = b * s
    nq = s // tq
    return pl.pallas_call(
        functools.partial(_mla_body, tq=tq, tk=tk),
        grid=(b, MLA_HEADS // 2, nq),
        in_specs=[
            pl.BlockSpec((tq, 2 * LANES), lambda bi, hp, qi: (bi * nq + qi, hp)),
            pl.BlockSpec((s, 2 * LANES), lambda bi, hp, qi: (bi, hp)),
            pl.BlockSpec((s, 2 * MLA_V), lambda bi, hp, qi: (bi, hp)),
        ],
        out_specs=pl.BlockSpec((tq, 2 * MLA_V), lambda bi, hp, qi: (bi * nq + qi, hp)),
        out_shape=jax.ShapeDtypeStruct((n, MLA_HEADS * MLA_V), BF16),
        compiler_params=_cparams(("parallel", "parallel", "arbitrary")),
        name="mla_attention",
    )(qm, km, vm)


INT_MIN = -2 ** 31
SNAP_AFTER_TRIPS = 5
KEY_NEG_INF = (0xFF800000 - 2 ** 32) ^ 0x7FFFFFFF


def _sortable_key(score):
    bits = pltpu.bitcast(score, jnp.int32)
    return bits ^ ((bits >> 31) & 0x7FFFFFFF)


def _sortable_key_inverse(key):
    return pltpu.bitcast(key ^ ((key >> 31) & 0x7FFFFFFF), F32)


def _dsa_body(qi_ref, qd_ref, ki_ref, kd_ref, vd_ref, w_ref, o_ref, keys_ref, s_ref, p_ref, *,
              tq, tk, tka, topk, s_len):
    qb = pl.program_id(1)
    n_tiles = (qb * tq + tq + tk - 1) // tk
    q_pos = qb * tq + lax.broadcasted_iota(jnp.int32, (tq, 1), 0)

    qidx = qi_ref[...].reshape(IDX_HEADS * tq, IDX_DIM)
    w = w_ref[...]
    wcols = [jnp.broadcast_to(w[:, h:h + 1], (tq, tk)) for h in range(IDX_HEADS)]

    def score_tile(j, _):
        start = pl.multiple_of(j * tk, tk)
        kt = ki_ref[pl.ds(start, tk), :]
        sh = lax.dot_general(qidx, kt, _NT, preferred_element_type=F32)
        sc = jnp.zeros((tq, tk), F32)
        for h in range(IDX_HEADS):
            sc = sc + jnp.maximum(sh[h * tq:(h + 1) * tq], 0.0) * wcols[h]
        col = j * tk + lax.broadcasted_iota(jnp.int32, (tq, tk), 1)
        sc = jnp.where(col <= q_pos, sc, -jnp.inf)
        keys_ref[:, pl.ds(start, tk)] = _sortable_key(sc)
        return 0

    lax.fori_loop(0, n_tiles, score_tile, 0)

    def count_where(pred):
        def body(j, cnt):
            start = pl.multiple_of(j * tk, tk)
            hit = pred(keys_ref[:, pl.ds(start, tk)], j).astype(jnp.int32)
            for c in range(tk // LANES):
                cnt = cnt + hit[:, c * LANES:(c + 1) * LANES]
            return cnt
        cnt = lax.fori_loop(0, n_tiles, body, jnp.zeros((tq, LANES), jnp.int32))
        return jnp.sum(cnt, axis=-1, keepdims=True)

    def count_ge(cand):
        return count_where(lambda kt, j: kt >= cand)

    def max_le(bound):
        def body(j, mx):
            start = pl.multiple_of(j * tk, tk)
            kt = keys_ref[:, pl.ds(start, tk)]
            kt = jnp.where(kt <= bound, kt, INT_MIN)
            for c in range(tk // LANES):
                mx = jnp.maximum(mx, kt[:, c * LANES:(c + 1) * LANES])
            return mx
        mx = lax.fori_loop(0, n_tiles, body, jnp.full((tq, LANES), INT_MIN, jnp.int32))
        return jnp.max(mx, axis=-1, keepdims=True)

    def group_max(j, carry):
        g0, g1 = carry
        start = pl.multiple_of(j * tk, tk)
        kt = keys_ref[:, pl.ds(start, tk)]
        for c in range(0, tk // LANES, 2):
            g0 = jnp.maximum(g0, kt[:, c * LANES:(c + 1) * LANES])
            g1 = jnp.maximum(g1, kt[:, (c + 1) * LANES:(c + 2) * LANES])
        return g0, g1

    lowest = jnp.full((tq, LANES), INT_MIN, jnp.int32)
    g0, g1 = lax.fori_loop(0, n_tiles, group_max, (lowest, lowest))
    hi = jnp.max(jnp.maximum(g0, g1), axis=-1, keepdims=True)
    lo = jnp.min(jnp.minimum(g0, g1), axis=-1, keepdims=True)
    need = q_pos + 1 > topk
    lo = jnp.where(need, lo, KEY_NEG_INF + 1)
    hi = jnp.where(need, hi, KEY_NEG_INF + 1)

    def n_active(lo, hi, cnt_lo):
        return jnp.max(((lo < hi) & (cnt_lo != topk)).astype(jnp.int32))

    def halve_step(lo, hi, cnt_lo):
        mid = _sortable_key(0.5 * _sortable_key_inverse(lo) + 0.5 * _sortable_key_inverse(hi))
        mid = jnp.minimum(jnp.maximum(mid, lo + 1), hi)
        cnt = count_ge(mid)
        ok = cnt >= topk
        return jnp.where(ok, mid, lo), jnp.where(ok, hi, mid - 1), jnp.where(ok, cnt, cnt_lo)

    def snap_step(lo, hi, cnt_lo):
        mid = jnp.minimum(jnp.maximum(max_le(hi), lo + 1), hi)
        cnt = count_ge(mid)
        ok = cnt >= topk
        return jnp.where(ok, mid, lo), jnp.where(ok, mid, mid - 1), jnp.where(ok, cnt, cnt_lo)

    def narrow(carry):
        lo, hi, cnt_lo, trip, _ = carry
        lo, hi, cnt_lo = halve_step(lo, hi, cnt_lo)
        lo, hi, cnt_lo = lax.cond(trip >= SNAP_AFTER_TRIPS, snap_step, halve_step, lo, hi, cnt_lo)
        return lo, hi, cnt_lo, trip + 1, n_active(lo, hi, cnt_lo)

    cnt_lo = count_ge(lo)
    thr, _, cnt_thr, _, _ = lax.while_loop(
        lambda c: c[4] > 0, narrow, (lo, hi, cnt_lo, jnp.int32(0), n_active(lo, hi, cnt_lo)))
    thr = jnp.maximum(thr, KEY_NEG_INF + 1)

    tied = jnp.logical_and(need, cnt_thr > topk)
    keep_all = jnp.full((tq, 1), s_len, jnp.int32)

    def tie_cut():
        n_keep = topk - count_ge(thr + 1)

        def step(_, carry):
            jlo, jhi = carry
            jm = (jlo + jhi) >> 1

            def pred(kt, j):
                col = j * tk + lax.broadcasted_iota(jnp.int32, (tq, tk), 1)
                return jnp.logical_and(kt == thr, col <= jm)

            ok = count_where(pred) >= n_keep
            return jnp.where(ok, jlo, jm + 1), jnp.where(ok, jm, jhi)

        _, jhi = lax.fori_loop(0, (s_len - 1).bit_length(), step,
                               (jnp.zeros((tq, 1), jnp.int32), keep_all - 1))
        return jnp.where(tied, jhi, keep_all)

    jcut = lax.cond(jnp.max(tied.astype(jnp.int32)) > 0, tie_cut, lambda: keep_all)

    qd = qd_ref[...].reshape(DSA_HEADS * tq, DSA_HEAD_DIM)

    def attn_tile(j, carry):
        m, acc = carry
        start = pl.multiple_of(j * tka, tka)
        kt = kd_ref[pl.ds(start, tka), :]
        vt = vd_ref[pl.ds(start, tka), :]
        keys = keys_ref[:, pl.ds(start, tka)]
        col = j * tka + lax.broadcasted_iota(jnp.int32, (tq, tka), 1)
        sel = jnp.logical_or(keys > thr, jnp.logical_and(keys == thr, col <= jcut))
        s = lax.dot_general(qd, kt, _NT, preferred_element_type=F32)
        s = jnp.where(sel[None], s.reshape(DSA_HEADS, tq, tka), NEG_BIG)
        m_new = jnp.maximum(m, jnp.max(s, axis=-1, keepdims=True))
        p = jnp.exp2(s - m_new).astype(BF16)
        pv = jnp.dot(p.reshape(DSA_HEADS * tq, tka), vt, preferred_element_type=F32)
        acc = jnp.exp2(m - m_new) * acc + pv.reshape(DSA_HEADS, tq, LANES)
        return m_new, acc

    init = (jnp.full((DSA_HEADS, tq, 1), NEG_BIG, F32), jnp.zeros((DSA_HEADS, tq, LANES), F32))
    m, acc = lax.fori_loop(0, n_tiles * (tk // tka), attn_tile, init)
    for h in range(DSA_HEADS):
        out = acc[h, :, :DSA_HEAD_DIM] / acc[h, :, DSA_HEAD_DIM:DSA_HEAD_DIM + 1]
        o_ref[:, h * DSA_HEAD_DIM:(h + 1) * DSA_HEAD_DIM] = out.astype(BF16)


def _dsa_attention(qi, qd, ki, kd, vd, wi, b, s, tq, tk, tka):
    assert tk % tka == 0 and tk % (2 * LANES) == 0
    n = b * s
    nq = s // tq
    topk = min(TOPK_MAX, s // 4)
    hm = pl.BlockSpec((DSA_HEADS, tq, DSA_HEAD_DIM), lambda bi, qb: (0, bi * nq + qb, 0))
    kv = pl.BlockSpec((s, DSA_HEAD_DIM), lambda bi, qb: (bi, 0))
    return pl.pallas_call(
        functools.partial(_dsa_body, tq=tq, tk=tk, tka=tka, topk=topk, s_len=s),
        grid=(b, nq),
        in_specs=[hm, hm, kv, kv, pl.BlockSpec((s, LANES), lambda bi, qb: (bi, 0)),
                  pl.BlockSpec((tq, IDX_HEADS), lambda bi, qb: (bi * nq + qb, 0))],
        out_specs=pl.BlockSpec((tq, DSA_HEADS * DSA_HEAD_DIM), lambda bi, qb: (bi * nq + qb, 0)),
        out_shape=jax.ShapeDtypeStruct((n, DSA_HEADS * DSA_HEAD_DIM), BF16),
        scratch_shapes=[pltpu.VMEM((tq, s), jnp.int32),
                        pltpu.VMEM((DSA_HEADS * tq, tka), F32),
                        pltpu.VMEM((DSA_HEADS * tq, tka), BF16)],
        compiler_params=_cparams(("parallel", "arbitrary")),
        name="dsa_attention",
    )(qi, qd, ki, kd, vd, wi)


HALF_D = D_MODEL // 2
RUN_ALIGN = 8


def _pack_bf16_pairs(y):
    r = pltpu.bitcast(y.astype(BF16).astype(F32), jnp.uint32)
    return r[:, :HALF_D] | (r[:, HALF_D:] >> 16)


def _unpack_bf16_pairs(p):
    hi = pltpu.bitcast(p & jnp.uint32(0xFFFF0000), F32)
    lo = pltpu.bitcast(p << 16, F32)
    return jnp.concatenate([hi, lo], axis=1)


def _split_bf16(a):
    hi = a.astype(BF16)
    lo = (a - hi.astype(F32)).astype(BF16)
    return hi, lo


def _merge_body(x_ref, ya_ref, yb_ref, ga_ref, gb_ref, wa_ref, wb_ref, wo_ref, g_ref,
                rwh_ref, rwl_ref, rb_ref, x2_ref, h_ref, logit_ref):
    ma = jnp.dot(ya_ref[...], wa_ref[...], preferred_element_type=F32)
    mb = jnp.dot(yb_ref[...], wb_ref[...], preferred_element_type=F32)
    merged = ga_ref[...].astype(F32) * ma + gb_ref[...].astype(F32) * mb
    x2 = x_ref[...] + jnp.dot(merged.astype(BF16), wo_ref[...], preferred_element_type=F32)
    x2_ref[...] = x2
    h = x2 * lax.rsqrt(jnp.mean(x2 * x2, axis=-1, keepdims=True) + EPS) * g_ref[...]
    hh, hl = _split_bf16(h)
    h_ref[...] = hh
    logit_ref[...] = (jnp.dot(hh, rwh_ref[...], preferred_element_type=F32)
                      + jnp.dot(hh, rwl_ref[...], preferred_element_type=F32)
                      + jnp.dot(hl, rwh_ref[...], preferred_element_type=F32)) + rb_ref[...]


def _merge(x2d, y_a, y_b, ga, gb, w_ba, w_bb, w_out, ffn_norm_g, router_w, router_b, tm):
    n, d = x2d.shape
    rw = jnp.pad(router_w, ((0, 0), (0, LANES - N_EXPERTS)))
    rwh = rw.astype(BF16)
    rwl = (rw - rwh.astype(F32)).astype(BF16)
    rb = jnp.pad(router_b, (0, LANES - N_EXPERTS), constant_values=NEG_BIG).reshape(1, LANES)
    consts = (w_ba.astype(BF16), w_bb.astype(BF16), w_out.astype(BF16), ffn_norm_g.reshape(1, d),
              rwh, rwl, rb)
    row = lambda w_: pl.BlockSpec((tm, w_), lambda i: (i, 0))
    full = lambda a: pl.BlockSpec(a.shape, lambda i: (0,) * a.ndim)
    out_shape = (
        jax.ShapeDtypeStruct((n, d), F32),
        jax.ShapeDtypeStruct((n, d), BF16),
        jax.ShapeDtypeStruct((n, LANES), F32),
    )
    return pl.pallas_call(
        _merge_body,
        grid=(n // tm,),
        in_specs=[row(d), row(HALF_D), row(HALF_D), row(d), row(d)] + [full(a) for a in consts],
        out_specs=(row(d), row(d), row(LANES)),
        out_shape=out_shape,
        compiler_params=_cparams(("parallel",)),
        name="merge",
    )(x2d, y_a, y_b, ga, gb, *consts)


def _router_body(logit_ref, upper_ref, e_ref, gate_ref, lp_ref, bc_ref, carry_out_ref, cnt_ref,
                 carry_ref, *, tm):
    i = pl.program_id(0)

    @pl.when(i == 0)
    def _():
        carry_ref[...] = jnp.zeros_like(carry_ref)

    lane = lax.broadcasted_iota(jnp.int32, (tm, LANES), 1)
    work = logit_ref[...]
    experts, vals = [], []
    onehot = jnp.zeros((tm, LANES), F32)
    for _ in range(TOP_K):
        mx = jnp.max(work, axis=-1, keepdims=True)
        idx = jnp.min(jnp.where(work == mx, lane, LANES), axis=-1, keepdims=True)
        hit = lane == idx
        experts.append(idx)
        vals.append(mx)
        onehot = onehot + hit.astype(F32)
        work = jnp.where(hit, -jnp.inf, work)
    ex = [jnp.exp(v - vals[0]) for v in vals]
    denom = ex[0] + ex[1] + ex[2] + ex[3]
    for r in range(TOP_K):
        e_ref[:, r:r + 1] = experts[r]
        gate_ref[:, r:r + 1] = ex[r] / denom

    rr = lax.broadcasted_iota(jnp.int32, (tm, tm), 0)
    cc = lax.broadcasted_iota(jnp.int32, (tm, tm), 1)
    lower = (cc < rr).astype(BF16)
    prefix = jnp.dot(lower, onehot.astype(BF16), preferred_element_type=F32)
    bc = jnp.sum(onehot, axis=0, keepdims=True)
    bc = jnp.floor((bc + (RUN_ALIGN - 1)) * (1.0 / RUN_ALIGN)) * RUN_ALIGN
    bc8 = jnp.broadcast_to(bc, (8, LANES))
    boff = jnp.dot(bc8.astype(BF16), upper_ref[...], preferred_element_type=F32)[0:1, :]
    local = prefix + boff
    for r in range(TOP_K):
        lp = jnp.sum(jnp.where(lane == experts[r], local, 0.0), axis=-1, keepdims=True)
        lp_ref[:, r:r + 1] = lp.astype(jnp.int32)
    bc_ref[0] = bc8
    carry_out_ref[0] = carry_ref[...]
    total = carry_ref[...] + bc8
    carry_ref[...] = total
    cnt_ref[...] = total


def _router(logits, tm):
    assert tm <= 256
    n = logits.shape[0]
    nblk = n // tm
    upper = (jnp.arange(LANES)[:, None] < jnp.arange(LANES)[None, :]).astype(BF16)
    row = lambda w_: pl.BlockSpec((tm, w_), lambda i: (i, 0))
    blk = pl.BlockSpec((1, 8, LANES), lambda i: (i, 0, 0))
    out_shape = (
        jax.ShapeDtypeStruct((n, TOP_K), jnp.int32),
        jax.ShapeDtypeStruct((n, TOP_K), F32),
        jax.ShapeDtypeStruct((n, TOP_K), jnp.int32),
        jax.ShapeDtypeStruct((nblk, 8, LANES), F32),
        jax.ShapeDtypeStruct((nblk, 8, LANES), F32),
        jax.ShapeDtypeStruct((8, LANES), F32),
    )
    return pl.pallas_call(
        functools.partial(_router_body, tm=tm),
        grid=(nblk,),
        in_specs=[row(LANES), pl.BlockSpec((LANES, LANES), lambda i: (0, 0))],
        out_specs=(row(TOP_K), row(TOP_K), row(TOP_K), blk, blk,
                   pl.BlockSpec((8, LANES), lambda i: (0, 0))),
        out_shape=out_shape,
        scratch_shapes=[pltpu.VMEM((8, LANES), F32)],
        compiler_params=_cparams(("arbitrary",)),
        name="router",
    )(logits, upper)


def _local_rows(tm):
    return TOP_K * tm + N_EXPERTS * RUN_ALIGN


def _for_each_run_chunk(tbl_ref, tm, fn):
    sizes = [s for s in (1 << k for k in range(tm.bit_length())) if RUN_ALIGN <= s <= tm]

    def per_expert(e, _):
        length = tbl_ref[0, 0, e]
        src = tbl_ref[0, 0, N_EXPERTS + e]
        dst = tbl_ref[0, 0, 2 * N_EXPERTS + e]
        for size in sizes:

            @pl.when((length & size) != 0)
            def _(size=size):
                off = length & (size - 1)
                fn(pl.multiple_of(src + off, RUN_ALIGN), pl.multiple_of(dst + off, RUN_ALIGN), size)
        return 0

    lax.fori_loop(0, N_EXPERTS, per_expert, 0)


def _dispatch_body(tbl_ref, lpt_ref, h_ref, xbuf_in_ref, xbuf_ref, sorted_ref, sem, *, tm):
    del xbuf_in_ref
    h = h_ref[...]
    lpt = lpt_ref[0]
    chunk = tm
    for c in range(_local_rows(tm) // chunk):
        r_idx = c * chunk + lax.broadcasted_iota(jnp.int32, (chunk, tm), 0)
        sel = jnp.zeros((chunk, tm), F32)
        for r in range(TOP_K):
            sel = sel + (r_idx == lpt[r:r + 1, :]).astype(F32)
        rows = jnp.dot(sel.astype(BF16), h, preferred_element_type=F32)
        sorted_ref[c * chunk:(c + 1) * chunk, :] = _pack_bf16_pairs(rows)

    def copy(local_row, global_row, size):
        return pltpu.make_async_copy(sorted_ref.at[pl.ds(local_row, size)],
                                     xbuf_ref.at[pl.ds(global_row, size)], sem)

    _for_each_run_chunk(tbl_ref, tm, lambda s, d, n: copy(s, d, n).start())
    _for_each_run_chunk(tbl_ref, tm, lambda s, d, n: copy(s, d, n).wait())


def _dispatch(h2, lpt, tbl, n_rows, tm):
    n = h2.shape[0]
    xbuf0 = jnp.zeros((n_rows, HALF_D), jnp.uint32)
    return pl.pallas_call(
        functools.partial(_dispatch_body, tm=tm),
        grid=(n // tm,),
        in_specs=[pl.BlockSpec((1, 1, LANES), lambda i: (i, 0, 0), memory_space=pltpu.SMEM),
                  pl.BlockSpec((1, TOP_K, tm), lambda i: (i, 0, 0)),
                  pl.BlockSpec((tm, D_MODEL), lambda i: (i, 0)),
                  pl.BlockSpec(memory_space=pl.ANY)],
        out_specs=pl.BlockSpec(memory_space=pl.ANY),
        out_shape=jax.ShapeDtypeStruct((n_rows, HALF_D), jnp.uint32),
        scratch_shapes=[pltpu.VMEM((_local_rows(tm), HALF_D), jnp.uint32),
                        pltpu.SemaphoreType.DMA],
        input_output_aliases={3: 0},
        compiler_params=_cparams(("arbitrary",)),
        name="moe_dispatch",
    )(tbl, lpt, h2, xbuf0)


def _expert_body(blk_e_ref, n_used_ref, x_ref, wg_ref, wl_ref, bg_ref, bl_ref, wd_ref, bd_ref,
                 y_ref):
    del blk_e_ref
    i = pl.program_id(0)

    @pl.when(i < n_used_ref[0])
    def _():
        xb = _unpack_bf16_pairs(x_ref[...]).astype(BF16)
        glu = jnp.dot(xb, wg_ref[0], preferred_element_type=F32) + bg_ref[0]
        lin = jnp.dot(xb, wl_ref[0], preferred_element_type=F32) + bl_ref[0]
        glu = jnp.minimum(glu, SWIGLU_LIMIT)
        lin = jnp.clip(lin, -SWIGLU_LIMIT, SWIGLU_LIMIT)
        act = glu * jax.nn.sigmoid(SWIGLU_ALPHA * glu) * (lin + 1.0)
        y = jnp.dot(act.astype(BF16), wd_ref[0], preferred_element_type=F32) + bd_ref[0]
        y_ref[...] = _pack_bf16_pairs(y)

    @pl.when(i >= n_used_ref[0])
    def _():
        y_ref[...] = jnp.zeros_like(y_ref)


def _experts(xbuf, blk_e, n_used, w_gate_up, b_gate_up, w_down, b_down, tb):
    n_rows = xbuf.shape[0]
    d = D_MODEL
    wgu = w_gate_up.astype(BF16)
    wd = w_down.astype(BF16)
    bgu = b_gate_up.reshape(N_EXPERTS, 1, 2 * D_FF)
    bd = b_down.reshape(N_EXPERTS, 1, d)
    grid_spec = pltpu.PrefetchScalarGridSpec(
        num_scalar_prefetch=2,
        grid=(n_rows // tb,),
        in_specs=[
            pl.BlockSpec((tb, HALF_D), lambda i, be, nu: (i, 0)),
            pl.BlockSpec((1, d, D_FF), lambda i, be, nu: (be[i], 0, 0)),
            pl.BlockSpec((1, d, D_FF), lambda i, be, nu: (be[i], 0, 1)),
            pl.BlockSpec((1, 1, D_FF), lambda i, be, nu: (be[i], 0, 0)),
            pl.BlockSpec((1, 1, D_FF), lambda i, be, nu: (be[i], 0, 1)),
            pl.BlockSpec((1, D_FF, d), lambda i, be, nu: (be[i], 0, 0)),
            pl.BlockSpec((1, 1, d), lambda i, be, nu: (be[i], 0, 0)),
        ],
        out_specs=pl.BlockSpec((tb, HALF_D), lambda i, be, nu: (i, 0)),
    )
    return pl.pallas_call(
        _expert_body,
        grid_spec=grid_spec,
        out_shape=jax.ShapeDtypeStruct((n_rows, HALF_D), jnp.uint32),
        compiler_params=_cparams(("arbitrary",)),
        name="moe_experts",
    )(blk_e, n_used, xbuf, wgu, wgu, bgu, bgu, wd, bd)


def _combine_body(tbl_ref, x2_ref, gate_ref, lp_ref, g_ref, ybuf_ref, o_ref, ys_ref, sem, *, tm):
    n_local = _local_rows(tm)
    ys_ref[TOP_K * tm:, :] = jnp.zeros((n_local - TOP_K * tm, HALF_D), jnp.uint32)

    def copy(local_row, global_row, size):
        return pltpu.make_async_copy(ybuf_ref.at[pl.ds(global_row, size)],
                                     ys_ref.at[pl.ds(local_row, size)], sem)

    _for_each_run_chunk(tbl_ref, tm, lambda s, d, n: copy(s, d, n).start())
    _for_each_run_chunk(tbl_ref, tm, lambda s, d, n: copy(s, d, n).wait())

    gate = gate_ref[...]
    lp = lp_ref[...]
    col = lax.broadcasted_iota(jnp.int32, (tm, n_local), 1)
    gmat = jnp.zeros((tm, n_local), F32)
    for r in range(TOP_K):
        gmat = gmat + jnp.where(col == lp[:, r:r + 1], gate[:, r:r + 1], 0.0)
    g_hi, g_lo = _split_bf16(gmat)
    ys = _unpack_bf16_pairs(ys_ref[...]).astype(BF16)
    out = (x2_ref[...] + jnp.dot(g_hi, ys, preferred_element_type=F32)
           + jnp.dot(g_lo, ys, preferred_element_type=F32))
    o_ref[...] = out * lax.rsqrt(jnp.mean(out * out, axis=-1, keepdims=True) + EPS) * g_ref[...]


def _combine(x2, gate, lp, tbl, ybuf, final_norm_g, tm):
    n, d = x2.shape
    return pl.pallas_call(
        functools.partial(_combine_body, tm=tm),
        grid=(n // tm,),
        in_specs=[pl.BlockSpec((1, 1, LANES), lambda i: (i, 0, 0), memory_space=pltpu.SMEM),
                  pl.BlockSpec((tm, d), lambda i: (i, 0)),
                  pl.BlockSpec((tm, TOP_K), lambda i: (i, 0)),
                  pl.BlockSpec((tm, TOP_K), lambda i: (i, 0)),
                  pl.BlockSpec((1, d), lambda i: (0, 0)),
                  pl.BlockSpec(memory_space=pl.ANY)],
        out_specs=pl.BlockSpec((tm, d), lambda i: (i, 0)),
        out_shape=jax.ShapeDtypeStruct((n, d), F32),
        scratch_shapes=[pltpu.VMEM((_local_rows(tm), HALF_D), jnp.uint32),
                        pltpu.SemaphoreType.DMA],
        compiler_params=_cparams(("arbitrary",)),
        name="moe_combine",
    )(tbl, x2, gate, lp, final_norm_g.reshape(1, d), ybuf)


def _moe(x2, h2, logits, w_gate_up, b_gate_up, w_down, b_down, final_norm_g, tb, tm):
    n = x2.shape[0]
    nblk = n // tm
    max_rows = n * TOP_K + nblk * N_EXPERTS * (RUN_ALIGN - 1)
    n_blk = -(-max_rows // tb) + N_EXPERTS
    top_e, gate, lp, bcount, before, counts = _router(logits, tm)
    del top_e
    cnt = counts[0, :N_EXPERTS].astype(jnp.int32)
    padded = (cnt + tb - 1) // tb * tb
    pad_end = jnp.cumsum(padded)
    pad_start = pad_end - padded
    blk_first = jnp.arange(n_blk, dtype=jnp.int32) * tb
    blk_e = jnp.minimum(jnp.sum((pad_end[None, :] <= blk_first[:, None]).astype(jnp.int32), axis=1),
                        N_EXPERTS - 1)
    n_used = (pad_end[-1:] // tb).astype(jnp.int32)
    run_len = bcount[:, 0, :N_EXPERTS].astype(jnp.int32)
    run_src = jnp.cumsum(run_len, axis=1) - run_len
    run_dst = pad_start[None, :] + before[:, 0, :N_EXPERTS].astype(jnp.int32)
    tbl = jnp.concatenate([run_len, run_src, run_dst,
                           jnp.zeros((nblk, LANES - 3 * N_EXPERTS), jnp.int32)], axis=1)
    tbl = tbl.reshape(nblk, 1, LANES)
    lpt = lp.reshape(nblk, tm, TOP_K).transpose(0, 2, 1)
    xbuf = _dispatch(h2, lpt, tbl, n_blk * tb, tm)
    ybuf = _experts(xbuf, blk_e, n_used, w_gate_up, b_gate_up, w_down, b_down, tb)
    return _combine(x2, gate, lp, tbl, ybuf, final_norm_g, tm)


def kernel(x, positions, attn_norm_g, w_in, mla_q_norm_g, mla_w_uq, mla_kv_norm_g, mla_w_ukv,
           w_branch_mla, w_branch_dsa, w_out, ffn_norm_g, router_w, router_b,
           w_gate_up, b_gate_up, w_down, b_down, final_norm_g):
    b, s, d = x.shape
    n = b * s
    (qm, km, vm, qd, qi, kd, ki, vd, wi, ga, gb) = _in_proj(
        x.reshape(n, d), positions.reshape(n), attn_norm_g[0], w_in[0],
        mla_q_norm_g[0], mla_w_uq[0], mla_kv_norm_g[0], mla_w_ukv[0], tm=512)
    y_a = _mla_attention(qm, km, vm, b, s, tq=1024, tk=1024)
    y_b = _dsa_attention(qi, qd, ki, kd, vd, wi, b, s, tq=128, tk=1024, tka=1024)
    x2, h2, logits = _merge(
        x.reshape(n, d), y_a, y_b, ga, gb, w_branch_mla[0], w_branch_dsa[0], w_out[0],
        ffn_norm_g[0], router_w[0], router_b[0], tm=512)
    out = _moe(x2, h2, logits, w_gate_up[0], b_gate_up[0], w_down[0], b_down[0], final_norm_g,
               tb=512, tm=256)
    return out.reshape(b, s, d)
```

```python
import functools
import math

import jax
import jax.numpy as jnp
from jax import lax
from jax.experimental import pallas as pl
from jax.experimental.pallas import tpu as pltpu

F32 = jnp.float32
BF16 = jnp.bfloat16

LANES = 128
LOG2_E = 1.4426950408889634

D_MODEL = 1024
EPS = 1e-6
ROPE_THETA = 500000.0
MLA_HEADS = 8
MLA_Q_LORA = 256
MLA_KV_LORA = 128
MLA_NOPE = 64
MLA_ROPE = 32
MLA_V = 64
DSA_HEADS = 8
DSA_HEAD_DIM = 64
DSA_ROT = 16
IDX_HEADS = 8
IDX_DIM = 64
TOPK_MAX = 256
N_EXPERTS = 32
TOP_K = 4
D_FF = 1024
SWIGLU_LIMIT = 7.0
SWIGLU_ALPHA = 1.702

VMEM_LIMIT = 56 * 1024 * 1024

_OFF_CQ = 0
_OFF_CKV = _OFF_CQ + MLA_Q_LORA
_OFF_KPE = _OFF_CKV + MLA_KV_LORA
_OFF_QB = _OFF_KPE + MLA_ROPE
_OFF_KB = _OFF_QB + DSA_HEADS * DSA_HEAD_DIM
_OFF_VB = _OFF_KB + DSA_HEAD_DIM
_OFF_QI = _OFF_VB + DSA_HEAD_DIM
_OFF_KI = _OFF_QI + IDX_HEADS * IDX_DIM
_OFF_WI = _OFF_KI + IDX_DIM
_OFF_GA = _OFF_WI + IDX_HEADS
_OFF_GB = _OFF_GA + D_MODEL
_D_IN = _OFF_GB + D_MODEL


def _cparams(sem):
    return pltpu.CompilerParams(dimension_semantics=sem, vmem_limit_bytes=VMEM_LIMIT)


def _rope_group(xg, c, slo, shi, shift):
    return (xg * c + pltpu.roll(xg, LANES - shift, 1) * slo
            + pltpu.roll(xg, shift, 1) * shi)


def _in_proj_body(x_ref, g_ref, w1_ref, w2_ref, w3_ref, w5_ref, qng_ref, wuq_ref,
                  kvng_ref, wuk_ref, wuv_ref, em_ref, bm_ref, ed_ref, bd_ref, csm_ref, csd_ref,
                  qm_ref, km_ref, vm_ref, qd_ref, qi_ref, kd_ref, ki_ref, vd_ref,
                  wi_ref, ga_ref, gb_ref):
    x = x_ref[...]
    h = x * lax.rsqrt(jnp.mean(x * x, axis=-1, keepdims=True) + EPS) * g_ref[...]
    hb = h.astype(BF16)

    def spread(cs_ref, e_ref, b_ref):
        hi, lo = _split_bf16(cs_ref[...])
        t = (jnp.dot(hi, e_ref[...], preferred_element_type=F32)
             + jnp.dot(lo, e_ref[...], preferred_element_type=F32))
        return t[:, :LANES] + b_ref[...], t[:, LANES:2 * LANES], t[:, 2 * LANES:]

    cm, slm, shm = spread(csm_ref, em_ref, bm_ref)
    cd, sld, shd = spread(csd_ref, ed_ref, bd_ref)

    z1 = jnp.dot(hb, w1_ref[...], preferred_element_type=F32)
    cq = z1[:, 0:MLA_Q_LORA]
    cqn = cq * lax.rsqrt(jnp.mean(cq * cq, axis=-1, keepdims=True) + EPS) * qng_ref[...]
    q = jnp.dot(cqn.astype(BF16), wuq_ref[...], preferred_element_type=F32)
    q_scale = (MLA_NOPE + MLA_ROPE) ** -0.5 * LOG2_E
    for j in range(MLA_HEADS):
        qg = _rope_group(q[:, j * LANES:(j + 1) * LANES], cm, slm, shm, MLA_ROPE // 2)
        qm_ref[:, j * LANES:(j + 1) * LANES] = (qg * q_scale).astype(BF16)

    ckv = z1[:, MLA_Q_LORA:MLA_Q_LORA + MLA_KV_LORA]
    ckvn = (ckv * lax.rsqrt(jnp.mean(ckv * ckv, axis=-1, keepdims=True) + EPS)
            * kvng_ref[...]).astype(BF16)
    kpe = _rope_group(z1[:, MLA_Q_LORA + MLA_KV_LORA:], cm, slm, shm, MLA_ROPE // 2)
    kn = jnp.dot(ckvn, wuk_ref[...], preferred_element_type=F32)
    for j in range(MLA_HEADS):
        km_ref[:, j * LANES:(j + 1) * LANES] = (kn[:, j * LANES:(j + 1) * LANES] + kpe).astype(BF16)
    vm_ref[...] = jnp.dot(ckvn, wuv_ref[...], preferred_element_type=F32).astype(BF16)

    z2 = jnp.dot(hb, w2_ref[...], preferred_element_type=F32)
    d_scale = DSA_HEAD_DIM ** -0.5 * LOG2_E
    for j in range(DSA_HEADS // 2):
        g = _rope_group(z2[:, j * LANES:(j + 1) * LANES], cd, sld, shd, DSA_ROT // 2) * d_scale
        qd_ref[2 * j] = g[:, :DSA_HEAD_DIM].astype(BF16)
        qd_ref[2 * j + 1] = g[:, DSA_HEAD_DIM:].astype(BF16)
    base = DSA_HEADS * DSA_HEAD_DIM
    for j in range(IDX_HEADS // 2):
        g = _rope_group(z2[:, base + j * LANES:base + (j + 1) * LANES], cd, sld, shd, DSA_ROT // 2)
        qi_ref[2 * j] = g[:, :IDX_DIM].astype(BF16)
        qi_ref[2 * j + 1] = g[:, IDX_DIM:].astype(BF16)

    z3 = jnp.dot(hb, w3_ref[...], preferred_element_type=F32)
    kb = _rope_group(z3[:, :LANES], cd, sld, shd, DSA_ROT // 2)
    kd_ref[...] = kb[:, :DSA_HEAD_DIM].astype(BF16)
    ki_ref[...] = kb[:, DSA_HEAD_DIM:].astype(BF16)
    zv = z3[:, LANES:]
    vlane = lax.broadcasted_iota(jnp.int32, zv.shape, 1)
    vd_ref[...] = jnp.where(vlane < DSA_HEAD_DIM, zv,
                            jnp.where(vlane == DSA_HEAD_DIM, 1.0, 0.0)).astype(BF16)
    w_scale = IDX_HEADS ** -0.5 * IDX_DIM ** -0.5
    wi_ref[...] = z3[:, LANES + DSA_HEAD_DIM:LANES + DSA_HEAD_DIM + IDX_HEADS] * w_scale

    z5 = jnp.dot(hb, w5_ref[...], preferred_element_type=F32)
    ga_ref[...] = jax.nn.sigmoid(z5[:, :D_MODEL]).astype(BF16)
    gb_ref[...] = jax.nn.sigmoid(z5[:, D_MODEL:]).astype(BF16)


ROPE_COLS = 16


def _rope_tables(pos, rot_dim, lane_of_x1, period):
    half = rot_dim // 2
    inv_freq = ROPE_THETA ** (-jnp.arange(half, dtype=F32) / half)
    ang = pos.astype(F32)[:, None] * inv_freq
    pad = ((0, 0), (0, ROPE_COLS - half))
    cs = jnp.concatenate([jnp.pad(jnp.cos(ang), pad), jnp.pad(jnp.sin(ang), pad)], axis=1)

    lane = jnp.arange(LANES)
    in_period = lane % period - lane_of_x1
    freq = jnp.arange(ROPE_COLS)[:, None]
    on_x1 = (in_period[None, :] == freq) & (freq < half)
    on_x2 = (in_period[None, :] - half == freq) & (freq < half)
    zero = jnp.zeros((ROPE_COLS, LANES), F32)
    f = lambda m: m.astype(F32)
    spread = jnp.concatenate([
        jnp.concatenate([f(on_x1 | on_x2), zero, zero], axis=1),
        jnp.concatenate([zero, -f(on_x1), f(on_x2)], axis=1),
    ], axis=0).astype(BF16)
    rotated = (in_period >= 0) & (in_period < rot_dim)
    bias = jnp.where(rotated, 0.0, 1.0).astype(F32).reshape(1, LANES)
    return cs, spread, bias


def _head_cols(w, n_heads, widths, total):
    k = w.shape[0]
    per = sum(widths)
    w = w.reshape(k, n_heads, per)
    return jnp.pad(w, ((0, 0), (0, 0), (0, total - per))).reshape(k, n_heads * total)


def _in_proj(x2, pos, attn_norm_g, w_in, q_norm_g, w_uq, kv_norm_g, w_ukv, tm):
    n = x2.shape[0]
    d = D_MODEL
    zc = lambda k: jnp.zeros((d, k), F32)
    w1 = jnp.concatenate([w_in[:, _OFF_CQ:_OFF_KPE], zc(MLA_NOPE), w_in[:, _OFF_KPE:_OFF_QB],
                          zc(LANES - MLA_NOPE - MLA_ROPE)], axis=1).astype(BF16)
    w2 = jnp.concatenate([w_in[:, _OFF_QB:_OFF_KB], w_in[:, _OFF_QI:_OFF_KI]], axis=1).astype(BF16)
    w3 = jnp.concatenate([w_in[:, _OFF_KB:_OFF_VB], w_in[:, _OFF_KI:_OFF_WI],
                          w_in[:, _OFF_VB:_OFF_QI], w_in[:, _OFF_WI:_OFF_GA],
                          zc(LANES - DSA_HEAD_DIM - IDX_HEADS)], axis=1).astype(BF16)
    w5 = w_in[:, _OFF_GA:].astype(BF16)
    wuq = _head_cols(w_uq, MLA_HEADS, (MLA_NOPE, MLA_ROPE), LANES).astype(BF16)
    w_ukv3 = w_ukv.reshape(MLA_KV_LORA, MLA_HEADS, MLA_NOPE + MLA_V)
    wuk = jnp.pad(w_ukv3[:, :, :MLA_NOPE], ((0, 0), (0, 0), (0, LANES - MLA_NOPE))
                  ).reshape(MLA_KV_LORA, MLA_HEADS * LANES).astype(BF16)
    wuv = w_ukv3[:, :, MLA_NOPE:].reshape(MLA_KV_LORA, MLA_HEADS * MLA_V).astype(BF16)
    csm, em, bm = _rope_tables(pos, MLA_ROPE, MLA_NOPE, LANES)
    csd, ed, bdd = _rope_tables(pos, DSA_ROT, 0, DSA_HEAD_DIM)

    row = lambda w_: pl.BlockSpec((tm, w_), lambda i: (i, 0))
    full = lambda a: pl.BlockSpec(a.shape, lambda i: (0,) * a.ndim)
    hm = pl.BlockSpec((DSA_HEADS, tm, DSA_HEAD_DIM), lambda i: (0, i, 0))
    g2 = attn_norm_g.reshape(1, d)
    qng = q_norm_g.reshape(1, -1)
    kvng = kv_norm_g.reshape(1, -1)
    consts = (g2, w1, w2, w3, w5, qng, wuq, kvng, wuk, wuv, em, bm, ed, bdd)
    out_shape = (
        jax.ShapeDtypeStruct((n, MLA_HEADS * LANES), BF16),
        jax.ShapeDtypeStruct((n, MLA_HEADS * LANES), BF16),
        jax.ShapeDtypeStruct((n, MLA_HEADS * MLA_V), BF16),
        jax.ShapeDtypeStruct((DSA_HEADS, n, DSA_HEAD_DIM), BF16),
        jax.ShapeDtypeStruct((IDX_HEADS, n, IDX_DIM), BF16),
        jax.ShapeDtypeStruct((n, DSA_HEAD_DIM), BF16),
        jax.ShapeDtypeStruct((n, IDX_DIM), BF16),
        jax.ShapeDtypeStruct((n, LANES), BF16),
        jax.ShapeDtypeStruct((n, IDX_HEADS), F32),
        jax.ShapeDtypeStruct((n, D_MODEL), BF16),
        jax.ShapeDtypeStruct((n, D_MODEL), BF16),
    )
    out_specs = (row(MLA_HEADS * LANES), row(MLA_HEADS * LANES), row(MLA_HEADS * MLA_V),
                 hm, hm, row(DSA_HEAD_DIM), row(IDX_DIM), row(LANES), row(IDX_HEADS),
                 row(D_MODEL), row(D_MODEL))
    return pl.pallas_call(
        _in_proj_body,
        grid=(n // tm,),
        in_specs=[row(d)] + [full(a) for a in consts] + [row(2 * ROPE_COLS)] * 2,
        out_specs=out_specs,
        out_shape=out_shape,
        compiler_params=_cparams(("parallel",)),
        name="in_proj",
    )(x2, *consts, csm, csd)


NEG_BIG = -1e30
_NT = (((1,), (1,)), ((), ()))


def _mla_body(q_ref, k_ref, v_ref, o_ref, *, tq, tk):
    qi = pl.program_id(2)
    n_sub = tq // tk
    qs = [q_ref[:, hh * LANES:(hh + 1) * LANES] for hh in range(2)]

    def step(j, carry, masked):
        start = pl.multiple_of(j * tk, tk)
        vs = v_ref[pl.ds(start, tk), :]
        new = []
        for hh in range(2):
            m, l, acc = carry[hh]
            ks = k_ref[pl.ds(start, tk), hh * LANES:(hh + 1) * LANES]
            s = lax.dot_general(qs[hh], ks, _NT, preferred_element_type=F32)
            if masked:
                row = qi * tq + lax.broadcasted_iota(jnp.int32, (tq, tk), 0)
                col = j * tk + lax.broadcasted_iota(jnp.int32, (tq, tk), 1)
                s = jnp.where(col <= row, s, NEG_BIG)
            m_new = jnp.maximum(m, jnp.max(s, axis=-1, keepdims=True))
            alpha = jnp.exp2(m - m_new)
            p = jnp.exp2(s - m_new)
            l = alpha * l + jnp.sum(p, axis=-1, keepdims=True)
            acc = alpha * acc + jnp.dot(p.astype(BF16), vs, preferred_element_type=F32)
            new.append((m_new, l, acc))
        return tuple(new)

    one = (jnp.full((tq, 1), NEG_BIG, F32), jnp.zeros((tq, 1), F32), jnp.zeros((tq, LANES), F32))
    carry = lax.fori_loop(0, qi * n_sub, functools.partial(step, masked=False), (one, one))
    for dd in range(n_sub):
        carry = step(qi * n_sub + dd, carry, True)
    outs = [acc / l for (_, l, acc) in carry]
    lane = lax.broadcasted_iota(jnp.int32, (tq, LANES), 1)
    o_ref[...] = jnp.where(lane < MLA_V, outs[0], outs[1]).astype(BF16)


def _mla_attention(qm, km, vm, b, s, tq, tk):
    n = b * s
    nq = s // tq
    return pl.pallas_call(
        functools.partial(_mla_body, tq=tq, tk=tk),
        grid=(b, MLA_HEADS // 2, nq),
        in_specs=[
            pl.BlockSpec((tq, 2 * LANES), lambda bi, hp, qi: (bi * nq + qi, hp)),
            pl.BlockSpec((s, 2 * LANES), lambda bi, hp, qi: (bi, hp)),
            pl.BlockSpec((s, 2 * MLA_V), lambda bi, hp, qi: (bi, hp)),
        ],
        out_specs=pl.BlockSpec((tq, 2 * MLA_V), lambda bi, hp, qi: (bi * nq + qi, hp)),
        out_shape=jax.ShapeDtypeStruct((n, MLA_HEADS * MLA_V), BF16),
        compiler_params=_cparams(("parallel", "parallel", "arbitrary")),
        name="mla_attention",
    )(qm, km, vm)


INT_MIN = -2 ** 31
SNAP_AFTER_TRIPS = 5
KEY_NEG_INF = (0xFF800000 - 2 ** 32) ^ 0x7FFFFFFF


def _sortable_key(score):
    bits = pltpu.bitcast(score, jnp.int32)
    return bits ^ ((bits >> 31) & 0x7FFFFFFF)


def _sortable_key_inverse(key):
    return pltpu.bitcast(key ^ ((key >> 31) & 0x7FFFFFFF), F32)


def _dsa_body(qi_ref, qd_ref, ki_ref, kd_ref, vd_ref, w_ref, o_ref, keys_ref, s_ref, p_ref, *,
              tq, tk, tka, topk, s_len):
    qb = pl.program_id(1)
    n_tiles = (qb * tq + tq + tk - 1) // tk
    q_pos = qb * tq + lax.broadcasted_iota(jnp.int32, (tq, 1), 0)

    qidx = qi_ref[...].reshape(IDX_HEADS * tq, IDX_DIM)
    w = w_ref[...]
    wcols = [jnp.broadcast_to(w[:, h:h + 1], (tq, tk)) for h in range(IDX_HEADS)]

    def score_tile(j, _):
        start = pl.multiple_of(j * tk, tk)
        kt = ki_ref[pl.ds(start, tk), :]
        sh = lax.dot_general(qidx, kt, _NT, preferred_element_type=F32)
        sc = jnp.zeros((tq, tk), F32)
        for h in range(IDX_HEADS):
            sc = sc + jnp.maximum(sh[h * tq:(h + 1) * tq], 0.0) * wcols[h]
        col = j * tk + lax.broadcasted_iota(jnp.int32, (tq, tk), 1)
        sc = jnp.where(col <= q_pos, sc, -jnp.inf)
        keys_ref[:, pl.ds(start, tk)] = _sortable_key(sc)
        return 0

    lax.fori_loop(0, n_tiles, score_tile, 0)

    def count_where(pred):
        def body(j, cnt):
            start = pl.multiple_of(j * tk, tk)
            hit = pred(keys_ref[:, pl.ds(start, tk)], j).astype(jnp.int32)
            for c in range(tk // LANES):
                cnt = cnt + hit[:, c * LANES:(c + 1) * LANES]
            return cnt
        cnt = lax.fori_loop(0, n_tiles, body, jnp.zeros((tq, LANES), jnp.int32))
        return jnp.sum(cnt, axis=-1, keepdims=True)

    def count_ge(cand):
        return count_where(lambda kt, j: kt >= cand)

    def max_le(bound):
        def body(j, mx):
            start = pl.multiple_of(j * tk, tk)
            kt = keys_ref[:, pl.ds(start, tk)]
            kt = jnp.where(kt <= bound, kt, INT_MIN)
            for c in range(tk // LANES):
                mx = jnp.maximum(mx, kt[:, c * LANES:(c + 1) * LANES])
            return mx
        mx = lax.fori_loop(0, n_tiles, body, jnp.full((tq, LANES), INT_MIN, jnp.int32))
        return jnp.max(mx, axis=-1, keepdims=True)

    def group_max(j, carry):
        g0, g1 = carry
        start = pl.multiple_of(j * tk, tk)
        kt = keys_ref[:, pl.ds(start, tk)]
        for c in range(0, tk // LANES, 2):
            g0 = jnp.maximum(g0, kt[:, c * LANES:(c + 1) * LANES])
            g1 = jnp.maximum(g1, kt[:, (c + 1) * LANES:(c + 2) * LANES])
        return g0, g1

    lowest = jnp.full((tq, LANES), INT_MIN, jnp.int32)
    g0, g1 = lax.fori_loop(0, n_tiles, group_max, (lowest, lowest))
    hi = jnp.max(jnp.maximum(g0, g1), axis=-1, keepdims=True)
    lo = jnp.min(jnp.minimum(g0, g1), axis=-1, keepdims=True)
    need = q_pos + 1 > topk
    lo = jnp.where(need, lo, KEY_NEG_INF + 1)
    hi = jnp.where(need, hi, KEY_NEG_INF + 1)

    def n_active(lo, hi, cnt_lo):
        return jnp.max(((lo < hi) & (cnt_lo != topk)).astype(jnp.int32))

    def halve_step(lo, hi, cnt_lo):
        mid = _sortable_key(0.5 * _sortable_key_inverse(lo) + 0.5 * _sortable_key_inverse(hi))
        mid = jnp.minimum(jnp.maximum(mid, lo + 1), hi)
        cnt = count_ge(mid)
        ok = cnt >= topk
        return jnp.where(ok, mid, lo), jnp.where(ok, hi, mid - 1), jnp.where(ok, cnt, cnt_lo)

    def snap_step(lo, hi, cnt_lo):
        mid = jnp.minimum(jnp.maximum(max_le(hi), lo + 1), hi)
        cnt = count_ge(mid)
        ok = cnt >= topk
        return jnp.where(ok, mid, lo), jnp.where(ok, mid, mid - 1), jnp.where(ok, cnt, cnt_lo)

    def narrow(carry):
        lo, hi, cnt_lo, trip, _ = carry
        lo, hi, cnt_lo = halve_step(lo, hi, cnt_lo)
        lo, hi, cnt_lo = lax.cond(trip >= SNAP_AFTER_TRIPS, snap_step, halve_step, lo, hi, cnt_lo)
        return lo, hi, cnt_lo, trip + 1, n_active(lo, hi, cnt_lo)

    cnt_lo = count_ge(lo)
    thr, _, cnt_thr, _, _ = lax.while_loop(
        lambda c: c[4] > 0, narrow, (lo, hi, cnt_lo, jnp.int32(0), n_active(lo, hi, cnt_lo)))
    thr = jnp.maximum(thr, KEY_NEG_INF + 1)

    tied = jnp.logical_and(need, cnt_thr > topk)
    keep_all = jnp.full((tq, 1), s_len, jnp.int32)

    def tie_cut():
        n_keep = topk - count_ge(thr + 1)

        def step(_, carry):
            jlo, jhi = carry
            jm = (jlo + jhi) >> 1

            def pred(kt, j):
                col = j * tk + lax.broadcasted_iota(jnp.int32, (tq, tk), 1)
                return jnp.logical_and(kt == thr, col <= jm)

            ok = count_where(pred) >= n_keep
            return jnp.where(ok, jlo, jm + 1), jnp.where(ok, jm, jhi)

        _, jhi = lax.fori_loop(0, (s_len - 1).bit_length(), step,
                               (jnp.zeros((tq, 1), jnp.int32), keep_all - 1))
        return jnp.where(tied, jhi, keep_all)

    jcut = lax.cond(jnp.max(tied.astype(jnp.int32)) > 0, tie_cut, lambda: keep_all)

    qd = qd_ref[...].reshape(DSA_HEADS * tq, DSA_HEAD_DIM)

    def attn_tile(j, carry):
        m, acc = carry
        start = pl.multiple_of(j * tka, tka)
        kt = kd_ref[pl.ds(start, tka), :]
        vt = vd_ref[pl.ds(start, tka), :]
        keys = keys_ref[:, pl.ds(start, tka)]
        col = j * tka + lax.broadcasted_iota(jnp.int32, (tq, tka), 1)
        sel = jnp.logical_or(keys > thr, jnp.logical_and(keys == thr, col <= jcut))
        s = lax.dot_general(qd, kt, _NT, preferred_element_type=F32)
        s = jnp.where(sel[None], s.reshape(DSA_HEADS, tq, tka), NEG_BIG)
        m_new = jnp.maximum(m, jnp.max(s, axis=-1, keepdims=True))
        p = jnp.exp2(s - m_new).astype(BF16)
        pv = jnp.dot(p.reshape(DSA_HEADS * tq, tka), vt, preferred_element_type=F32)
        acc = jnp.exp2(m - m_new) * acc + pv.reshape(DSA_HEADS, tq, LANES)
        return m_new, acc

    init = (jnp.full((DSA_HEADS, tq, 1), NEG_BIG, F32), jnp.zeros((DSA_HEADS, tq, LANES), F32))
    m, acc = lax.fori_loop(0, n_tiles * (tk // tka), attn_tile, init)
    for h in range(DSA_HEADS):
        out = acc[h, :, :DSA_HEAD_DIM] / acc[h, :, DSA_HEAD_DIM:DSA_HEAD_DIM + 1]
        o_ref[:, h * DSA_HEAD_DIM:(h + 1) * DSA_HEAD_DIM] = out.astype(BF16)


def _dsa_attention(qi, qd, ki, kd, vd, wi, b, s, tq, tk, tka):
    assert tk % tka == 0 and tk % (2 * LANES) == 0
    n = b * s
    nq = s // tq
    topk = min(TOPK_MAX, s // 4)
    hm = pl.BlockSpec((DSA_HEADS, tq, DSA_HEAD_DIM), lambda bi, qb: (0, bi * nq + qb, 0))
    kv = pl.BlockSpec((s, DSA_HEAD_DIM), lambda bi, qb: (bi, 0))
    return pl.pallas_call(
        functools.partial(_dsa_body, tq=tq, tk=tk, tka=tka, topk=topk, s_len=s),
        grid=(b, nq),
        in_specs=[hm, hm, kv, kv, pl.BlockSpec((s, LANES), lambda bi, qb: (bi, 0)),
                  pl.BlockSpec((tq, IDX_HEADS), lambda bi, qb: (bi * nq + qb, 0))],
        out_specs=pl.BlockSpec((tq, DSA_HEADS * DSA_HEAD_DIM), lambda bi, qb: (bi * nq + qb, 0)),
        out_shape=jax.ShapeDtypeStruct((n, DSA_HEADS * DSA_HEAD_DIM), BF16),
        scratch_shapes=[pltpu.VMEM((tq, s), jnp.int32),
                        pltpu.VMEM((DSA_HEADS * tq, tka), F32),
                        pltpu.VMEM((DSA_HEADS * tq, tka), BF16)],
        compiler_params=_cparams(("parallel", "arbitrary")),
        name="dsa_attention",
    )(qi, qd, ki, kd, vd, wi)


HALF_D = D_MODEL // 2
RUN_ALIGN = 8


def _pack_bf16_pairs(y):
    r = pltpu.bitcast(y.astype(BF16).astype(F32), jnp.uint32)
    return r[:, :HALF_D] | (r[:, HALF_D:] >> 16)


def _unpack_bf16_pairs(p):
    hi = pltpu.bitcast(p & jnp.uint32(0xFFFF0000), F32)
    lo = pltpu.bitcast(p << 16, F32)
    return jnp.concatenate([hi, lo], axis=1)


def _split_bf16(a):
    hi = a.astype(BF16)
    lo = (a - hi.astype(F32)).astype(BF16)
    return hi, lo


def _merge_body(x_ref, ya_ref, yb_ref, ga_ref, gb_ref, wa_ref, wb_ref, wo_ref, g_ref,
                rwh_ref, rwl_ref, rb_ref, x2_ref, h_ref, logit_ref):
    ma = jnp.dot(ya_ref[...], wa_ref[...], preferred_element_type=F32)
    mb = jnp.dot(yb_ref[...], wb_ref[...], preferred_element_type=F32)
    merged = ga_ref[...].astype(F32) * ma + gb_ref[...].astype(F32) * mb
    x2 = x_ref[...] + jnp.dot(merged.astype(BF16), wo_ref[...], preferred_element_type=F32)
    x2_ref[...] = x2
    h = x2 * lax.rsqrt(jnp.mean(x2 * x2, axis=-1, keepdims=True) + EPS) * g_ref[...]
    hh, hl = _split_bf16(h)
    h_ref[...] = hh
    logit_ref[...] = (jnp.dot(hh, rwh_ref[...], preferred_element_type=F32)
                      + jnp.dot(hh, rwl_ref[...], preferred_element_type=F32)
                      + jnp.dot(hl, rwh_ref[...], preferred_element_type=F32)) + rb_ref[...]


def _merge(x2d, y_a, y_b, ga, gb, w_ba, w_bb, w_out, ffn_norm_g, router_w, router_b, tm):
    n, d = x2d.shape
    rw = jnp.pad(router_w, ((0, 0), (0, LANES - N_EXPERTS)))
    rwh = rw.astype(BF16)
    rwl = (rw - rwh.astype(F32)).astype(BF16)
    rb = jnp.pad(router_b, (0, LANES - N_EXPERTS), constant_values=NEG_BIG).reshape(1, LANES)
    consts = (w_ba.astype(BF16), w_bb.astype(BF16), w_out.astype(BF16), ffn_norm_g.reshape(1, d),
              rwh, rwl, rb)
    row = lambda w_: pl.BlockSpec((tm, w_), lambda i: (i, 0))
    full = lambda a: pl.BlockSpec(a.shape, lambda i: (0,) * a.ndim)
    out_shape = (
        jax.ShapeDtypeStruct((n, d), F32),
        jax.ShapeDtypeStruct((n, d), BF16),
        jax.ShapeDtypeStruct((n, LANES), F32),
    )
    return pl.pallas_call(
        _merge_body,
        grid=(n // tm,),
        in_specs=[row(d), row(HALF_D), row(HALF_D), row(d), row(d)] + [full(a) for a in consts],
        out_specs=(row(d), row(d), row(LANES)),
        out_shape=out_shape,
        compiler_params=_cparams(("parallel",)),
        name="merge",
    )(x2d, y_a, y_b, ga, gb, *consts)


def _router_body(logit_ref, upper_ref, e_ref, gate_ref, lp_ref, bc_ref, carry_out_ref, cnt_ref,
                 carry_ref, *, tm):
    i = pl.program_id(0)

    @pl.when(i == 0)
    def _():
        carry_ref[...] = jnp.zeros_like(carry_ref)

    lane = lax.broadcasted_iota(jnp.int32, (tm, LANES), 1)
    work = logit_ref[...]
    experts, vals = [], []
    onehot = jnp.zeros((tm, LANES), F32)
    for _ in range(TOP_K):
        mx = jnp.max(work, axis=-1, keepdims=True)
        idx = jnp.min(jnp.where(work == mx, lane, LANES), axis=-1, keepdims=True)
        hit = lane == idx
        experts.append(idx)
        vals.append(mx)
        onehot = onehot + hit.astype(F32)
        work = jnp.where(hit, -jnp.inf, work)
    ex = [jnp.exp(v - vals[0]) for v in vals]
    denom = ex[0] + ex[1] + ex[2] + ex[3]
    for r in range(TOP_K):
        e_ref[:, r:r + 1] = experts[r]
        gate_ref[:, r:r + 1] = ex[r] / denom

    rr = lax.broadcasted_iota(jnp.int32, (tm, tm), 0)
    cc = lax.broadcasted_iota(jnp.int32, (tm, tm), 1)
    lower = (cc < rr).astype(BF16)
    prefix = jnp.dot(lower, onehot.astype(BF16), preferred_element_type=F32)
    bc = jnp.sum(onehot, axis=0, keepdims=True)
    bc = jnp.floor((bc + (RUN_ALIGN - 1)) * (1.0 / RUN_ALIGN)) * RUN_ALIGN
    bc8 = jnp.broadcast_to(bc, (8, LANES))
    boff = jnp.dot(bc8.astype(BF16), upper_ref[...], preferred_element_type=F32)[0:1, :]
    local = prefix + boff
    for r in range(TOP_K):
        lp = jnp.sum(jnp.where(lane == experts[r], local, 0.0), axis=-1, keepdims=True)
        lp_ref[:, r:r + 1] = lp.astype(jnp.int32)
    bc_ref[0] = bc8
    carry_out_ref[0] = carry_ref[...]
    total = carry_ref[...] + bc8
    carry_ref[...] = total
    cnt_ref[...] = total


def _router(logits, tm):
    assert tm <= 256
    n = logits.shape[0]
    nblk = n // tm
    upper = (jnp.arange(LANES)[:, None] < jnp.arange(LANES)[None, :]).astype(BF16)
    row = lambda w_: pl.BlockSpec((tm, w_), lambda i: (i, 0))
    blk = pl.BlockSpec((1, 8, LANES), lambda i: (i, 0, 0))
    out_shape = (
        jax.ShapeDtypeStruct((n, TOP_K), jnp.int32),
        jax.ShapeDtypeStruct((n, TOP_K), F32),
        jax.ShapeDtypeStruct((n, TOP_K), jnp.int32),
        jax.ShapeDtypeStruct((nblk, 8, LANES), F32),
        jax.ShapeDtypeStruct((nblk, 8, LANES), F32),
        jax.ShapeDtypeStruct((8, LANES), F32),
    )
    return pl.pallas_call(
        functools.partial(_router_body, tm=tm),
        grid=(nblk,),
        in_specs=[row(LANES), pl.BlockSpec((LANES, LANES), lambda i: (0, 0))],
        out_specs=(row(TOP_K), row(TOP_K), row(TOP_K), blk, blk,
                   pl.BlockSpec((8, LANES), lambda i: (0, 0))),
        out_shape=out_shape,
        scratch_shapes=[pltpu.VMEM((8, LANES), F32)],
        compiler_params=_cparams(("arbitrary",)),
        name="router",
    )(logits, upper)


def _local_rows(tm):
    return TOP_K * tm + N_EXPERTS * RUN_ALIGN


def _for_each_run_chunk(tbl_ref, tm, fn):
    sizes = [s for s in (1 << k for k in range(tm.bit_length())) if RUN_ALIGN <= s <= tm]

    def per_expert(e, _):
        length = tbl_ref[0, 0, e]
        src = tbl_ref[0, 0, N_EXPERTS + e]
        dst = tbl_ref[0, 0, 2 * N_EXPERTS + e]
        for size in sizes:

            @pl.when((length & size) != 0)
            def _(size=size):
                off = length & (size - 1)
                fn(pl.multiple_of(src + off, RUN_ALIGN), pl.multiple_of(dst + off, RUN_ALIGN), size)
        return 0

    lax.fori_loop(0, N_EXPERTS, per_expert, 0)


def _dispatch_body(tbl_ref, lpt_ref, h_ref, xbuf_in_ref, xbuf_ref, sorted_ref, sem, *, tm):
    del xbuf_in_ref
    h = h_ref[...]
    lpt = lpt_ref[0]
    chunk = tm
    for c in range(_local_rows(tm) // chunk):
        r_idx = c * chunk + lax.broadcasted_iota(jnp.int32, (chunk, tm), 0)
        sel = jnp.zeros((chunk, tm), F32)
        for r in range(TOP_K):
            sel = sel + (r_idx == lpt[r:r + 1, :]).astype(F32)
        rows = jnp.dot(sel.astype(BF16), h, preferred_element_type=F32)
        sorted_ref[c * chunk:(c + 1) * chunk, :] = _pack_bf16_pairs(rows)

    def copy(local_row, global_row, size):
        return pltpu.make_async_copy(sorted_ref.at[pl.ds(local_row, size)],
                                     xbuf_ref.at[pl.ds(global_row, size)], sem)

    _for_each_run_chunk(tbl_ref, tm, lambda s, d, n: copy(s, d, n).start())
    _for_each_run_chunk(tbl_ref, tm, lambda s, d, n: copy(s, d, n).wait())


def _dispatch(h2, lpt, tbl, n_rows, tm):
    n = h2.shape[0]
    xbuf0 = jnp.zeros((n_rows, HALF_D), jnp.uint32)
    return pl.pallas_call(
        functools.partial(_dispatch_body, tm=tm),
        grid=(n // tm,),
        in_specs=[pl.BlockSpec((1, 1, LANES), lambda i: (i, 0, 0), memory_space=pltpu.SMEM),
                  pl.BlockSpec((1, TOP_K, tm), lambda i: (i, 0, 0)),
                  pl.BlockSpec((tm, D_MODEL), lambda i: (i, 0)),
                  pl.BlockSpec(memory_space=pl.ANY)],
        out_specs=pl.BlockSpec(memory_space=pl.ANY),
        out_shape=jax.ShapeDtypeStruct((n_rows, HALF_D), jnp.uint32),
        scratch_shapes=[pltpu.VMEM((_local_rows(tm), HALF_D), jnp.uint32),
                        pltpu.SemaphoreType.DMA],
        input_output_aliases={3: 0},
        compiler_params=_cparams(("arbitrary",)),
        name="moe_dispatch",
    )(tbl, lpt, h2, xbuf0)


def _expert_body(blk_e_ref, n_used_ref, x_ref, wg_ref, wl_ref, bg_ref, bl_ref, wd_ref, bd_ref,
                 y_ref, wg_s, wl_s, wd_s):
    i = pl.program_id(0)
    used = i < n_used_ref[0]

    changed = jnp.logical_or(i == 0, blk_e_ref[i] != blk_e_ref[jnp.maximum(i - 1, 0)])

    @pl.when(jnp.logical_and(used, changed))
    def _():
        wg_s[...] = wg_ref[0].astype(BF16)
        wl_s[...] = wl_ref[0].astype(BF16)
        wd_s[...] = wd_ref[0].astype(BF16)

    @pl.when(used)
    def _():
        xb = _unpack_bf16_pairs(x_ref[...]).astype(BF16)
        glu = jnp.dot(xb, wg_s[...], preferred_element_type=F32) + bg_ref[0]
        lin = jnp.dot(xb, wl_s[...], preferred_element_type=F32) + bl_ref[0]
        glu = jnp.minimum(glu, SWIGLU_LIMIT)
        lin = jnp.clip(lin, -SWIGLU_LIMIT, SWIGLU_LIMIT)
        act = glu * jax.nn.sigmoid(SWIGLU_ALPHA * glu) * (lin + 1.0)
        y = jnp.dot(act.astype(BF16), wd_s[...], preferred_element_type=F32) + bd_ref[0]
        y_ref[...] = _pack_bf16_pairs(y)

    @pl.when(jnp.logical_not(used))
    def _():
        y_ref[...] = jnp.zeros_like(y_ref)


def _experts(xbuf, blk_e, n_used, w_gate_up, b_gate_up, w_down, b_down, tb):
    n_rows = xbuf.shape[0]
    d = D_MODEL
    bgu = b_gate_up.reshape(N_EXPERTS, 1, 2 * D_FF)
    bd = b_down.reshape(N_EXPERTS, 1, d)
    grid_spec = pltpu.PrefetchScalarGridSpec(
        num_scalar_prefetch=2,
        grid=(n_rows // tb,),
        in_specs=[
            pl.BlockSpec((tb, HALF_D), lambda i, be, nu: (i, 0)),
            pl.BlockSpec((1, d, D_FF), lambda i, be, nu: (be[i], 0, 0)),
            pl.BlockSpec((1, d, D_FF), lambda i, be, nu: (be[i], 0, 1)),
            pl.BlockSpec((1, 1, D_FF), lambda i, be, nu: (be[i], 0, 0)),
            pl.BlockSpec((1, 1, D_FF), lambda i, be, nu: (be[i], 0, 1)),
            pl.BlockSpec((1, D_FF, d), lambda i, be, nu: (be[i], 0, 0)),
            pl.BlockSpec((1, 1, d), lambda i, be, nu: (be[i], 0, 0)),
        ],
        out_specs=pl.BlockSpec((tb, HALF_D), lambda i, be, nu: (i, 0)),
        scratch_shapes=[pltpu.VMEM((d, D_FF), BF16), pltpu.VMEM((d, D_FF), BF16),
                        pltpu.VMEM((D_FF, d), BF16)],
    )
    return pl.pallas_call(
        _expert_body,
        grid_spec=grid_spec,
        out_shape=jax.ShapeDtypeStruct((n_rows, HALF_D), jnp.uint32),
        compiler_params=_cparams(("arbitrary",)),
        name="moe_experts",
    )(blk_e, n_used, xbuf, w_gate_up, w_gate_up, bgu, bgu, w_down, bd)


def _combine_body(tbl_ref, x2_ref, gate_ref, lp_ref, g_ref, ybuf_ref, o_ref, ys_ref, sem, *, tm):
    n_local = _local_rows(tm)
    ys_ref[TOP_K * tm:, :] = jnp.zeros((n_local - TOP_K * tm, HALF_D), jnp.uint32)

    def copy(local_row, global_row, size):
        return pltpu.make_async_copy(ybuf_ref.at[pl.ds(global_row, size)],
                                     ys_ref.at[pl.ds(local_row, size)], sem)

    _for_each_run_chunk(tbl_ref, tm, lambda s, d, n: copy(s, d, n).start())
    _for_each_run_chunk(tbl_ref, tm, lambda s, d, n: copy(s, d, n).wait())

    gate = gate_ref[...]
    lp = lp_ref[...]
    col = lax.broadcasted_iota(jnp.int32, (tm, n_local), 1)
    gmat = jnp.zeros((tm, n_local), F32)
    for r in range(TOP_K):
        gmat = gmat + jnp.where(col == lp[:, r:r + 1], gate[:, r:r + 1], 0.0)
    g_hi, g_lo = _split_bf16(gmat)
    ys = _unpack_bf16_pairs(ys_ref[...]).astype(BF16)
    out = (x2_ref[...] + jnp.dot(g_hi, ys, preferred_element_type=F32)
           + jnp.dot(g_lo, ys, preferred_element_type=F32))
    o_ref[...] = out * lax.rsqrt(jnp.mean(out * out, axis=-1, keepdims=True) + EPS) * g_ref[...]


def _combine(x2, gate, lp, tbl, ybuf, final_norm_g, tm):
    n, d = x2.shape
    return pl.pallas_call(
        functools.partial(_combine_body, tm=tm),
        grid=(n // tm,),
        in_specs=[pl.BlockSpec((1, 1, LANES), lambda i: (i, 0, 0), memory_space=pltpu.SMEM),
                  pl.BlockSpec((tm, d), lambda i: (i, 0)),
                  pl.BlockSpec((tm, TOP_K), lambda i: (i, 0)),
                  pl.BlockSpec((tm, TOP_K), lambda i: (i, 0)),
                  pl.BlockSpec((1, d), lambda i: (0, 0)),
                  pl.BlockSpec(memory_space=pl.ANY)],
        out_specs=pl.BlockSpec((tm, d), lambda i: (i, 0)),
        out_shape=jax.ShapeDtypeStruct((n, d), F32),
        scratch_shapes=[pltpu.VMEM((_local_rows(tm), HALF_D), jnp.uint32),
                        pltpu.SemaphoreType.DMA],
        compiler_params=_cparams(("arbitrary",)),
        name="moe_combine",
    )(tbl, x2, gate, lp, final_norm_g.reshape(1, d), ybuf)


def _moe(x2, h2, logits, w_gate_up, b_gate_up, w_down, b_down, final_norm_g, tb, tm):
    n = x2.shape[0]
    nblk = n // tm
    max_rows = n * TOP_K + nblk * N_EXPERTS * (RUN_ALIGN - 1)
    n_blk = -(-max_rows // tb) + N_EXPERTS
    top_e, gate, lp, bcount, before, counts = _router(logits, tm)
    del top_e
    cnt = counts[0, :N_EXPERTS].astype(jnp.int32)
    padded = (cnt + tb - 1) // tb * tb
    pad_end = jnp.cumsum(padded)
    pad_start = pad_end - padded
    blk_first = jnp.arange(n_blk, dtype=jnp.int32) * tb
    blk_e = jnp.minimum(jnp.sum((pad_end[None, :] <= blk_first[:, None]).astype(jnp.int32), axis=1),
                        N_EXPERTS - 1)
    n_used = (pad_end[-1:] // tb).astype(jnp.int32)
    run_len = bcount[:, 0, :N_EXPERTS].astype(jnp.int32)
    run_src = jnp.cumsum(run_len, axis=1) - run_len
    run_dst = pad_start[None, :] + before[:, 0, :N_EXPERTS].astype(jnp.int32)
    tbl = jnp.concatenate([run_len, run_src, run_dst,
                           jnp.zeros((nblk, LANES - 3 * N_EXPERTS), jnp.int32)], axis=1)
    tbl = tbl.reshape(nblk, 1, LANES)
    lpt = lp.reshape(nblk, tm, TOP_K).transpose(0, 2, 1)
    xbuf = _dispatch(h2, lpt, tbl, n_blk * tb, tm)
    ybuf = _experts(xbuf, blk_e, n_used, w_gate_up, b_gate_up, w_down, b_down, tb)
    return _combine(x2, gate, lp, tbl, ybuf, final_norm_g, tm)


def kernel(x, positions, attn_norm_g, w_in, mla_q_norm_g, mla_w_uq, mla_kv_norm_g, mla_w_ukv,
           w_branch_mla, w_branch_dsa, w_out, ffn_norm_g, router_w, router_b,
           w_gate_up, b_gate_up, w_down, b_down, final_norm_g):
    b, s, d = x.shape
    n = b * s
    (qm, km, vm, qd, qi, kd, ki, vd, wi, ga, gb) = _in_proj(
        x.reshape(n, d), positions.reshape(n), attn_norm_g[0], w_in[0],
        mla_q_norm_g[0], mla_w_uq[0], mla_kv_norm_g[0], mla_w_ukv[0], tm=512)
    y_a = _mla_attention(qm, km, vm, b, s, tq=1024, tk=1024)
    y_b = _dsa_attention(qi, qd, ki, kd, vd, wi, b, s, tq=256, tk=1024, tka=1024)
    x2, h2, logits = _merge(
        x.reshape(n, d), y_a, y_b, ga, gb, w_branch_mla[0], w_branch_dsa[0], w_out[0],
        ffn_norm_g[0], router_w[0], router_b[0], tm=512)
    out = _moe(x2, h2, logits, w_gate_up[0], b_gate_up[0], w_down[0], b_down[0], final_norm_g,
               tb=512, tm=256)
    return out.reshape(b, s, d)
```

```python
import functools
import math

import jax
import jax.numpy as jnp
from jax import lax
from jax.experimental import pallas as pl
from jax.experimental.pallas import tpu as pltpu

F32 = jnp.float32
BF16 = jnp.bfloat16

LANES = 128
LOG2_E = 1.4426950408889634

D_MODEL = 1024
EPS = 1e-6
ROPE_THETA = 500000.0
MLA_HEADS = 8
MLA_Q_LORA = 256
MLA_KV_LORA = 128
MLA_NOPE = 64
MLA_ROPE = 32
MLA_V = 64
DSA_HEADS = 8
DSA_HEAD_DIM = 64
DSA_ROT = 16
IDX_HEADS = 8
IDX_DIM = 64
TOPK_MAX = 256
N_EXPERTS = 32
TOP_K = 4
D_FF = 1024
SWIGLU_LIMIT = 7.0
SWIGLU_ALPHA = 1.702

VMEM_LIMIT = 56 * 1024 * 1024

_OFF_CQ = 0
_OFF_CKV = _OFF_CQ + MLA_Q_LORA
_OFF_KPE = _OFF_CKV + MLA_KV_LORA
_OFF_QB = _OFF_KPE + MLA_ROPE
_OFF_KB = _OFF_QB + DSA_HEADS * DSA_HEAD_DIM
_OFF_VB = _OFF_KB + DSA_HEAD_DIM
_OFF_QI = _OFF_VB + DSA_HEAD_DIM
_OFF_KI = _OFF_QI + IDX_HEADS * IDX_DIM
_OFF_WI = _OFF_KI + IDX_DIM
_OFF_GA = _OFF_WI + IDX_HEADS
_OFF_GB = _OFF_GA + D_MODEL
_D_IN = _OFF_GB + D_MODEL


def _cparams(sem):
    return pltpu.CompilerParams(dimension_semantics=sem, vmem_limit_bytes=VMEM_LIMIT)


def _rope_group(xg, c, slo, shi, shift):
    return (xg * c + pltpu.roll(xg, LANES - shift, 1) * slo
            + pltpu.roll(xg, shift, 1) * shi)


def _in_proj_body(x_ref, g_ref, w1_ref, w2_ref, w3_ref, w5_ref, qng_ref, wuq_ref,
                  kvng_ref, wuk_ref, wuv_ref, em_ref, bm_ref, ed_ref, bd_ref, csm_ref, csd_ref,
                  qm_ref, km_ref, vm_ref, qd_ref, qi_ref, kd_ref, ki_ref, vd_ref,
                  wi_ref, ga_ref, gb_ref):
    x = x_ref[...]
    h = x * lax.rsqrt(jnp.mean(x * x, axis=-1, keepdims=True) + EPS) * g_ref[...]
    hb = h.astype(BF16)

    def spread(cs_ref, e_ref, b_ref):
        hi, lo = _split_bf16(cs_ref[...])
        t = (jnp.dot(hi, e_ref[...], preferred_element_type=F32)
             + jnp.dot(lo, e_ref[...], preferred_element_type=F32))
        return t[:, :LANES] + b_ref[...], t[:, LANES:2 * LANES], t[:, 2 * LANES:]

    cm, slm, shm = spread(csm_ref, em_ref, bm_ref)
    cd, sld, shd = spread(csd_ref, ed_ref, bd_ref)

    z1 = jnp.dot(hb, w1_ref[...], preferred_element_type=F32)
    cq = z1[:, 0:MLA_Q_LORA]
    cqn = cq * lax.rsqrt(jnp.mean(cq * cq, axis=-1, keepdims=True) + EPS) * qng_ref[...]
    q = jnp.dot(cqn.astype(BF16), wuq_ref[...], preferred_element_type=F32)
    q_scale = (MLA_NOPE + MLA_ROPE) ** -0.5 * LOG2_E
    for j in range(MLA_HEADS):
        qg = _rope_group(q[:, j * LANES:(j + 1) * LANES], cm, slm, shm, MLA_ROPE // 2)
        qm_ref[:, j * LANES:(j + 1) * LANES] = (qg * q_scale).astype(BF16)

    ckv = z1[:, MLA_Q_LORA:MLA_Q_LORA + MLA_KV_LORA]
    ckvn = (ckv * lax.rsqrt(jnp.mean(ckv * ckv, axis=-1, keepdims=True) + EPS)
            * kvng_ref[...]).astype(BF16)
    kpe = _rope_group(z1[:, MLA_Q_LORA + MLA_KV_LORA:], cm, slm, shm, MLA_ROPE // 2)
    kn = jnp.dot(ckvn, wuk_ref[...], preferred_element_type=F32)
    for j in range(MLA_HEADS):
        km_ref[:, j * LANES:(j + 1) * LANES] = (kn[:, j * LANES:(j + 1) * LANES] + kpe).astype(BF16)
    vm_ref[...] = jnp.dot(ckvn, wuv_ref[...], preferred_element_type=F32).astype(BF16)

    z2 = jnp.dot(hb, w2_ref[...], preferred_element_type=F32)
    d_scale = DSA_HEAD_DIM ** -0.5 * LOG2_E
    for j in range(DSA_HEADS // 2):
        g = _rope_group(z2[:, j * LANES:(j + 1) * LANES], cd, sld, shd, DSA_ROT // 2) * d_scale
        qd_ref[2 * j] = g[:, :DSA_HEAD_DIM].astype(BF16)
        qd_ref[2 * j + 1] = g[:, DSA_HEAD_DIM:].astype(BF16)
    base = DSA_HEADS * DSA_HEAD_DIM
    for j in range(IDX_HEADS // 2):
        g = _rope_group(z2[:, base + j * LANES:base + (j + 1) * LANES], cd, sld, shd, DSA_ROT // 2)
        qi_ref[2 * j] = g[:, :IDX_DIM].astype(BF16)
        qi_ref[2 * j + 1] = g[:, IDX_DIM:].astype(BF16)

    z3 = jnp.dot(hb, w3_ref[...], preferred_element_type=F32)
    kb = _rope_group(z3[:, :LANES], cd, sld, shd, DSA_ROT // 2)
    kd_ref[...] = kb[:, :DSA_HEAD_DIM].astype(BF16)
    ki_ref[...] = kb[:, DSA_HEAD_DIM:].astype(BF16)
    zv = z3[:, LANES:]
    vlane = lax.broadcasted_iota(jnp.int32, zv.shape, 1)
    vd_ref[...] = jnp.where(vlane < DSA_HEAD_DIM, zv,
                            jnp.where(vlane == DSA_HEAD_DIM, 1.0, 0.0)).astype(BF16)
    w_scale = IDX_HEADS ** -0.5 * IDX_DIM ** -0.5
    wi_ref[...] = z3[:, LANES + DSA_HEAD_DIM:LANES + DSA_HEAD_DIM + IDX_HEADS] * w_scale

    z5 = jnp.dot(hb, w5_ref[...], preferred_element_type=F32)
    ga_ref[...] = jax.nn.sigmoid(z5[:, :D_MODEL]).astype(BF16)
    gb_ref[...] = jax.nn.sigmoid(z5[:, D_MODEL:]).astype(BF16)


ROPE_COLS = 16


def _rope_tables(pos, rot_dim, lane_of_x1, period):
    half = rot_dim // 2
    inv_freq = ROPE_THETA ** (-jnp.arange(half, dtype=F32) / half)
    ang = pos.astype(F32)[:, None] * inv_freq
    pad = ((0, 0), (0, ROPE_COLS - half))
    cs = jnp.concatenate([jnp.pad(jnp.cos(ang), pad), jnp.pad(jnp.sin(ang), pad)], axis=1)

    lane = jnp.arange(LANES)
    in_period = lane % period - lane_of_x1
    freq = jnp.arange(ROPE_COLS)[:, None]
    on_x1 = (in_period[None, :] == freq) & (freq < half)
    on_x2 = (in_period[None, :] - half == freq) & (freq < half)
    zero = jnp.zeros((ROPE_COLS, LANES), F32)
    f = lambda m: m.astype(F32)
    spread = jnp.concatenate([
        jnp.concatenate([f(on_x1 | on_x2), zero, zero], axis=1),
        jnp.concatenate([zero, -f(on_x1), f(on_x2)], axis=1),
    ], axis=0).astype(BF16)
    rotated = (in_period >= 0) & (in_period < rot_dim)
    bias = jnp.where(rotated, 0.0, 1.0).astype(F32).reshape(1, LANES)
    return cs, spread, bias


def _head_cols(w, n_heads, widths, total):
    k = w.shape[0]
    per = sum(widths)
    w = w.reshape(k, n_heads, per)
    return jnp.pad(w, ((0, 0), (0, 0), (0, total - per))).reshape(k, n_heads * total)


def _in_proj(x2, pos, attn_norm_g, w_in, q_norm_g, w_uq, kv_norm_g, w_ukv, tm):
    n = x2.shape[0]
    d = D_MODEL
    zc = lambda k: jnp.zeros((d, k), F32)
    w1 = jnp.concatenate([w_in[:, _OFF_CQ:_OFF_KPE], zc(MLA_NOPE), w_in[:, _OFF_KPE:_OFF_QB],
                          zc(LANES - MLA_NOPE - MLA_ROPE)], axis=1).astype(BF16)
    w2 = jnp.concatenate([w_in[:, _OFF_QB:_OFF_KB], w_in[:, _OFF_QI:_OFF_KI]], axis=1).astype(BF16)
    w3 = jnp.concatenate([w_in[:, _OFF_KB:_OFF_VB], w_in[:, _OFF_KI:_OFF_WI],
                          w_in[:, _OFF_VB:_OFF_QI], w_in[:, _OFF_WI:_OFF_GA],
                          zc(LANES - DSA_HEAD_DIM - IDX_HEADS)], axis=1).astype(BF16)
    w5 = w_in[:, _OFF_GA:].astype(BF16)
    wuq = _head_cols(w_uq, MLA_HEADS, (MLA_NOPE, MLA_ROPE), LANES).astype(BF16)
    w_ukv3 = w_ukv.reshape(MLA_KV_LORA, MLA_HEADS, MLA_NOPE + MLA_V)
    wuk = jnp.pad(w_ukv3[:, :, :MLA_NOPE], ((0, 0), (0, 0), (0, LANES - MLA_NOPE))
                  ).reshape(MLA_KV_LORA, MLA_HEADS * LANES).astype(BF16)
    wuv = w_ukv3[:, :, MLA_NOPE:].reshape(MLA_KV_LORA, MLA_HEADS * MLA_V).astype(BF16)
    csm, em, bm = _rope_tables(pos, MLA_ROPE, MLA_NOPE, LANES)
    csd, ed, bdd = _rope_tables(pos, DSA_ROT, 0, DSA_HEAD_DIM)

    row = lambda w_: pl.BlockSpec((tm, w_), lambda i: (i, 0))
    full = lambda a: pl.BlockSpec(a.shape, lambda i: (0,) * a.ndim)
    hm = pl.BlockSpec((DSA_HEADS, tm, DSA_HEAD_DIM), lambda i: (0, i, 0))
    g2 = attn_norm_g.reshape(1, d)
    qng = q_norm_g.reshape(1, -1)
    kvng = kv_norm_g.reshape(1, -1)
    consts = (g2, w1, w2, w3, w5, qng, wuq, kvng, wuk, wuv, em, bm, ed, bdd)
    out_shape = (
        jax.ShapeDtypeStruct((n, MLA_HEADS * LANES), BF16),
        jax.ShapeDtypeStruct((n, MLA_HEADS * LANES), BF16),
        jax.ShapeDtypeStruct((n, MLA_HEADS * MLA_V), BF16),
        jax.ShapeDtypeStruct((DSA_HEADS, n, DSA_HEAD_DIM), BF16),
        jax.ShapeDtypeStruct((IDX_HEADS, n, IDX_DIM), BF16),
        jax.ShapeDtypeStruct((n, DSA_HEAD_DIM), BF16),
        jax.ShapeDtypeStruct((n, IDX_DIM), BF16),
        jax.ShapeDtypeStruct((n, LANES), BF16),
        jax.ShapeDtypeStruct((n, IDX_HEADS), F32),
        jax.ShapeDtypeStruct((n, D_MODEL), BF16),
        jax.ShapeDtypeStruct((n, D_MODEL), BF16),
    )
    out_specs = (row(MLA_HEADS * LANES), row(MLA_HEADS * LANES), row(MLA_HEADS * MLA_V),
                 hm, hm, row(DSA_HEAD_DIM), row(IDX_DIM), row(LANES), row(IDX_HEADS),
                 row(D_MODEL), row(D_MODEL))
    return pl.pallas_call(
        _in_proj_body,
        grid=(n // tm,),
        in_specs=[row(d)] + [full(a) for a in consts] + [row(2 * ROPE_COLS)] * 2,
        out_specs=out_specs,
        out_shape=out_shape,
        compiler_params=_cparams(("parallel",)),
        name="in_proj",
    )(x2, *consts, csm, csd)


NEG_BIG = -1e30
_NT = (((1,), (1,)), ((), ()))


def _mla_body(q_ref, k_ref, v_ref, o_ref, *, tq, tk):
    qi = pl.program_id(2)
    n_sub = tq // tk
    qs = [q_ref[:, hh * LANES:(hh + 1) * LANES] for hh in range(2)]

    def step(j, carry, masked):
        start = pl.multiple_of(j * tk, tk)
        vs = v_ref[pl.ds(start, tk), :]
        new = []
        for hh in range(2):
            m, l, acc = carry[hh]
            ks = k_ref[pl.ds(start, tk), hh * LANES:(hh + 1) * LANES]
            s = lax.dot_general(qs[hh], ks, _NT, preferred_element_type=F32)
            if masked:
                row = qi * tq + lax.broadcasted_iota(jnp.int32, (tq, tk), 0)
                col = j * tk + lax.broadcasted_iota(jnp.int32, (tq, tk), 1)
                s = jnp.where(col <= row, s, NEG_BIG)
            m_new = jnp.maximum(m, jnp.max(s, axis=-1, keepdims=True))
            alpha = jnp.exp2(m - m_new)
            p = jnp.exp2(s - m_new)
            l = alpha * l + jnp.sum(p, axis=-1, keepdims=True)
            acc = alpha * acc + jnp.dot(p.astype(BF16), vs, preferred_element_type=F32)
            new.append((m_new, l, acc))
        return tuple(new)

    one = (jnp.full((tq, 1), NEG_BIG, F32), jnp.zeros((tq, 1), F32), jnp.zeros((tq, LANES), F32))
    carry = lax.fori_loop(0, qi * n_sub, functools.partial(step, masked=False), (one, one))
    for dd in range(n_sub):
        carry = step(qi * n_sub + dd, carry, True)
    outs = [acc / l for (_, l, acc) in carry]
    lane = lax.broadcasted_iota(jnp.int32, (tq, LANES), 1)
    o_ref[...] = jnp.where(lane < MLA_V, outs[0], outs[1]).astype(BF16)


def _mla_attention(qm, km, vm, b, s, tq, tk):
    n = b * s
    nq = s // tq
    return pl.pallas_call(
        functools.partial(_mla_body, tq=tq, tk=tk),
        grid=(b, MLA_HEADS // 2, nq),
        in_specs=[
            pl.BlockSpec((tq, 2 * LANES), lambda bi, hp, qi: (bi * nq + qi, hp)),
            pl.BlockSpec((s, 2 * LANES), lambda bi, hp, qi: (bi, hp)),
            pl.BlockSpec((s, 2 * MLA_V), lambda bi, hp, qi: (bi, hp)),
        ],
        out_specs=pl.BlockSpec((tq, 2 * MLA_V), lambda bi, hp, qi: (bi * nq + qi, hp)),
        out_shape=jax.ShapeDtypeStruct((n, MLA_HEADS * MLA_V), BF16),
        compiler_params=_cparams(("parallel", "parallel", "arbitrary")),
        name="mla_attention",
    )(qm, km, vm)


INT_MIN = -2 ** 31
ACC_ROWS = 64
SEARCH_GROUPS = 256
SNAP_AFTER_TRIPS = 5
KEY_NEG_INF = (0xFF800000 - 2 ** 32) ^ 0x7FFFFFFF


def _sortable_key(score):
    bits = pltpu.bitcast(score, jnp.int32)
    return bits ^ ((bits >> 31) & 0x7FFFFFFF)


def _sortable_key_inverse(key):
    return pltpu.bitcast(key ^ ((key >> 31) & 0x7FFFFFFF), F32)


def _dsa_body(qi_ref, qd_ref, ki_ref, kd_ref, vd_ref, w_ref, o_ref, keys_ref, keyst_ref, *,
              tq, tk, tka, topk, s_len):
    qb = pl.program_id(1)
    n_tiles = (qb * tq + tq + tk - 1) // tk
    q_pos = qb * tq + lax.broadcasted_iota(jnp.int32, (tq, 1), 0)
    q_pos_t = qb * tq + lax.broadcasted_iota(jnp.int32, (1, tq), 1)

    qidx = qi_ref[...].reshape(IDX_HEADS * tq, IDX_DIM)
    w = w_ref[...]
    wcols = [jnp.broadcast_to(w[:, h:h + 1], (tq, tk)) for h in range(IDX_HEADS)]

    def score_tile(j, _):
        start = pl.multiple_of(j * tk, tk)
        kt = ki_ref[pl.ds(start, tk), :]
        sh = lax.dot_general(qidx, kt, _NT, preferred_element_type=F32)
        sc = jnp.zeros((tq, tk), F32)
        for h in range(IDX_HEADS):
            sc = sc + jnp.maximum(sh[h * tq:(h + 1) * tq], 0.0) * wcols[h]
        col = j * tk + lax.broadcasted_iota(jnp.int32, (tq, tk), 1)
        sc = jnp.where(col <= q_pos, sc, -jnp.inf)
        keys = _sortable_key(sc)
        keys_ref[:, pl.ds(start, tk)] = keys
        keyst_ref[pl.ds(start, tk), :] = keys.T
        return 0

    lax.fori_loop(0, n_tiles, score_tile, 0)

    def tile_t(j):
        return keyst_ref[pl.ds(pl.multiple_of(j * tk, tk), tk), :]

    def count_where(pred):
        def body(j, cnt):
            hit = pred(tile_t(j), j).astype(jnp.int32)
            return cnt + jnp.sum(hit.reshape(tk // ACC_ROWS, ACC_ROWS, tq), axis=0)
        cnt = lax.fori_loop(0, n_tiles, body, jnp.zeros((ACC_ROWS, tq), jnp.int32))
        return jnp.sum(cnt, axis=0, keepdims=True)

    def count_ge(cand):
        return count_where(lambda kt, j: kt >= cand)

    def max_le(bound):
        def body(j, mx):
            kt = tile_t(j)
            kt = jnp.where(kt <= bound, kt, INT_MIN)
            return jnp.maximum(mx, jnp.max(kt.reshape(tk // ACC_ROWS, ACC_ROWS, tq), axis=0))
        mx = lax.fori_loop(0, n_tiles, body, jnp.full((ACC_ROWS, tq), INT_MIN, jnp.int32))
        return jnp.max(mx, axis=0, keepdims=True)

    def group_max(j, g):
        return jnp.maximum(g, jnp.max(tile_t(j).reshape(tk // SEARCH_GROUPS, SEARCH_GROUPS, tq),
                                      axis=0))

    g = lax.fori_loop(0, n_tiles, group_max, jnp.full((SEARCH_GROUPS, tq), INT_MIN, jnp.int32))
    hi = jnp.max(g, axis=0, keepdims=True)
    lo = jnp.min(g, axis=0, keepdims=True)
    need = q_pos_t + 1 > topk
    lo = jnp.where(need, lo, KEY_NEG_INF + 1)
    hi = jnp.where(need, hi, KEY_NEG_INF + 1)

    def n_active(lo, hi, cnt_lo):
        return jnp.max(((lo < hi) & (cnt_lo != topk)).astype(jnp.int32))

    def halve_step(lo, hi, cnt_lo):
        mid = _sortable_key(0.5 * _sortable_key_inverse(lo) + 0.5 * _sortable_key_inverse(hi))
        mid = jnp.minimum(jnp.maximum(mid, lo + 1), hi)
        cnt = count_ge(mid)
        ok = cnt >= topk
        return jnp.where(ok, mid, lo), jnp.where(ok, hi, mid - 1), jnp.where(ok, cnt, cnt_lo)

    def snap_step(lo, hi, cnt_lo):
        mid = jnp.minimum(jnp.maximum(max_le(hi), lo + 1), hi)
        cnt = count_ge(mid)
        ok = cnt >= topk
        return jnp.where(ok, mid, lo), jnp.where(ok, mid, mid - 1), jnp.where(ok, cnt, cnt_lo)

    def narrow(carry):
        lo, hi, cnt_lo, trip, _ = carry
        lo, hi, cnt_lo = halve_step(lo, hi, cnt_lo)
        lo, hi, cnt_lo = lax.cond(trip >= SNAP_AFTER_TRIPS, snap_step, halve_step, lo, hi, cnt_lo)
        return lo, hi, cnt_lo, trip + 1, n_active(lo, hi, cnt_lo)

    cnt_lo = count_ge(lo)
    thr, _, cnt_thr, _, _ = lax.while_loop(
        lambda c: c[4] > 0, narrow, (lo, hi, cnt_lo, jnp.int32(0), n_active(lo, hi, cnt_lo)))
    thr = jnp.maximum(thr, KEY_NEG_INF + 1)

    tied = jnp.logical_and(need, cnt_thr > topk)
    keep_all = jnp.full((1, tq), s_len, jnp.int32)

    def tie_cut():
        n_keep = topk - count_ge(thr + 1)

        def step(_, carry):
            jlo, jhi = carry
            jm = (jlo + jhi) >> 1

            def pred(kt, j):
                kpos = j * tk + lax.broadcasted_iota(jnp.int32, (tk, tq), 0)
                return jnp.logical_and(kt == thr, kpos <= jm)

            ok = count_where(pred) >= n_keep
            return jnp.where(ok, jlo, jm + 1), jnp.where(ok, jm, jhi)

        _, jhi = lax.fori_loop(0, (s_len - 1).bit_length(), step,
                               (jnp.zeros((1, tq), jnp.int32), keep_all - 1))
        return jnp.where(tied, jhi, keep_all)

    jcut = lax.cond(jnp.max(tied.astype(jnp.int32)) > 0, tie_cut, lambda: keep_all)

    def to_column(v):
        return jnp.transpose(jnp.broadcast_to(v, (LANES, tq)))[:, 0:1]

    thr = to_column(thr)
    jcut = to_column(jcut)

    qd = qd_ref[...].reshape(DSA_HEADS * tq, DSA_HEAD_DIM)

    def attn_tile(j, carry):
        m, acc = carry
        start = pl.multiple_of(j * tka, tka)
        kt = kd_ref[pl.ds(start, tka), :]
        vt = vd_ref[pl.ds(start, tka), :]
        keys = keys_ref[:, pl.ds(start, tka)]
        col = j * tka + lax.broadcasted_iota(jnp.int32, (tq, tka), 1)
        sel = jnp.logical_or(keys > thr, jnp.logical_and(keys == thr, col <= jcut))
        s = lax.dot_general(qd, kt, _NT, preferred_element_type=F32)
        s = jnp.where(sel[None], s.reshape(DSA_HEADS, tq, tka), NEG_BIG)
        m_new = jnp.maximum(m, jnp.max(s, axis=-1, keepdims=True))
        p = jnp.exp2(s - m_new).astype(BF16)
        pv = jnp.dot(p.reshape(DSA_HEADS * tq, tka), vt, preferred_element_type=F32)
        acc = jnp.exp2(m - m_new) * acc + pv.reshape(DSA_HEADS, tq, LANES)
        return m_new, acc

    init = (jnp.full((DSA_HEADS, tq, 1), NEG_BIG, F32), jnp.zeros((DSA_HEADS, tq, LANES), F32))
    m, acc = lax.fori_loop(0, n_tiles * (tk // tka), attn_tile, init)
    for h in range(DSA_HEADS):
        out = acc[h, :, :DSA_HEAD_DIM] / acc[h, :, DSA_HEAD_DIM:DSA_HEAD_DIM + 1]
        o_ref[:, h * DSA_HEAD_DIM:(h + 1) * DSA_HEAD_DIM] = out.astype(BF16)


def _dsa_attention(qi, qd, ki, kd, vd, wi, b, s, tq, tk, tka):
    assert tk % tka == 0 and tk % SEARCH_GROUPS == 0 and SEARCH_GROUPS >= TOPK_MAX
    n = b * s
    nq = s // tq
    topk = min(TOPK_MAX, s // 4)
    hm = pl.BlockSpec((DSA_HEADS, tq, DSA_HEAD_DIM), lambda bi, qb: (0, bi * nq + qb, 0))
    kv = pl.BlockSpec((s, DSA_HEAD_DIM), lambda bi, qb: (bi, 0))
    return pl.pallas_call(
        functools.partial(_dsa_body, tq=tq, tk=tk, tka=tka, topk=topk, s_len=s),
        grid=(b, nq),
        in_specs=[hm, hm, kv, kv, pl.BlockSpec((s, LANES), lambda bi, qb: (bi, 0)),
                  pl.BlockSpec((tq, IDX_HEADS), lambda bi, qb: (bi * nq + qb, 0))],
        out_specs=pl.BlockSpec((tq, DSA_HEADS * DSA_HEAD_DIM), lambda bi, qb: (bi * nq + qb, 0)),
        out_shape=jax.ShapeDtypeStruct((n, DSA_HEADS * DSA_HEAD_DIM), BF16),
        scratch_shapes=[pltpu.VMEM((tq, s), jnp.int32), pltpu.VMEM((s, tq), jnp.int32)],
        compiler_params=_cparams(("parallel", "arbitrary")),
        name="dsa_attention",
    )(qi, qd, ki, kd, vd, wi)


HALF_D = D_MODEL // 2
RUN_ALIGN = 8


def _pack_bf16_pairs(y):
    r = pltpu.bitcast(y.astype(BF16).astype(F32), jnp.uint32)
    return r[:, :HALF_D] | (r[:, HALF_D:] >> 16)


def _unpack_bf16_pairs(p):
    hi = pltpu.bitcast(p & jnp.uint32(0xFFFF0000), F32)
    lo = pltpu.bitcast(p << 16, F32)
    return jnp.concatenate([hi, lo], axis=1)


def _split_bf16(a):
    hi = a.astype(BF16)
    lo = (a - hi.astype(F32)).astype(BF16)
    return hi, lo


def _merge_body(x_ref, ya_ref, yb_ref, ga_ref, gb_ref, wa_ref, wb_ref, wo_ref, g_ref,
                rwh_ref, rwl_ref, rb_ref, x2_ref, h_ref, logit_ref):
    ma = jnp.dot(ya_ref[...], wa_ref[...], preferred_element_type=F32)
    mb = jnp.dot(yb_ref[...], wb_ref[...], preferred_element_type=F32)
    merged = ga_ref[...].astype(F32) * ma + gb_ref[...].astype(F32) * mb
    x2 = x_ref[...] + jnp.dot(merged.astype(BF16), wo_ref[...], preferred_element_type=F32)
    x2_ref[...] = x2
    h = x2 * lax.rsqrt(jnp.mean(x2 * x2, axis=-1, keepdims=True) + EPS) * g_ref[...]
    hh, hl = _split_bf16(h)
    h_ref[...] = hh
    logit_ref[...] = (jnp.dot(hh, rwh_ref[...], preferred_element_type=F32)
                      + jnp.dot(hh, rwl_ref[...], preferred_element_type=F32)
                      + jnp.dot(hl, rwh_ref[...], preferred_element_type=F32)) + rb_ref[...]


def _merge(x2d, y_a, y_b, ga, gb, w_ba, w_bb, w_out, ffn_norm_g, router_w, router_b, tm):
    n, d = x2d.shape
    rw = jnp.pad(router_w, ((0, 0), (0, LANES - N_EXPERTS)))
    rwh = rw.astype(BF16)
    rwl = (rw - rwh.astype(F32)).astype(BF16)
    rb = jnp.pad(router_b, (0, LANES - N_EXPERTS), constant_values=NEG_BIG).reshape(1, LANES)
    consts = (w_ba.astype(BF16), w_bb.astype(BF16), w_out.astype(BF16), ffn_norm_g.reshape(1, d),
              rwh, rwl, rb)
    row = lambda w_: pl.BlockSpec((tm, w_), lambda i: (i, 0))
    full = lambda a: pl.BlockSpec(a.shape, lambda i: (0,) * a.ndim)
    out_shape = (
        jax.ShapeDtypeStruct((n, d), F32),
        jax.ShapeDtypeStruct((n, d), BF16),
        jax.ShapeDtypeStruct((n, LANES), F32),
    )
    return pl.pallas_call(
        _merge_body,
        grid=(n // tm,),
        in_specs=[row(d), row(HALF_D), row(HALF_D), row(d), row(d)] + [full(a) for a in consts],
        out_specs=(row(d), row(d), row(LANES)),
        out_shape=out_shape,
        compiler_params=_cparams(("parallel",)),
        name="merge",
    )(x2d, y_a, y_b, ga, gb, *consts)


def _router_body(logit_ref, upper_ref, e_ref, gate_ref, lp_ref, bc_ref, carry_out_ref, cnt_ref,
                 carry_ref, *, tm):
    i = pl.program_id(0)

    @pl.when(i == 0)
    def _():
        carry_ref[...] = jnp.zeros_like(carry_ref)

    lane = lax.broadcasted_iota(jnp.int32, (tm, LANES), 1)
    work = logit_ref[...]
    experts, vals = [], []
    onehot = jnp.zeros((tm, LANES), F32)
    for _ in range(TOP_K):
        mx = jnp.max(work, axis=-1, keepdims=True)
        idx = jnp.min(jnp.where(work == mx, lane, LANES), axis=-1, keepdims=True)
        hit = lane == idx
        experts.append(idx)
        vals.append(mx)
        onehot = onehot + hit.astype(F32)
        work = jnp.where(hit, -jnp.inf, work)
    ex = [jnp.exp(v - vals[0]) for v in vals]
    denom = ex[0] + ex[1] + ex[2] + ex[3]
    for r in range(TOP_K):
        e_ref[:, r:r + 1] = experts[r]
        gate_ref[:, r:r + 1] = ex[r] / denom

    rr = lax.broadcasted_iota(jnp.int32, (tm, tm), 0)
    cc = lax.broadcasted_iota(jnp.int32, (tm, tm), 1)
    lower = (cc < rr).astype(BF16)
    prefix = jnp.dot(lower, onehot.astype(BF16), preferred_element_type=F32)
    bc = jnp.sum(onehot, axis=0, keepdims=True)
    bc = jnp.floor((bc + (RUN_ALIGN - 1)) * (1.0 / RUN_ALIGN)) * RUN_ALIGN
    bc8 = jnp.broadcast_to(bc, (8, LANES))
    boff = jnp.dot(bc8.astype(BF16), upper_ref[...], preferred_element_type=F32)[0:1, :]
    local = prefix + boff
    for r in range(TOP_K):
        lp = jnp.sum(jnp.where(lane == experts[r], local, 0.0), axis=-1, keepdims=True)
        lp_ref[:, r:r + 1] = lp.astype(jnp.int32)
    bc_ref[0] = bc8
    carry_out_ref[0] = carry_ref[...]
    total = carry_ref[...] + bc8
    carry_ref[...] = total
    cnt_ref[...] = total


def _router(logits, tm):
    assert tm <= 256
    n = logits.shape[0]
    nblk = n // tm
    upper = (jnp.arange(LANES)[:, None] < jnp.arange(LANES)[None, :]).astype(BF16)
    row = lambda w_: pl.BlockSpec((tm, w_), lambda i: (i, 0))
    blk = pl.BlockSpec((1, 8, LANES), lambda i: (i, 0, 0))
    out_shape = (
        jax.ShapeDtypeStruct((n, TOP_K), jnp.int32),
        jax.ShapeDtypeStruct((n, TOP_K), F32),
        jax.ShapeDtypeStruct((n, TOP_K), jnp.int32),
        jax.ShapeDtypeStruct((nblk, 8, LANES), F32),
        jax.ShapeDtypeStruct((nblk, 8, LANES), F32),
        jax.ShapeDtypeStruct((8, LANES), F32),
    )
    return pl.pallas_call(
        functools.partial(_router_body, tm=tm),
        grid=(nblk,),
        in_specs=[row(LANES), pl.BlockSpec((LANES, LANES), lambda i: (0, 0))],
        out_specs=(row(TOP_K), row(TOP_K), row(TOP_K), blk, blk,
                   pl.BlockSpec((8, LANES), lambda i: (0, 0))),
        out_shape=out_shape,
        scratch_shapes=[pltpu.VMEM((8, LANES), F32)],
        compiler_params=_cparams(("arbitrary",)),
        name="router",
    )(logits, upper)


def _local_rows(tm):
    return TOP_K * tm + N_EXPERTS * RUN_ALIGN


def _for_each_run_chunk(tbl_ref, tm, fn):
    sizes = [s for s in (1 << k for k in range(tm.bit_length())) if RUN_ALIGN <= s <= tm]

    def per_expert(e, _):
        length = tbl_ref[0, 0, e]
        src = tbl_ref[0, 0, N_EXPERTS + e]
        dst = tbl_ref[0, 0, 2 * N_EXPERTS + e]
        for size in sizes:

            @pl.when((length & size) != 0)
            def _(size=size):
                off = length & (size - 1)
                fn(pl.multiple_of(src + off, RUN_ALIGN), pl.multiple_of(dst + off, RUN_ALIGN), size)
        return 0

    lax.fori_loop(0, N_EXPERTS, per_expert, 0)


def _dispatch_body(tbl_ref, lpt_ref, h_ref, xbuf_in_ref, xbuf_ref, sorted_ref, sem, *, tm):
    del xbuf_in_ref
    h = h_ref[...]
    lpt = lpt_ref[0]
    chunk = tm
    for c in range(_local_rows(tm) // chunk):
        r_idx = c * chunk + lax.broadcasted_iota(jnp.int32, (chunk, tm), 0)
        sel = jnp.zeros((chunk, tm), F32)
        for r in range(TOP_K):
            sel = sel + (r_idx == lpt[r:r + 1, :]).astype(F32)
        rows = jnp.dot(sel.astype(BF16), h, preferred_element_type=F32)
        sorted_ref[c * chunk:(c + 1) * chunk, :] = _pack_bf16_pairs(rows)

    def copy(local_row, global_row, size):
        return pltpu.make_async_copy(sorted_ref.at[pl.ds(local_row, size)],
                                     xbuf_ref.at[pl.ds(global_row, size)], sem)

    _for_each_run_chunk(tbl_ref, tm, lambda s, d, n: copy(s, d, n).start())
    _for_each_run_chunk(tbl_ref, tm, lambda s, d, n: copy(s, d, n).wait())


def _dispatch(h2, lpt, tbl, n_rows, tm):
    n = h2.shape[0]
    xbuf0 = jnp.zeros((n_rows, HALF_D), jnp.uint32)
    return pl.pallas_call(
        functools.partial(_dispatch_body, tm=tm),
        grid=(n // tm,),
        in_specs=[pl.BlockSpec((1, 1, LANES), lambda i: (i, 0, 0), memory_space=pltpu.SMEM),
                  pl.BlockSpec((1, TOP_K, tm), lambda i: (i, 0, 0)),
                  pl.BlockSpec((tm, D_MODEL), lambda i: (i, 0)),
                  pl.BlockSpec(memory_space=pl.ANY)],
        out_specs=pl.BlockSpec(memory_space=pl.ANY),
        out_shape=jax.ShapeDtypeStruct((n_rows, HALF_D), jnp.uint32),
        scratch_shapes=[pltpu.VMEM((_local_rows(tm), HALF_D), jnp.uint32),
                        pltpu.SemaphoreType.DMA],
        input_output_aliases={3: 0},
        compiler_params=_cparams(("arbitrary",)),
        name="moe_dispatch",
    )(tbl, lpt, h2, xbuf0)


def _expert_body(blk_e_ref, n_used_ref, x_ref, wg_ref, wl_ref, bg_ref, bl_ref, wd_ref, bd_ref,
                 y_ref, wg_s, wl_s, wd_s):
    i = pl.program_id(0)
    used = i < n_used_ref[0]

    changed = jnp.logical_or(i == 0, blk_e_ref[i] != blk_e_ref[jnp.maximum(i - 1, 0)])

    @pl.when(jnp.logical_and(used, changed))
    def _():
        wg_s[...] = wg_ref[0].astype(BF16)
        wl_s[...] = wl_ref[0].astype(BF16)
        wd_s[...] = wd_ref[0].astype(BF16)

    @pl.when(used)
    def _():
        xb = _unpack_bf16_pairs(x_ref[...]).astype(BF16)
        glu = jnp.dot(xb, wg_s[...], preferred_element_type=F32) + bg_ref[0]
        lin = jnp.dot(xb, wl_s[...], preferred_element_type=F32) + bl_ref[0]
        glu = jnp.minimum(glu, SWIGLU_LIMIT)
        lin = jnp.clip(lin, -SWIGLU_LIMIT, SWIGLU_LIMIT)
        act = glu * jax.nn.sigmoid(SWIGLU_ALPHA * glu) * (lin + 1.0)
        y = jnp.dot(act.astype(BF16), wd_s[...], preferred_element_type=F32) + bd_ref[0]
        y_ref[...] = _pack_bf16_pairs(y)

    @pl.when(jnp.logical_not(used))
    def _():
        y_ref[...] = jnp.zeros_like(y_ref)


def _experts(xbuf, blk_e, n_used, w_gate_up, b_gate_up, w_down, b_down, tb):
    n_rows = xbuf.shape[0]
    d = D_MODEL
    bgu = b_gate_up.reshape(N_EXPERTS, 1, 2 * D_FF)
    bd = b_down.reshape(N_EXPERTS, 1, d)
    grid_spec = pltpu.PrefetchScalarGridSpec(
        num_scalar_prefetch=2,
        grid=(n_rows // tb,),
        in_specs=[
            pl.BlockSpec((tb, HALF_D), lambda i, be, nu: (i, 0)),
            pl.BlockSpec((1, d, D_FF), lambda i, be, nu: (be[i], 0, 0)),
            pl.BlockSpec((1, d, D_FF), lambda i, be, nu: (be[i], 0, 1)),
            pl.BlockSpec((1, 1, D_FF), lambda i, be, nu: (be[i], 0, 0)),
            pl.BlockSpec((1, 1, D_FF), lambda i, be, nu: (be[i], 0, 1)),
            pl.BlockSpec((1, D_FF, d), lambda i, be, nu: (be[i], 0, 0)),
            pl.BlockSpec((1, 1, d), lambda i, be, nu: (be[i], 0, 0)),
        ],
        out_specs=pl.BlockSpec((tb, HALF_D), lambda i, be, nu: (i, 0)),
        scratch_shapes=[pltpu.VMEM((d, D_FF), BF16), pltpu.VMEM((d, D_FF), BF16),
                        pltpu.VMEM((D_FF, d), BF16)],
    )
    return pl.pallas_call(
        _expert_body,
        grid_spec=grid_spec,
        out_shape=jax.ShapeDtypeStruct((n_rows, HALF_D), jnp.uint32),
        compiler_params=_cparams(("arbitrary",)),
        name="moe_experts",
    )(blk_e, n_used, xbuf, w_gate_up, w_gate_up, bgu, bgu, w_down, bd)


def _combine_body(tbl_ref, x2_ref, gate_ref, lp_ref, g_ref, ybuf_ref, o_ref, ys_ref, sem, *, tm):
    n_local = _local_rows(tm)
    ys_ref[TOP_K * tm:, :] = jnp.zeros((n_local - TOP_K * tm, HALF_D), jnp.uint32)

    def copy(local_row, global_row, size):
        return pltpu.make_async_copy(ybuf_ref.at[pl.ds(global_row, size)],
                                     ys_ref.at[pl.ds(local_row, size)], sem)

    _for_each_run_chunk(tbl_ref, tm, lambda s, d, n: copy(s, d, n).start())
    _for_each_run_chunk(tbl_ref, tm, lambda s, d, n: copy(s, d, n).wait())

    gate = gate_ref[...]
    lp = lp_ref[...]
    col = lax.broadcasted_iota(jnp.int32, (tm, n_local), 1)
    gmat = jnp.zeros((tm, n_local), F32)
    for r in range(TOP_K):
        gmat = gmat + jnp.where(col == lp[:, r:r + 1], gate[:, r:r + 1], 0.0)
    g_hi, g_lo = _split_bf16(gmat)
    ys = _unpack_bf16_pairs(ys_ref[...]).astype(BF16)
    out = (x2_ref[...] + jnp.dot(g_hi, ys, preferred_element_type=F32)
           + jnp.dot(g_lo, ys, preferred_element_type=F32))
    o_ref[...] = out * lax.rsqrt(jnp.mean(out * out, axis=-1, keepdims=True) + EPS) * g_ref[...]


def _combine(x2, gate, lp, tbl, ybuf, final_norm_g, tm):
    n, d = x2.shape
    return pl.pallas_call(
        functools.partial(_combine_body, tm=tm),
        grid=(n // tm,),
        in_specs=[pl.BlockSpec((1, 1, LANES), lambda i: (i, 0, 0), memory_space=pltpu.SMEM),
                  pl.BlockSpec((tm, d), lambda i: (i, 0)),
                  pl.BlockSpec((tm, TOP_K), lambda i: (i, 0)),
                  pl.BlockSpec((tm, TOP_K), lambda i: (i, 0)),
                  pl.BlockSpec((1, d), lambda i: (0, 0)),
                  pl.BlockSpec(memory_space=pl.ANY)],
        out_specs=pl.BlockSpec((tm, d), lambda i: (i, 0)),
        out_shape=jax.ShapeDtypeStruct((n, d), F32),
        scratch_shapes=[pltpu.VMEM((_local_rows(tm), HALF_D), jnp.uint32),
                        pltpu.SemaphoreType.DMA],
        compiler_params=_cparams(("arbitrary",)),
        name="moe_combine",
    )(tbl, x2, gate, lp, final_norm_g.reshape(1, d), ybuf)


def _moe(x2, h2, logits, w_gate_up, b_gate_up, w_down, b_down, final_norm_g, tb, tm):
    n = x2.shape[0]
    nblk = n // tm
    max_rows = n * TOP_K + nblk * N_EXPERTS * (RUN_ALIGN - 1)
    n_blk = -(-max_rows // tb) + N_EXPERTS
    top_e, gate, lp, bcount, before, counts = _router(logits, tm)
    del top_e
    cnt = counts[0, :N_EXPERTS].astype(jnp.int32)
    padded = (cnt + tb - 1) // tb * tb
    pad_end = jnp.cumsum(padded)
    pad_start = pad_end - padded
    blk_first = jnp.arange(n_blk, dtype=jnp.int32) * tb
    blk_e = jnp.minimum(jnp.sum((pad_end[None, :] <= blk_first[:, None]).astype(jnp.int32), axis=1),
                        N_EXPERTS - 1)
    n_used = (pad_end[-1:] // tb).astype(jnp.int32)
    run_len = bcount[:, 0, :N_EXPERTS].astype(jnp.int32)
    run_src = jnp.cumsum(run_len, axis=1) - run_len
    run_dst = pad_start[None, :] + before[:, 0, :N_EXPERTS].astype(jnp.int32)
    tbl = jnp.concatenate([run_len, run_src, run_dst,
                           jnp.zeros((nblk, LANES - 3 * N_EXPERTS), jnp.int32)], axis=1)
    tbl = tbl.reshape(nblk, 1, LANES)
    lpt = lp.reshape(nblk, tm, TOP_K).transpose(0, 2, 1)
    xbuf = _dispatch(h2, lpt, tbl, n_blk * tb, tm)
    ybuf = _experts(xbuf, blk_e, n_used, w_gate_up, b_gate_up, w_down, b_down, tb)
    return _combine(x2, gate, lp, tbl, ybuf, final_norm_g, tm)


def kernel(x, positions, attn_norm_g, w_in, mla_q_norm_g, mla_w_uq, mla_kv_norm_g, mla_w_ukv,
           w_branch_mla, w_branch_dsa, w_out, ffn_norm_g, router_w, router_b,
           w_gate_up, b_gate_up, w_down, b_down, final_norm_g):
    b, s, d = x.shape
    n = b * s
    (qm, km, vm, qd, qi, kd, ki, vd, wi, ga, gb) = _in_proj(
        x.reshape(n, d), positions.reshape(n), attn_norm_g[0], w_in[0],
        mla_q_norm_g[0], mla_w_uq[0], mla_kv_norm_g[0], mla_w_ukv[0], tm=512)
    y_a = _mla_attention(qm, km, vm, b, s, tq=1024, tk=1024)
    y_b = _dsa_attention(qi, qd, ki, kd, vd, wi, b, s, tq=256, tk=1024, tka=1024)
    x2, h2, logits = _merge(
        x.reshape(n, d), y_a, y_b, ga, gb, w_branch_mla[0], w_branch_dsa[0], w_out[0],
        ffn_norm_g[0], router_w[0], router_b[0], tm=512)
    out = _moe(x2, h2, logits, w_gate_up[0], b_gate_up[0], w_down[0], b_down[0], final_norm_g,
               tb=512, tm=256)
    return out.reshape(b, s, d)
```

```python
import functools
import math

import jax
import jax.numpy as jnp
from jax import lax
from jax.experimental import pallas as pl
from jax.experimental.pallas import tpu as pltpu

F32 = jnp.float32
BF16 = jnp.bfloat16

LANES = 128
LOG2_E = 1.4426950408889634

D_MODEL = 1024
EPS = 1e-6
ROPE_THETA = 500000.0
MLA_HEADS = 8
MLA_Q_LORA = 256
MLA_KV_LORA = 128
MLA_NOPE = 64
MLA_ROPE = 32
MLA_V = 64
DSA_HEADS = 8
DSA_HEAD_DIM = 64
DSA_ROT = 16
IDX_HEADS = 8
IDX_DIM = 64
TOPK_MAX = 256
N_EXPERTS = 32
TOP_K = 4
D_FF = 1024
SWIGLU_LIMIT = 7.0
SWIGLU_ALPHA = 1.702

VMEM_LIMIT = 56 * 1024 * 1024

_OFF_CQ = 0
_OFF_CKV = _OFF_CQ + MLA_Q_LORA
_OFF_KPE = _OFF_CKV + MLA_KV_LORA
_OFF_QB = _OFF_KPE + MLA_ROPE
_OFF_KB = _OFF_QB + DSA_HEADS * DSA_HEAD_DIM
_OFF_VB = _OFF_KB + DSA_HEAD_DIM
_OFF_QI = _OFF_VB + DSA_HEAD_DIM
_OFF_KI = _OFF_QI + IDX_HEADS * IDX_DIM
_OFF_WI = _OFF_KI + IDX_DIM
_OFF_GA = _OFF_WI + IDX_HEADS
_OFF_GB = _OFF_GA + D_MODEL
_D_IN = _OFF_GB + D_MODEL


def _cparams(sem):
    return pltpu.CompilerParams(dimension_semantics=sem, vmem_limit_bytes=VMEM_LIMIT)


def _rope_group(xg, c, slo, shi, shift):
    return (xg * c + pltpu.roll(xg, LANES - shift, 1) * slo
            + pltpu.roll(xg, shift, 1) * shi)


def _in_proj_body(x_ref, g_ref, w1_ref, w2_ref, w3_ref, w5_ref, qng_ref, wuq_ref,
                  kvng_ref, wuk_ref, wuv_ref, em_ref, bm_ref, ed_ref, bd_ref, csm_ref, csd_ref,
                  qm_ref, km_ref, vm_ref, qd_ref, qi_ref, kd_ref, ki_ref, vd_ref,
                  wi_ref, ga_ref, gb_ref):
    x = x_ref[...]
    h = x * lax.rsqrt(jnp.mean(x * x, axis=-1, keepdims=True) + EPS) * g_ref[...]
    hb = h.astype(BF16)

    def spread(cs_ref, e_ref, b_ref):
        hi, lo = _split_bf16(cs_ref[...])
        t = (jnp.dot(hi, e_ref[...], preferred_element_type=F32)
             + jnp.dot(lo, e_ref[...], preferred_element_type=F32))
        return t[:, :LANES] + b_ref[...], t[:, LANES:2 * LANES], t[:, 2 * LANES:]

    cm, slm, shm = spread(csm_ref, em_ref, bm_ref)
    cd, sld, shd = spread(csd_ref, ed_ref, bd_ref)

    z1 = jnp.dot(hb, w1_ref[...], preferred_element_type=F32)
    cq = z1[:, 0:MLA_Q_LORA]
    cqn = cq * lax.rsqrt(jnp.mean(cq * cq, axis=-1, keepdims=True) + EPS) * qng_ref[...]
    q = jnp.dot(cqn.astype(BF16), wuq_ref[...], preferred_element_type=F32)
    q_scale = (MLA_NOPE + MLA_ROPE) ** -0.5 * LOG2_E
    for j in range(MLA_HEADS):
        qg = _rope_group(q[:, j * LANES:(j + 1) * LANES], cm, slm, shm, MLA_ROPE // 2)
        qm_ref[:, j * LANES:(j + 1) * LANES] = (qg * q_scale).astype(BF16)

    ckv = z1[:, MLA_Q_LORA:MLA_Q_LORA + MLA_KV_LORA]
    ckvn = (ckv * lax.rsqrt(jnp.mean(ckv * ckv, axis=-1, keepdims=True) + EPS)
            * kvng_ref[...]).astype(BF16)
    kpe = _rope_group(z1[:, MLA_Q_LORA + MLA_KV_LORA:], cm, slm, shm, MLA_ROPE // 2)
    kn = jnp.dot(ckvn, wuk_ref[...], preferred_element_type=F32)
    for j in range(MLA_HEADS):
        km_ref[:, j * LANES:(j + 1) * LANES] = (kn[:, j * LANES:(j + 1) * LANES] + kpe).astype(BF16)
    vm_ref[...] = jnp.dot(ckvn, wuv_ref[...], preferred_element_type=F32).astype(BF16)

    z2 = jnp.dot(hb, w2_ref[...], preferred_element_type=F32)
    d_scale = DSA_HEAD_DIM ** -0.5 * LOG2_E
    for j in range(DSA_HEADS // 2):
        g = _rope_group(z2[:, j * LANES:(j + 1) * LANES], cd, sld, shd, DSA_ROT // 2) * d_scale
        qd_ref[2 * j] = g[:, :DSA_HEAD_DIM].astype(BF16)
        qd_ref[2 * j + 1] = g[:, DSA_HEAD_DIM:].astype(BF16)
    base = DSA_HEADS * DSA_HEAD_DIM
    for j in range(IDX_HEADS // 2):
        g = _rope_group(z2[:, base + j * LANES:base + (j + 1) * LANES], cd, sld, shd, DSA_ROT // 2)
        qi_ref[2 * j] = g[:, :IDX_DIM].astype(BF16)
        qi_ref[2 * j + 1] = g[:, IDX_DIM:].astype(BF16)

    z3 = jnp.dot(hb, w3_ref[...], preferred_element_type=F32)
    kb = _rope_group(z3[:, :LANES], cd, sld, shd, DSA_ROT // 2)
    kd_ref[...] = kb[:, :DSA_HEAD_DIM].astype(BF16)
    ki_ref[...] = kb[:, DSA_HEAD_DIM:].astype(BF16)
    zv = z3[:, LANES:]
    vlane = lax.broadcasted_iota(jnp.int32, zv.shape, 1)
    vd_ref[...] = jnp.where(vlane < DSA_HEAD_DIM, zv,
                            jnp.where(vlane == DSA_HEAD_DIM, 1.0, 0.0)).astype(BF16)
    w_scale = IDX_HEADS ** -0.5 * IDX_DIM ** -0.5
    wi_ref[...] = z3[:, LANES + DSA_HEAD_DIM:LANES + DSA_HEAD_DIM + IDX_HEADS] * w_scale

    z5 = jnp.dot(hb, w5_ref[...], preferred_element_type=F32)
    ga_ref[...] = jax.nn.sigmoid(z5[:, :D_MODEL]).astype(BF16)
    gb_ref[...] = jax.nn.sigmoid(z5[:, D_MODEL:]).astype(BF16)


ROPE_COLS = 16


def _rope_tables(pos, rot_dim, lane_of_x1, period):
    half = rot_dim // 2
    inv_freq = ROPE_THETA ** (-jnp.arange(half, dtype=F32) / half)
    ang = pos.astype(F32)[:, None] * inv_freq
    pad = ((0, 0), (0, ROPE_COLS - half))
    cs = jnp.concatenate([jnp.pad(jnp.cos(ang), pad), jnp.pad(jnp.sin(ang), pad)], axis=1)

    lane = jnp.arange(LANES)
    in_period = lane % period - lane_of_x1
    freq = jnp.arange(ROPE_COLS)[:, None]
    on_x1 = (in_period[None, :] == freq) & (freq < half)
    on_x2 = (in_period[None, :] - half == freq) & (freq < half)
    zero = jnp.zeros((ROPE_COLS, LANES), F32)
    f = lambda m: m.astype(F32)
    spread = jnp.concatenate([
        jnp.concatenate([f(on_x1 | on_x2), zero, zero], axis=1),
        jnp.concatenate([zero, -f(on_x1), f(on_x2)], axis=1),
    ], axis=0).astype(BF16)
    rotated = (in_period >= 0) & (in_period < rot_dim)
    bias = jnp.where(rotated, 0.0, 1.0).astype(F32).reshape(1, LANES)
    return cs, spread, bias


def _head_cols(w, n_heads, widths, total):
    k = w.shape[0]
    per = sum(widths)
    w = w.reshape(k, n_heads, per)
    return jnp.pad(w, ((0, 0), (0, 0), (0, total - per))).reshape(k, n_heads * total)


def _in_proj(x2, pos, attn_norm_g, w_in, q_norm_g, w_uq, kv_norm_g, w_ukv, tm):
    n = x2.shape[0]
    d = D_MODEL
    zc = lambda k: jnp.zeros((d, k), F32)
    w1 = jnp.concatenate([w_in[:, _OFF_CQ:_OFF_KPE], zc(MLA_NOPE), w_in[:, _OFF_KPE:_OFF_QB],
                          zc(LANES - MLA_NOPE - MLA_ROPE)], axis=1).astype(BF16)
    w2 = jnp.concatenate([w_in[:, _OFF_QB:_OFF_KB], w_in[:, _OFF_QI:_OFF_KI]], axis=1).astype(BF16)
    w3 = jnp.concatenate([w_in[:, _OFF_KB:_OFF_VB], w_in[:, _OFF_KI:_OFF_WI],
                          w_in[:, _OFF_VB:_OFF_QI], w_in[:, _OFF_WI:_OFF_GA],
                          zc(LANES - DSA_HEAD_DIM - IDX_HEADS)], axis=1).astype(BF16)
    w5 = w_in[:, _OFF_GA:].astype(BF16)
    wuq = _head_cols(w_uq, MLA_HEADS, (MLA_NOPE, MLA_ROPE), LANES).astype(BF16)
    w_ukv3 = w_ukv.reshape(MLA_KV_LORA, MLA_HEADS, MLA_NOPE + MLA_V)
    wuk = jnp.pad(w_ukv3[:, :, :MLA_NOPE], ((0, 0), (0, 0), (0, LANES - MLA_NOPE))
                  ).reshape(MLA_KV_LORA, MLA_HEADS * LANES).astype(BF16)
    wuv = w_ukv3[:, :, MLA_NOPE:].reshape(MLA_KV_LORA, MLA_HEADS * MLA_V).astype(BF16)
    csm, em, bm = _rope_tables(pos, MLA_ROPE, MLA_NOPE, LANES)
    csd, ed, bdd = _rope_tables(pos, DSA_ROT, 0, DSA_HEAD_DIM)

    row = lambda w_: pl.BlockSpec((tm, w_), lambda i: (i, 0))
    full = lambda a: pl.BlockSpec(a.shape, lambda i: (0,) * a.ndim)
    hm = pl.BlockSpec((DSA_HEADS, tm, DSA_HEAD_DIM), lambda i: (0, i, 0))
    g2 = attn_norm_g.reshape(1, d)
    qng = q_norm_g.reshape(1, -1)
    kvng = kv_norm_g.reshape(1, -1)
    consts = (g2, w1, w2, w3, w5, qng, wuq, kvng, wuk, wuv, em, bm, ed, bdd)
    out_shape = (
        jax.ShapeDtypeStruct((n, MLA_HEADS * LANES), BF16),
        jax.ShapeDtypeStruct((n, MLA_HEADS * LANES), BF16),
        jax.ShapeDtypeStruct((n, MLA_HEADS * MLA_V), BF16),
        jax.ShapeDtypeStruct((DSA_HEADS, n, DSA_HEAD_DIM), BF16),
        jax.ShapeDtypeStruct((IDX_HEADS, n, IDX_DIM), BF16),
        jax.ShapeDtypeStruct((n, DSA_HEAD_DIM), BF16),
        jax.ShapeDtypeStruct((n, IDX_DIM), BF16),
        jax.ShapeDtypeStruct((n, LANES), BF16),
        jax.ShapeDtypeStruct((n, IDX_HEADS), F32),
        jax.ShapeDtypeStruct((n, D_MODEL), BF16),
        jax.ShapeDtypeStruct((n, D_MODEL), BF16),
    )
    out_specs = (row(MLA_HEADS * LANES), row(MLA_HEADS * LANES), row(MLA_HEADS * MLA_V),
                 hm, hm, row(DSA_HEAD_DIM), row(IDX_DIM), row(LANES), row(IDX_HEADS),
                 row(D_MODEL), row(D_MODEL))
    return pl.pallas_call(
        _in_proj_body,
        grid=(n // tm,),
        in_specs=[row(d)] + [full(a) for a in consts] + [row(2 * ROPE_COLS)] * 2,
        out_specs=out_specs,
        out_shape=out_shape,
        compiler_params=_cparams(("parallel",)),
        name="in_proj",
    )(x2, *consts, csm, csd)


NEG_BIG = -1e30
_NT = (((1,), (1,)), ((), ()))


def _mla_body(q_ref, k_ref, v_ref, o_ref, *, tq, tk):
    qi = pl.program_id(2)
    n_sub = tq // tk
    qs = [q_ref[:, hh * LANES:(hh + 1) * LANES] for hh in range(2)]

    def step(j, carry, masked):
        start = pl.multiple_of(j * tk, tk)
        vs = v_ref[pl.ds(start, tk), :]
        new = []
        for hh in range(2):
            m, l, acc = carry[hh]
            ks = k_ref[pl.ds(start, tk), hh * LANES:(hh + 1) * LANES]
            s = lax.dot_general(qs[hh], ks, _NT, preferred_element_type=F32)
            if masked:
                row = qi * tq + lax.broadcasted_iota(jnp.int32, (tq, tk), 0)
                col = j * tk + lax.broadcasted_iota(jnp.int32, (tq, tk), 1)
                s = jnp.where(col <= row, s, NEG_BIG)
            m_new = jnp.maximum(m, jnp.max(s, axis=-1, keepdims=True))
            alpha = jnp.exp2(m - m_new)
            p = jnp.exp2(s - m_new)
            l = alpha * l + jnp.sum(p, axis=-1, keepdims=True)
            acc = alpha * acc + jnp.dot(p.astype(BF16), vs, preferred_element_type=F32)
            new.append((m_new, l, acc))
        return tuple(new)

    one = (jnp.full((tq, 1), NEG_BIG, F32), jnp.zeros((tq, 1), F32), jnp.zeros((tq, LANES), F32))
    carry = lax.fori_loop(0, qi * n_sub, functools.partial(step, masked=False), (one, one))
    for dd in range(n_sub):
        carry = step(qi * n_sub + dd, carry, True)
    outs = [acc / l for (_, l, acc) in carry]
    lane = lax.broadcasted_iota(jnp.int32, (tq, LANES), 1)
    o_ref[...] = jnp.where(lane < MLA_V, outs[0], outs[1]).astype(BF16)


def _mla_attention(qm, km, vm, b, s, tq, tk):
    n = b * s
    nq = s // tq
    return pl.pallas_call(
        functools.partial(_mla_body, tq=tq, tk=tk),
        grid=(b, MLA_HEADS // 2, nq),
        in_specs=[
            pl.BlockSpec((tq, 2 * LANES), lambda bi, hp, qi: (bi * nq + qi, hp)),
            pl.BlockSpec((s, 2 * LANES), lambda bi, hp, qi: (bi, hp)),
            pl.BlockSpec((s, 2 * MLA_V), lambda bi, hp, qi: (bi, hp)),
        ],
        out_specs=pl.BlockSpec((tq, 2 * MLA_V), lambda bi, hp, qi: (bi * nq + qi, hp)),
        out_shape=jax.ShapeDtypeStruct((n, MLA_HEADS * MLA_V), BF16),
        compiler_params=_cparams(("parallel", "parallel", "arbitrary")),
        name="mla_attention",
    )(qm, km, vm)


INT_MIN = -2 ** 31
ACC_ROWS = 64
SEARCH_GROUPS = 256
SNAP_AFTER_TRIPS = 5
KEY_NEG_INF = (0xFF800000 - 2 ** 32) ^ 0x7FFFFFFF


def _sortable_key(score):
    bits = pltpu.bitcast(score, jnp.int32)
    return bits ^ ((bits >> 31) & 0x7FFFFFFF)


def _sortable_key_inverse(key):
    return pltpu.bitcast(key ^ ((key >> 31) & 0x7FFFFFFF), F32)


def _dsa_body(qi_ref, qd_ref, ki_ref, kd_ref, vd_ref, w_ref, o_ref, keys_ref, keyst_ref, *,
              tq, tk, tka, topk, s_len):
    qb = pl.program_id(1)
    n_tiles = (qb * tq + tq + tk - 1) // tk
    q_pos = qb * tq + lax.broadcasted_iota(jnp.int32, (tq, 1), 0)
    q_pos_t = qb * tq + lax.broadcasted_iota(jnp.int32, (1, tq), 1)

    qidx = qi_ref[...].reshape(IDX_HEADS * tq, IDX_DIM)
    w = w_ref[...]
    wcols = [jnp.broadcast_to(w[:, h:h + 1], (tq, tk)) for h in range(IDX_HEADS)]

    def score_tile(j, _):
        start = pl.multiple_of(j * tk, tk)
        kt = ki_ref[pl.ds(start, tk), :]
        sh = lax.dot_general(qidx, kt, _NT, preferred_element_type=F32)
        sc = jnp.zeros((tq, tk), F32)
        for h in range(IDX_HEADS):
            sc = sc + jnp.maximum(sh[h * tq:(h + 1) * tq], 0.0) * wcols[h]
        col = j * tk + lax.broadcasted_iota(jnp.int32, (tq, tk), 1)
        sc = jnp.where(col <= q_pos, sc, -jnp.inf)
        keys = _sortable_key(sc)
        keys_ref[:, pl.ds(start, tk)] = keys
        keyst_ref[pl.ds(start, tk), :] = keys.T
        return 0

    lax.fori_loop(0, n_tiles, score_tile, 0)

    def tile_t(j):
        return keyst_ref[pl.ds(pl.multiple_of(j * tk, tk), tk), :]

    def count_where(pred):
        def body(j, cnt):
            hit = pred(tile_t(j), j).astype(jnp.int32)
            return cnt + jnp.sum(hit.reshape(tk // ACC_ROWS, ACC_ROWS, tq), axis=0)
        cnt = lax.fori_loop(0, n_tiles, body, jnp.zeros((ACC_ROWS, tq), jnp.int32))
        return jnp.sum(cnt, axis=0, keepdims=True)

    def count_ge(cand):
        return count_where(lambda kt, j: kt >= cand)

    def max_le(bound):
        def body(j, mx):
            kt = tile_t(j)
            kt = jnp.where(kt <= bound, kt, INT_MIN)
            return jnp.maximum(mx, jnp.max(kt.reshape(tk // ACC_ROWS, ACC_ROWS, tq), axis=0))
        mx = lax.fori_loop(0, n_tiles, body, jnp.full((ACC_ROWS, tq), INT_MIN, jnp.int32))
        return jnp.max(mx, axis=0, keepdims=True)

    def group_max(j, g):
        return jnp.maximum(g, jnp.max(tile_t(j).reshape(tk // SEARCH_GROUPS, SEARCH_GROUPS, tq),
                                      axis=0))

    g = lax.fori_loop(0, n_tiles, group_max, jnp.full((SEARCH_GROUPS, tq), INT_MIN, jnp.int32))
    hi = jnp.max(g, axis=0, keepdims=True)
    lo = jnp.min(g, axis=0, keepdims=True)
    need = q_pos_t + 1 > topk
    lo = jnp.where(need, lo, KEY_NEG_INF + 1)
    hi = jnp.where(need, hi, KEY_NEG_INF + 1)

    def n_active(lo, hi, cnt_lo):
        return jnp.max(((lo < hi) & (cnt_lo != topk)).astype(jnp.int32))

    def halve_step(lo, hi, cnt_lo):
        mid = _sortable_key(0.5 * _sortable_key_inverse(lo) + 0.5 * _sortable_key_inverse(hi))
        mid = jnp.minimum(jnp.maximum(mid, lo + 1), hi)
        cnt = count_ge(mid)
        ok = cnt >= topk
        return jnp.where(ok, mid, lo), jnp.where(ok, hi, mid - 1), jnp.where(ok, cnt, cnt_lo)

    def snap_step(lo, hi, cnt_lo):
        mid = jnp.minimum(jnp.maximum(max_le(hi), lo + 1), hi)
        cnt = count_ge(mid)
        ok = cnt >= topk
        return jnp.where(ok, mid, lo), jnp.where(ok, mid, mid - 1), jnp.where(ok, cnt, cnt_lo)

    def narrow(carry):
        lo, hi, cnt_lo, trip, _ = carry
        lo, hi, cnt_lo = halve_step(lo, hi, cnt_lo)
        lo, hi, cnt_lo = lax.cond(trip >= SNAP_AFTER_TRIPS, snap_step, halve_step, lo, hi, cnt_lo)
        return lo, hi, cnt_lo, trip + 1, n_active(lo, hi, cnt_lo)

    cnt_lo = count_ge(lo)
    thr, _, cnt_thr, _, _ = lax.while_loop(
        lambda c: c[4] > 0, narrow, (lo, hi, cnt_lo, jnp.int32(0), n_active(lo, hi, cnt_lo)))
    thr = jnp.maximum(thr, KEY_NEG_INF + 1)

    tied = jnp.logical_and(need, cnt_thr > topk)
    keep_all = jnp.full((1, tq), s_len, jnp.int32)

    def tie_cut():
        n_keep = topk - count_ge(thr + 1)

        def step(_, carry):
            jlo, jhi = carry
            jm = (jlo + jhi) >> 1

            def pred(kt, j):
                kpos = j * tk + lax.broadcasted_iota(jnp.int32, (tk, tq), 0)
                return jnp.logical_and(kt == thr, kpos <= jm)

            ok = count_where(pred) >= n_keep
            return jnp.where(ok, jlo, jm + 1), jnp.where(ok, jm, jhi)

        _, jhi = lax.fori_loop(0, (s_len - 1).bit_length(), step,
                               (jnp.zeros((1, tq), jnp.int32), keep_all - 1))
        return jnp.where(tied, jhi, keep_all)

    jcut = lax.cond(jnp.max(tied.astype(jnp.int32)) > 0, tie_cut, lambda: keep_all)

    def to_column(v):
        return jnp.transpose(jnp.broadcast_to(v, (LANES, tq)))[:, 0:1]

    thr = to_column(thr)
    jcut = to_column(jcut)

    qd = qd_ref[...].reshape(DSA_HEADS * tq, DSA_HEAD_DIM)

    def attn_tile(j, carry):
        m, acc = carry
        start = pl.multiple_of(j * tka, tka)
        kt = kd_ref[pl.ds(start, tka), :]
        vt = vd_ref[pl.ds(start, tka), :]
        keys = keys_ref[:, pl.ds(start, tka)]
        col = j * tka + lax.broadcasted_iota(jnp.int32, (tq, tka), 1)
        sel = jnp.logical_or(keys > thr, jnp.logical_and(keys == thr, col <= jcut))
        s = lax.dot_general(qd, kt, _NT, preferred_element_type=F32)
        s = jnp.where(sel[None], s.reshape(DSA_HEADS, tq, tka), NEG_BIG)
        m_new = jnp.maximum(m, jnp.max(s, axis=-1, keepdims=True))
        p = jnp.exp2(s - m_new).astype(BF16)
        pv = jnp.dot(p.reshape(DSA_HEADS * tq, tka), vt, preferred_element_type=F32)
        acc = jnp.exp2(m - m_new) * acc + pv.reshape(DSA_HEADS, tq, LANES)
        return m_new, acc

    init = (jnp.full((DSA_HEADS, tq, 1), NEG_BIG, F32), jnp.zeros((DSA_HEADS, tq, LANES), F32))
    m, acc = lax.fori_loop(0, n_tiles * (tk // tka), attn_tile, init)
    for h in range(DSA_HEADS):
        out = acc[h, :, :DSA_HEAD_DIM] / acc[h, :, DSA_HEAD_DIM:DSA_HEAD_DIM + 1]
        o_ref[:, h * DSA_HEAD_DIM:(h + 1) * DSA_HEAD_DIM] = out.astype(BF16)


def _dsa_attention(qi, qd, ki, kd, vd, wi, b, s, tq, tk, tka):
    assert tk % tka == 0 and tk % SEARCH_GROUPS == 0 and SEARCH_GROUPS >= TOPK_MAX
    n = b * s
    nq = s // tq
    topk = min(TOPK_MAX, s // 4)
    hm = pl.BlockSpec((DSA_HEADS, tq, DSA_HEAD_DIM), lambda bi, qb: (0, bi * nq + qb, 0))
    kv = pl.BlockSpec((s, DSA_HEAD_DIM), lambda bi, qb: (bi, 0))
    return pl.pallas_call(
        functools.partial(_dsa_body, tq=tq, tk=tk, tka=tka, topk=topk, s_len=s),
        grid=(b, nq),
        in_specs=[hm, hm, kv, kv, pl.BlockSpec((s, LANES), lambda bi, qb: (bi, 0)),
                  pl.BlockSpec((tq, IDX_HEADS), lambda bi, qb: (bi * nq + qb, 0))],
        out_specs=pl.BlockSpec((tq, DSA_HEADS * DSA_HEAD_DIM), lambda bi, qb: (bi * nq + qb, 0)),
        out_shape=jax.ShapeDtypeStruct((n, DSA_HEADS * DSA_HEAD_DIM), BF16),
        scratch_shapes=[pltpu.VMEM((tq, s), jnp.int32), pltpu.VMEM((s, tq), jnp.int32)],
        compiler_params=_cparams(("parallel", "arbitrary")),
        name="dsa_attention",
    )(qi, qd, ki, kd, vd, wi)


HALF_D = D_MODEL // 2
RUN_ALIGN = 8


def _pack_bf16_pairs(y):
    r = pltpu.bitcast(y.astype(BF16).astype(F32), jnp.uint32)
    return r[:, :HALF_D] | (r[:, HALF_D:] >> 16)


def _unpack_bf16_pairs(p):
    hi = pltpu.bitcast(p & jnp.uint32(0xFFFF0000), F32)
    lo = pltpu.bitcast(p << 16, F32)
    return jnp.concatenate([hi, lo], axis=1)


def _split_bf16(a):
    hi = a.astype(BF16)
    lo = (a - hi.astype(F32)).astype(BF16)
    return hi, lo


def _merge_body(x_ref, ya_ref, yb_ref, ga_ref, gb_ref, wa_ref, wb_ref, wo_ref, g_ref,
                rwh_ref, rwl_ref, rb_ref, x2_ref, h_ref, logit_ref):
    ma = jnp.dot(ya_ref[...], wa_ref[...], preferred_element_type=F32)
    mb = jnp.dot(yb_ref[...], wb_ref[...], preferred_element_type=F32)
    merged = ga_ref[...].astype(F32) * ma + gb_ref[...].astype(F32) * mb
    x2 = x_ref[...] + jnp.dot(merged.astype(BF16), wo_ref[...], preferred_element_type=F32)
    x2_ref[...] = x2
    h = x2 * lax.rsqrt(jnp.mean(x2 * x2, axis=-1, keepdims=True) + EPS) * g_ref[...]
    hh, hl = _split_bf16(h)
    h_ref[...] = hh
    logit_ref[...] = (jnp.dot(hh, rwh_ref[...], preferred_element_type=F32)
                      + jnp.dot(hh, rwl_ref[...], preferred_element_type=F32)
                      + jnp.dot(hl, rwh_ref[...], preferred_element_type=F32)) + rb_ref[...]


def _merge(x2d, y_a, y_b, ga, gb, w_ba, w_bb, w_out, ffn_norm_g, router_w, router_b, tm):
    n, d = x2d.shape
    rw = jnp.pad(router_w, ((0, 0), (0, LANES - N_EXPERTS)))
    rwh = rw.astype(BF16)
    rwl = (rw - rwh.astype(F32)).astype(BF16)
    rb = jnp.pad(router_b, (0, LANES - N_EXPERTS), constant_values=NEG_BIG).reshape(1, LANES)
    consts = (w_ba.astype(BF16), w_bb.astype(BF16), w_out.astype(BF16), ffn_norm_g.reshape(1, d),
              rwh, rwl, rb)
    row = lambda w_: pl.BlockSpec((tm, w_), lambda i: (i, 0))
    full = lambda a: pl.BlockSpec(a.shape, lambda i: (0,) * a.ndim)
    out_shape = (
        jax.ShapeDtypeStruct((n, d), F32),
        jax.ShapeDtypeStruct((n, d), BF16),
        jax.ShapeDtypeStruct((n, LANES), F32),
    )
    return pl.pallas_call(
        _merge_body,
        grid=(n // tm,),
        in_specs=[row(d), row(HALF_D), row(HALF_D), row(d), row(d)] + [full(a) for a in consts],
        out_specs=(row(d), row(d), row(LANES)),
        out_shape=out_shape,
        compiler_params=_cparams(("parallel",)),
        name="merge",
    )(x2d, y_a, y_b, ga, gb, *consts)


def _router_body(logit_ref, upper_ref, e_ref, gate_ref, lp_ref, bc_ref, carry_out_ref, cnt_ref,
                 carry_ref, *, tm):
    i = pl.program_id(0)

    @pl.when(i == 0)
    def _():
        carry_ref[...] = jnp.zeros_like(carry_ref)

    lane = lax.broadcasted_iota(jnp.int32, (tm, LANES), 1)
    work = logit_ref[...]
    experts, vals = [], []
    onehot = jnp.zeros((tm, LANES), F32)
    for _ in range(TOP_K):
        mx = jnp.max(work, axis=-1, keepdims=True)
        idx = jnp.min(jnp.where(work == mx, lane, LANES), axis=-1, keepdims=True)
        hit = lane == idx
        experts.append(idx)
        vals.append(mx)
        onehot = onehot + hit.astype(F32)
        work = jnp.where(hit, -jnp.inf, work)
    ex = [jnp.exp(v - vals[0]) for v in vals]
    denom = ex[0] + ex[1] + ex[2] + ex[3]
    for r in range(TOP_K):
        e_ref[:, r:r + 1] = experts[r]
        gate_ref[:, r:r + 1] = ex[r] / denom

    rr = lax.broadcasted_iota(jnp.int32, (tm, tm), 0)
    cc = lax.broadcasted_iota(jnp.int32, (tm, tm), 1)
    lower = (cc < rr).astype(BF16)
    prefix = jnp.dot(lower, onehot.astype(BF16), preferred_element_type=F32)
    bc = jnp.sum(onehot, axis=0, keepdims=True)
    bc = jnp.floor((bc + (RUN_ALIGN - 1)) * (1.0 / RUN_ALIGN)) * RUN_ALIGN
    bc8 = jnp.broadcast_to(bc, (8, LANES))
    boff = jnp.dot(bc8.astype(BF16), upper_ref[...], preferred_element_type=F32)[0:1, :]
    local = prefix + boff
    for r in range(TOP_K):
        lp = jnp.sum(jnp.where(lane == experts[r], local, 0.0), axis=-1, keepdims=True)
        lp_ref[:, r:r + 1] = lp.astype(jnp.int32)
    bc_ref[0] = bc8
    carry_out_ref[0] = carry_ref[...]
    total = carry_ref[...] + bc8
    carry_ref[...] = total
    cnt_ref[...] = total


def _router(logits, tm):
    assert tm <= 256
    n = logits.shape[0]
    nblk = n // tm
    upper = (jnp.arange(LANES)[:, None] < jnp.arange(LANES)[None, :]).astype(BF16)
    row = lambda w_: pl.BlockSpec((tm, w_), lambda i: (i, 0))
    blk = pl.BlockSpec((1, 8, LANES), lambda i: (i, 0, 0))
    out_shape = (
        jax.ShapeDtypeStruct((n, TOP_K), jnp.int32),
        jax.ShapeDtypeStruct((n, TOP_K), F32),
        jax.ShapeDtypeStruct((n, TOP_K), jnp.int32),
        jax.ShapeDtypeStruct((nblk, 8, LANES), F32),
        jax.ShapeDtypeStruct((nblk, 8, LANES), F32),
        jax.ShapeDtypeStruct((8, LANES), F32),
    )
    return pl.pallas_call(
        functools.partial(_router_body, tm=tm),
        grid=(nblk,),
        in_specs=[row(LANES), pl.BlockSpec((LANES, LANES), lambda i: (0, 0))],
        out_specs=(row(TOP_K), row(TOP_K), row(TOP_K), blk, blk,
                   pl.BlockSpec((8, LANES), lambda i: (0, 0))),
        out_shape=out_shape,
        scratch_shapes=[pltpu.VMEM((8, LANES), F32)],
        compiler_params=_cparams(("arbitrary",)),
        name="router",
    )(logits, upper)


def _local_rows(tm):
    return TOP_K * tm + N_EXPERTS * RUN_ALIGN


def _for_each_run_chunk(tbl_ref, tm, fn):
    sizes = [s for s in (1 << k for k in range(tm.bit_length())) if RUN_ALIGN <= s <= tm]

    def per_expert(e, _):
        length = tbl_ref[0, 0, e]
        src = tbl_ref[0, 0, N_EXPERTS + e]
        dst = tbl_ref[0, 0, 2 * N_EXPERTS + e]
        for size in sizes:

            @pl.when((length & size) != 0)
            def _(size=size):
                off = length & (size - 1)
                fn(pl.multiple_of(src + off, RUN_ALIGN), pl.multiple_of(dst + off, RUN_ALIGN), size)
        return 0

    lax.fori_loop(0, N_EXPERTS, per_expert, 0)


def _dispatch_body(tbl_ref, tbl_prev_ref, lpt_ref, h_ref, xbuf_in_ref, xbuf_ref, sorted_ref, sems,
                   *, tm):
    del xbuf_in_ref
    i = pl.program_id(0)
    slot = i % 2
    h = h_ref[...]
    lpt = lpt_ref[0]
    chunk = tm
    for c in range(_local_rows(tm) // chunk):
        r_idx = c * chunk + lax.broadcasted_iota(jnp.int32, (chunk, tm), 0)
        sel = jnp.zeros((chunk, tm), F32)
        for r in range(TOP_K):
            sel = sel + (r_idx == lpt[r:r + 1, :]).astype(F32)
        rows = jnp.dot(sel.astype(BF16), h, preferred_element_type=F32)
        sorted_ref[slot, c * chunk:(c + 1) * chunk, :] = _pack_bf16_pairs(rows)

    def copy(buf, local_row, global_row, size):
        return pltpu.make_async_copy(sorted_ref.at[buf, pl.ds(local_row, size)],
                                     xbuf_ref.at[pl.ds(global_row, size)], sems.at[buf])

    _for_each_run_chunk(tbl_ref, tm, lambda s, d, n: copy(slot, s, d, n).start())

    @pl.when(i > 0)
    def _():
        _for_each_run_chunk(tbl_prev_ref, tm, lambda s, d, n: copy(1 - slot, s, d, n).wait())

    @pl.when(i == pl.num_programs(0) - 1)
    def _():
        _for_each_run_chunk(tbl_ref, tm, lambda s, d, n: copy(slot, s, d, n).wait())


def _dispatch(h2, lpt, tbl, n_rows, tm):
    n = h2.shape[0]
    xbuf0 = jnp.zeros((n_rows, HALF_D), jnp.uint32)
    smem_tbl = lambda index_map: pl.BlockSpec((1, 1, LANES), index_map, memory_space=pltpu.SMEM)
    return pl.pallas_call(
        functools.partial(_dispatch_body, tm=tm),
        grid=(n // tm,),
        in_specs=[smem_tbl(lambda i: (i, 0, 0)),
                  smem_tbl(lambda i: (jnp.maximum(i - 1, 0), 0, 0)),
                  pl.BlockSpec((1, TOP_K, tm), lambda i: (i, 0, 0)),
                  pl.BlockSpec((tm, D_MODEL), lambda i: (i, 0)),
                  pl.BlockSpec(memory_space=pl.ANY)],
        out_specs=pl.BlockSpec(memory_space=pl.ANY),
        out_shape=jax.ShapeDtypeStruct((n_rows, HALF_D), jnp.uint32),
        scratch_shapes=[pltpu.VMEM((2, _local_rows(tm), HALF_D), jnp.uint32),
                        pltpu.SemaphoreType.DMA((2,))],
        input_output_aliases={4: 0},
        compiler_params=_cparams(("arbitrary",)),
        name="moe_dispatch",
    )(tbl, tbl, lpt, h2, xbuf0)


def _expert_body(blk_e_ref, n_used_ref, x_ref, wg_ref, wl_ref, bg_ref, bl_ref, wd_ref, bd_ref,
                 y_ref, wg_s, wl_s, wd_s):
    i = pl.program_id(0)
    used = i < n_used_ref[0]

    changed = jnp.logical_or(i == 0, blk_e_ref[i] != blk_e_ref[jnp.maximum(i - 1, 0)])

    @pl.when(jnp.logical_and(used, changed))
    def _():
        wg_s[...] = wg_ref[0].astype(BF16)
        wl_s[...] = wl_ref[0].astype(BF16)
        wd_s[...] = wd_ref[0].astype(BF16)

    @pl.when(used)
    def _():
        xb = _unpack_bf16_pairs(x_ref[...]).astype(BF16)
        glu = jnp.dot(xb, wg_s[...], preferred_element_type=F32) + bg_ref[0]
        lin = jnp.dot(xb, wl_s[...], preferred_element_type=F32) + bl_ref[0]
        glu = jnp.minimum(glu, SWIGLU_LIMIT)
        lin = jnp.clip(lin, -SWIGLU_LIMIT, SWIGLU_LIMIT)
        act = glu * jax.nn.sigmoid(SWIGLU_ALPHA * glu) * (lin + 1.0)
        y = jnp.dot(act.astype(BF16), wd_s[...], preferred_element_type=F32) + bd_ref[0]
        y_ref[...] = _pack_bf16_pairs(y)

    @pl.when(jnp.logical_not(used))
    def _():
        y_ref[...] = jnp.zeros_like(y_ref)


def _experts(xbuf, blk_e, n_used, w_gate_up, b_gate_up, w_down, b_down, tb):
    n_rows = xbuf.shape[0]
    d = D_MODEL
    bgu = b_gate_up.reshape(N_EXPERTS, 1, 2 * D_FF)
    bd = b_down.reshape(N_EXPERTS, 1, d)
    grid_spec = pltpu.PrefetchScalarGridSpec(
        num_scalar_prefetch=2,
        grid=(n_rows // tb,),
        in_specs=[
            pl.BlockSpec((tb, HALF_D), lambda i, be, nu: (i, 0)),
            pl.BlockSpec((1, d, D_FF), lambda i, be, nu: (be[i], 0, 0)),
            pl.BlockSpec((1, d, D_FF), lambda i, be, nu: (be[i], 0, 1)),
            pl.BlockSpec((1, 1, D_FF), lambda i, be, nu: (be[i], 0, 0)),
            pl.BlockSpec((1, 1, D_FF), lambda i, be, nu: (be[i], 0, 1)),
            pl.BlockSpec((1, D_FF, d), lambda i, be, nu: (be[i], 0, 0)),
            pl.BlockSpec((1, 1, d), lambda i, be, nu: (be[i], 0, 0)),
        ],
        out_specs=pl.BlockSpec((tb, HALF_D), lambda i, be, nu: (i, 0)),
        scratch_shapes=[pltpu.VMEM((d, D_FF), BF16), pltpu.VMEM((d, D_FF), BF16),
                        pltpu.VMEM((D_FF, d), BF16)],
    )
    return pl.pallas_call(
        _expert_body,
        grid_spec=grid_spec,
        out_shape=jax.ShapeDtypeStruct((n_rows, HALF_D), jnp.uint32),
        compiler_params=_cparams(("arbitrary",)),
        name="moe_experts",
    )(blk_e, n_used, xbuf, w_gate_up, w_gate_up, bgu, bgu, w_down, bd)


def _combine_body(tbl_ref, tbl_next_ref, x2_ref, gate_ref, lp_ref, g_ref, ybuf_ref, o_ref, ys_ref,
                  sems, *, tm):
    i = pl.program_id(0)
    slot = i % 2
    n_local = _local_rows(tm)

    def copy(buf, local_row, global_row, size):
        return pltpu.make_async_copy(ybuf_ref.at[pl.ds(global_row, size)],
                                     ys_ref.at[buf, pl.ds(local_row, size)], sems.at[buf])

    def start_gather(table_ref, buf):
        ys_ref[buf, TOP_K * tm:, :] = jnp.zeros((n_local - TOP_K * tm, HALF_D), jnp.uint32)
        _for_each_run_chunk(table_ref, tm, lambda s, d, n: copy(buf, s, d, n).start())

    @pl.when(i == 0)
    def _():
        start_gather(tbl_ref, slot)

    @pl.when(i < pl.num_programs(0) - 1)
    def _():
        start_gather(tbl_next_ref, 1 - slot)

    _for_each_run_chunk(tbl_ref, tm, lambda s, d, n: copy(slot, s, d, n).wait())

    gate = gate_ref[...]
    lp = lp_ref[...]
    col = lax.broadcasted_iota(jnp.int32, (tm, n_local), 1)
    gmat = jnp.zeros((tm, n_local), F32)
    for r in range(TOP_K):
        gmat = gmat + jnp.where(col == lp[:, r:r + 1], gate[:, r:r + 1], 0.0)
    g_hi, g_lo = _split_bf16(gmat)
    ys = _unpack_bf16_pairs(ys_ref[slot]).astype(BF16)
    out = (x2_ref[...] + jnp.dot(g_hi, ys, preferred_element_type=F32)
           + jnp.dot(g_lo, ys, preferred_element_type=F32))
    o_ref[...] = out * lax.rsqrt(jnp.mean(out * out, axis=-1, keepdims=True) + EPS) * g_ref[...]


def _combine(x2, gate, lp, tbl, ybuf, final_norm_g, tm):
    n, d = x2.shape
    nblk = n // tm
    smem_tbl = lambda index_map: pl.BlockSpec((1, 1, LANES), index_map, memory_space=pltpu.SMEM)
    return pl.pallas_call(
        functools.partial(_combine_body, tm=tm),
        grid=(nblk,),
        in_specs=[smem_tbl(lambda i: (i, 0, 0)),
                  smem_tbl(lambda i: (jnp.minimum(i + 1, nblk - 1), 0, 0)),
                  pl.BlockSpec((tm, d), lambda i: (i, 0)),
                  pl.BlockSpec((tm, TOP_K), lambda i: (i, 0)),
                  pl.BlockSpec((tm, TOP_K), lambda i: (i, 0)),
                  pl.BlockSpec((1, d), lambda i: (0, 0)),
                  pl.BlockSpec(memory_space=pl.ANY)],
        out_specs=pl.BlockSpec((tm, d), lambda i: (i, 0)),
        out_shape=jax.ShapeDtypeStruct((n, d), F32),
        scratch_shapes=[pltpu.VMEM((2, _local_rows(tm), HALF_D), jnp.uint32),
                        pltpu.SemaphoreType.DMA((2,))],
        compiler_params=_cparams(("arbitrary",)),
        name="moe_combine",
    )(tbl, tbl, x2, gate, lp, final_norm_g.reshape(1, d), ybuf)


def _moe(x2, h2, logits, w_gate_up, b_gate_up, w_down, b_down, final_norm_g, tb, tm):
    n = x2.shape[0]
    nblk = n // tm
    max_rows = n * TOP_K + nblk * N_EXPERTS * (RUN_ALIGN - 1)
    n_blk = -(-max_rows // tb) + N_EXPERTS
    top_e, gate, lp, bcount, before, counts = _router(logits, tm)
    del top_e
    cnt = counts[0, :N_EXPERTS].astype(jnp.int32)
    padded = (cnt + tb - 1) // tb * tb
    pad_end = jnp.cumsum(padded)
    pad_start = pad_end - padded
    blk_first = jnp.arange(n_blk, dtype=jnp.int32) * tb
    blk_e = jnp.minimum(jnp.sum((pad_end[None, :] <= blk_first[:, None]).astype(jnp.int32), axis=1),
                        N_EXPERTS - 1)
    n_used = (pad_end[-1:] // tb).astype(jnp.int32)
    run_len = bcount[:, 0, :N_EXPERTS].astype(jnp.int32)
    run_src = jnp.cumsum(run_len, axis=1) - run_len
    run_dst = pad_start[None, :] + before[:, 0, :N_EXPERTS].astype(jnp.int32)
    tbl = jnp.concatenate([run_len, run_src, run_dst,
                           jnp.zeros((nblk, LANES - 3 * N_EXPERTS), jnp.int32)], axis=1)
    tbl = tbl.reshape(nblk, 1, LANES)
    lpt = lp.reshape(nblk, tm, TOP_K).transpose(0, 2, 1)
    xbuf = _dispatch(h2, lpt, tbl, n_blk * tb, tm)
    ybuf = _experts(xbuf, blk_e, n_used, w_gate_up, b_gate_up, w_down, b_down, tb)
    return _combine(x2, gate, lp, tbl, ybuf, final_norm_g, tm)


def kernel(x, positions, attn_norm_g, w_in, mla_q_norm_g, mla_w_uq, mla_kv_norm_g, mla_w_ukv,
           w_branch_mla, w_branch_dsa, w_out, ffn_norm_g, router_w, router_b,
           w_gate_up, b_gate_up, w_down, b_down, final_norm_g):
    b, s, d = x.shape
    n = b * s
    (qm, km, vm, qd, qi, kd, ki, vd, wi, ga, gb) = _in_proj(
        x.reshape(n, d), positions.reshape(n), attn_norm_g[0], w_in[0],
        mla_q_norm_g[0], mla_w_uq[0], mla_kv_norm_g[0], mla_w_ukv[0], tm=512)
    y_a = _mla_attention(qm, km, vm, b, s, tq=1024, tk=1024)
    y_b = _dsa_attention(qi, qd, ki, kd, vd, wi, b, s, tq=256, tk=1024, tka=1024)
    x2, h2, logits = _merge(
        x.reshape(n, d), y_a, y_b, ga, gb, w_branch_mla[0], w_branch_dsa[0], w_out[0],
        ffn_norm_g[0], router_w[0], router_b[0], tm=512)
    out = _moe(x2, h2, logits, w_gate_up[0], b_gate_up[0], w_down[0], b_down[0], final_norm_g,
               tb=512, tm=256)
    return out.reshape(b, s, d)
```

```python
import functools
import math

import jax
import jax.numpy as jnp
from jax import lax
from jax.experimental import pallas as pl
from jax.experimental.pallas import tpu as pltpu

F32 = jnp.float32
BF16 = jnp.bfloat16

LANES = 128
LOG2_E = 1.4426950408889634

D_MODEL = 1024
EPS = 1e-6
ROPE_THETA = 500000.0
MLA_HEADS = 8
MLA_Q_LORA = 256
MLA_KV_LORA = 128
MLA_NOPE = 64
MLA_ROPE = 32
MLA_V = 64
DSA_HEADS = 8
DSA_HEAD_DIM = 64
DSA_ROT = 16
IDX_HEADS = 8
IDX_DIM = 64
TOPK_MAX = 256
N_EXPERTS = 32
TOP_K = 4
D_FF = 1024
SWIGLU_LIMIT = 7.0
SWIGLU_ALPHA = 1.702

VMEM_LIMIT = 56 * 1024 * 1024

_OFF_CQ = 0
_OFF_CKV = _OFF_CQ + MLA_Q_LORA
_OFF_KPE = _OFF_CKV + MLA_KV_LORA
_OFF_QB = _OFF_KPE + MLA_ROPE
_OFF_KB = _OFF_QB + DSA_HEADS * DSA_HEAD_DIM
_OFF_VB = _OFF_KB + DSA_HEAD_DIM
_OFF_QI = _OFF_VB + DSA_HEAD_DIM
_OFF_KI = _OFF_QI + IDX_HEADS * IDX_DIM
_OFF_WI = _OFF_KI + IDX_DIM
_OFF_GA = _OFF_WI + IDX_HEADS
_OFF_GB = _OFF_GA + D_MODEL
_D_IN = _OFF_GB + D_MODEL


def _cparams(sem):
    return pltpu.CompilerParams(dimension_semantics=sem, vmem_limit_bytes=VMEM_LIMIT)


def _rope_group(xg, c, slo, shi, shift):
    return (xg * c + pltpu.roll(xg, LANES - shift, 1) * slo
            + pltpu.roll(xg, shift, 1) * shi)


def _in_proj_body(x_ref, g_ref, w1_ref, w2_ref, w3_ref, w5_ref, qng_ref, wuq_ref,
                  kvng_ref, wuk_ref, wuv_ref, em_ref, bm_ref, ed_ref, bd_ref, csm_ref, csd_ref,
                  qm_ref, km_ref, vm_ref, qd_ref, qi_ref, kd_ref, ki_ref, vd_ref,
                  wi_ref, ga_ref, gb_ref):
    x = x_ref[...]
    h = x * lax.rsqrt(jnp.mean(x * x, axis=-1, keepdims=True) + EPS) * g_ref[...]
    hb = h.astype(BF16)

    def spread(cs_ref, e_ref, b_ref):
        hi, lo = _split_bf16(cs_ref[...])
        t = (jnp.dot(hi, e_ref[...], preferred_element_type=F32)
             + jnp.dot(lo, e_ref[...], preferred_element_type=F32))
        return t[:, :LANES] + b_ref[...], t[:, LANES:2 * LANES], t[:, 2 * LANES:]

    cm, slm, shm = spread(csm_ref, em_ref, bm_ref)
    cd, sld, shd = spread(csd_ref, ed_ref, bd_ref)

    z1 = jnp.dot(hb, w1_ref[...], preferred_element_type=F32)
    cq = z1[:, 0:MLA_Q_LORA]
    cqn = cq * lax.rsqrt(jnp.mean(cq * cq, axis=-1, keepdims=True) + EPS) * qng_ref[...]
    q = jnp.dot(cqn.astype(BF16), wuq_ref[...], preferred_element_type=F32)
    q_scale = (MLA_NOPE + MLA_ROPE) ** -0.5 * LOG2_E
    for j in range(MLA_HEADS):
        qg = _rope_group(q[:, j * LANES:(j + 1) * LANES], cm, slm, shm, MLA_ROPE // 2)
        qm_ref[:, j * LANES:(j + 1) * LANES] = (qg * q_scale).astype(BF16)

    ckv = z1[:, MLA_Q_LORA:MLA_Q_LORA + MLA_KV_LORA]
    ckvn = (ckv * lax.rsqrt(jnp.mean(ckv * ckv, axis=-1, keepdims=True) + EPS)
            * kvng_ref[...]).astype(BF16)
    kpe = _rope_group(z1[:, MLA_Q_LORA + MLA_KV_LORA:], cm, slm, shm, MLA_ROPE // 2)
    kn = jnp.dot(ckvn, wuk_ref[...], preferred_element_type=F32)
    for j in range(MLA_HEADS):
        km_ref[:, j * LANES:(j + 1) * LANES] = (kn[:, j * LANES:(j + 1) * LANES] + kpe).astype(BF16)
    vm_ref[...] = jnp.dot(ckvn, wuv_ref[...], preferred_element_type=F32).astype(BF16)

    z2 = jnp.dot(hb, w2_ref[...], preferred_element_type=F32)
    d_scale = DSA_HEAD_DIM ** -0.5 * LOG2_E
    for j in range(DSA_HEADS // 2):
        g = _rope_group(z2[:, j * LANES:(j + 1) * LANES], cd, sld, shd, DSA_ROT // 2) * d_scale
        qd_ref[2 * j] = g[:, :DSA_HEAD_DIM].astype(BF16)
        qd_ref[2 * j + 1] = g[:, DSA_HEAD_DIM:].astype(BF16)
    base = DSA_HEADS * DSA_HEAD_DIM
    for j in range(IDX_HEADS // 2):
        g = _rope_group(z2[:, base + j * LANES:base + (j + 1) * LANES], cd, sld, shd, DSA_ROT // 2)
        qi_ref[2 * j] = g[:, :IDX_DIM].astype(BF16)
        qi_ref[2 * j + 1] = g[:, IDX_DIM:].astype(BF16)

    z3 = jnp.dot(hb, w3_ref[...], preferred_element_type=F32)
    kb = _rope_group(z3[:, :LANES], cd, sld, shd, DSA_ROT // 2)
    kd_ref[...] = kb[:, :DSA_HEAD_DIM].astype(BF16)
    ki_ref[...] = kb[:, DSA_HEAD_DIM:].astype(BF16)
    zv = z3[:, LANES:]
    vlane = lax.broadcasted_iota(jnp.int32, zv.shape, 1)
    vd_ref[...] = jnp.where(vlane < DSA_HEAD_DIM, zv,
                            jnp.where(vlane == DSA_HEAD_DIM, 1.0, 0.0)).astype(BF16)
    w_scale = IDX_HEADS ** -0.5 * IDX_DIM ** -0.5
    wi_ref[...] = z3[:, LANES + DSA_HEAD_DIM:LANES + DSA_HEAD_DIM + IDX_HEADS] * w_scale

    z5 = jnp.dot(hb, w5_ref[...], preferred_element_type=F32)
    ga_ref[...] = jax.nn.sigmoid(z5[:, :D_MODEL]).astype(BF16)
    gb_ref[...] = jax.nn.sigmoid(z5[:, D_MODEL:]).astype(BF16)


ROPE_COLS = 16


def _rope_tables(pos, rot_dim, lane_of_x1, period):
    half = rot_dim // 2
    inv_freq = ROPE_THETA ** (-jnp.arange(half, dtype=F32) / half)
    ang = pos.astype(F32)[:, None] * inv_freq
    pad = ((0, 0), (0, ROPE_COLS - half))
    cs = jnp.concatenate([jnp.pad(jnp.cos(ang), pad), jnp.pad(jnp.sin(ang), pad)], axis=1)

    lane = jnp.arange(LANES)
    in_period = lane % period - lane_of_x1
    freq = jnp.arange(ROPE_COLS)[:, None]
    on_x1 = (in_period[None, :] == freq) & (freq < half)
    on_x2 = (in_period[None, :] - half == freq) & (freq < half)
    zero = jnp.zeros((ROPE_COLS, LANES), F32)
    f = lambda m: m.astype(F32)
    spread = jnp.concatenate([
        jnp.concatenate([f(on_x1 | on_x2), zero, zero], axis=1),
        jnp.concatenate([zero, -f(on_x1), f(on_x2)], axis=1),
    ], axis=0).astype(BF16)
    rotated = (in_period >= 0) & (in_period < rot_dim)
    bias = jnp.where(rotated, 0.0, 1.0).astype(F32).reshape(1, LANES)
    return cs, spread, bias


def _head_cols(w, n_heads, widths, total):
    k = w.shape[0]
    per = sum(widths)
    w = w.reshape(k, n_heads, per)
    return jnp.pad(w, ((0, 0), (0, 0), (0, total - per))).reshape(k, n_heads * total)


def _in_proj(x2, pos, attn_norm_g, w_in, q_norm_g, w_uq, kv_norm_g, w_ukv, tm):
    n = x2.shape[0]
    d = D_MODEL
    zc = lambda k: jnp.zeros((d, k), F32)
    w1 = jnp.concatenate([w_in[:, _OFF_CQ:_OFF_KPE], zc(MLA_NOPE), w_in[:, _OFF_KPE:_OFF_QB],
                          zc(LANES - MLA_NOPE - MLA_ROPE)], axis=1).astype(BF16)
    w2 = jnp.concatenate([w_in[:, _OFF_QB:_OFF_KB], w_in[:, _OFF_QI:_OFF_KI]], axis=1).astype(BF16)
    w3 = jnp.concatenate([w_in[:, _OFF_KB:_OFF_VB], w_in[:, _OFF_KI:_OFF_WI],
                          w_in[:, _OFF_VB:_OFF_QI], w_in[:, _OFF_WI:_OFF_GA],
                          zc(LANES - DSA_HEAD_DIM - IDX_HEADS)], axis=1).astype(BF16)
    w5 = w_in[:, _OFF_GA:].astype(BF16)
    wuq = _head_cols(w_uq, MLA_HEADS, (MLA_NOPE, MLA_ROPE), LANES).astype(BF16)
    w_ukv3 = w_ukv.reshape(MLA_KV_LORA, MLA_HEADS, MLA_NOPE + MLA_V)
    wuk = jnp.pad(w_ukv3[:, :, :MLA_NOPE], ((0, 0), (0, 0), (0, LANES - MLA_NOPE))
                  ).reshape(MLA_KV_LORA, MLA_HEADS * LANES).astype(BF16)
    wuv = w_ukv3[:, :, MLA_NOPE:].reshape(MLA_KV_LORA, MLA_HEADS * MLA_V).astype(BF16)
    csm, em, bm = _rope_tables(pos, MLA_ROPE, MLA_NOPE, LANES)
    csd, ed, bdd = _rope_tables(pos, DSA_ROT, 0, DSA_HEAD_DIM)

    row = lambda w_: pl.BlockSpec((tm, w_), lambda i: (i, 0))
    full = lambda a: pl.BlockSpec(a.shape, lambda i: (0,) * a.ndim)
    hm = pl.BlockSpec((DSA_HEADS, tm, DSA_HEAD_DIM), lambda i: (0, i, 0))
    g2 = attn_norm_g.reshape(1, d)
    qng = q_norm_g.reshape(1, -1)
    kvng = kv_norm_g.reshape(1, -1)
    consts = (g2, w1, w2, w3, w5, qng, wuq, kvng, wuk, wuv, em, bm, ed, bdd)
    out_shape = (
        jax.ShapeDtypeStruct((n, MLA_HEADS * LANES), BF16),
        jax.ShapeDtypeStruct((n, MLA_HEADS * LANES), BF16),
        jax.ShapeDtypeStruct((n, MLA_HEADS * MLA_V), BF16),
        jax.ShapeDtypeStruct((DSA_HEADS, n, DSA_HEAD_DIM), BF16),
        jax.ShapeDtypeStruct((IDX_HEADS, n, IDX_DIM), BF16),
        jax.ShapeDtypeStruct((n, DSA_HEAD_DIM), BF16),
        jax.ShapeDtypeStruct((n, IDX_DIM), BF16),
        jax.ShapeDtypeStruct((n, LANES), BF16),
        jax.ShapeDtypeStruct((n, IDX_HEADS), F32),
        jax.ShapeDtypeStruct((n, D_MODEL), BF16),
        jax.ShapeDtypeStruct((n, D_MODEL), BF16),
    )
    out_specs = (row(MLA_HEADS * LANES), row(MLA_HEADS * LANES), row(MLA_HEADS * MLA_V),
                 hm, hm, row(DSA_HEAD_DIM), row(IDX_DIM), row(LANES), row(IDX_HEADS),
                 row(D_MODEL), row(D_MODEL))
    return pl.pallas_call(
        _in_proj_body,
        grid=(n // tm,),
        in_specs=[row(d)] + [full(a) for a in consts] + [row(2 * ROPE_COLS)] * 2,
        out_specs=out_specs,
        out_shape=out_shape,
        compiler_params=_cparams(("parallel",)),
        name="in_proj",
    )(x2, *consts, csm, csd)


NEG_BIG = -1e30
_NT = (((1,), (1,)), ((), ()))


def _mla_body(q_ref, k_ref, v_ref, o_ref, *, tq, tk):
    qi = pl.program_id(2)
    n_sub = tq // tk
    qs = [q_ref[:, hh * LANES:(hh + 1) * LANES] for hh in range(2)]

    def step(j, carry, masked):
        start = pl.multiple_of(j * tk, tk)
        vs = v_ref[pl.ds(start, tk), :]
        new = []
        for hh in range(2):
            m, l, acc = carry[hh]
            ks = k_ref[pl.ds(start, tk), hh * LANES:(hh + 1) * LANES]
            s = lax.dot_general(qs[hh], ks, _NT, preferred_element_type=F32)
            if masked:
                row = qi * tq + lax.broadcasted_iota(jnp.int32, (tq, tk), 0)
                col = j * tk + lax.broadcasted_iota(jnp.int32, (tq, tk), 1)
                s = jnp.where(col <= row, s, NEG_BIG)
            m_new = jnp.maximum(m, jnp.max(s, axis=-1, keepdims=True))
            alpha = jnp.exp2(m - m_new)
            p = jnp.exp2(s - m_new)
            l = alpha * l + jnp.sum(p, axis=-1, keepdims=True)
            acc = alpha * acc + jnp.dot(p.astype(BF16), vs, preferred_element_type=F32)
            new.append((m_new, l, acc))
        return tuple(new)

    one = (jnp.full((tq, 1), NEG_BIG, F32), jnp.zeros((tq, 1), F32), jnp.zeros((tq, LANES), F32))
    carry = lax.fori_loop(0, qi * n_sub, functools.partial(step, masked=False), (one, one))
    for dd in range(n_sub):
        carry = step(qi * n_sub + dd, carry, True)
    outs = [acc / l for (_, l, acc) in carry]
    lane = lax.broadcasted_iota(jnp.int32, (tq, LANES), 1)
    o_ref[...] = jnp.where(lane < MLA_V, outs[0], outs[1]).astype(BF16)


def _mla_attention(qm, km, vm, b, s, tq, tk):
    n = b * s
    nq = s // tq
    return pl.pallas_call(
        functools.partial(_mla_body, tq=tq, tk=tk),
        grid=(b, MLA_HEADS // 2, nq),
        in_specs=[
            pl.BlockSpec((tq, 2 * LANES), lambda bi, hp, qi: (bi * nq + qi, hp)),
            pl.BlockSpec((s, 2 * LANES), lambda bi, hp, qi: (bi, hp)),
            pl.BlockSpec((s, 2 * MLA_V), lambda bi, hp, qi: (bi, hp)),
        ],
        out_specs=pl.BlockSpec((tq, 2 * MLA_V), lambda bi, hp, qi: (bi * nq + qi, hp)),
        out_shape=jax.ShapeDtypeStruct((n, MLA_HEADS * MLA_V), BF16),
        compiler_params=_cparams(("parallel", "parallel", "arbitrary")),
        name="mla_attention",
    )(qm, km, vm)


INT_MIN = -2 ** 31
ACC_ROWS = 64
SEARCH_GROUPS = 256
SNAP_AFTER_TRIPS = 6
KEY_NEG_INF = (0xFF800000 - 2 ** 32) ^ 0x7FFFFFFF


def _sortable_key(score):
    bits = pltpu.bitcast(score, jnp.int32)
    return bits ^ ((bits >> 31) & 0x7FFFFFFF)


def _sortable_key_inverse(key):
    return pltpu.bitcast(key ^ ((key >> 31) & 0x7FFFFFFF), F32)


def _dsa_body(qi_ref, qd_ref, ki_ref, kd_ref, vd_ref, w_ref, o_ref, keys_ref, keyst_ref, *,
              tq, tk, tka, topk, s_len):
    qb = pl.program_id(1)
    n_tiles = (qb * tq + tq + tk - 1) // tk
    q_pos = qb * tq + lax.broadcasted_iota(jnp.int32, (tq, 1), 0)
    q_pos_t = qb * tq + lax.broadcasted_iota(jnp.int32, (1, tq), 1)

    qidx = qi_ref[...].reshape(IDX_HEADS * tq, IDX_DIM)
    w = w_ref[...]
    wcols = [jnp.broadcast_to(w[:, h:h + 1], (tq, tk)) for h in range(IDX_HEADS)]

    def score_tile(j, _):
        start = pl.multiple_of(j * tk, tk)
        kt = ki_ref[pl.ds(start, tk), :]
        sh = lax.dot_general(qidx, kt, _NT, preferred_element_type=F32)
        sc = jnp.zeros((tq, tk), F32)
        for h in range(IDX_HEADS):
            sc = sc + jnp.maximum(sh[h * tq:(h + 1) * tq], 0.0) * wcols[h]
        col = j * tk + lax.broadcasted_iota(jnp.int32, (tq, tk), 1)
        sc = jnp.where(col <= q_pos, sc, -jnp.inf)
        keys = _sortable_key(sc)
        keys_ref[:, pl.ds(start, tk)] = keys
        keyst_ref[pl.ds(start, tk), :] = keys.T
        return 0

    lax.fori_loop(0, n_tiles, score_tile, 0)

    def tile_t(j):
        return keyst_ref[pl.ds(pl.multiple_of(j * tk, tk), tk), :]

    def count_where(pred):
        def body(j, cnt):
            hit = pred(tile_t(j), j).astype(jnp.int32)
            return cnt + jnp.sum(hit.reshape(tk // ACC_ROWS, ACC_ROWS, tq), axis=0)
        cnt = lax.fori_loop(0, n_tiles, body, jnp.zeros((ACC_ROWS, tq), jnp.int32))
        return jnp.sum(cnt, axis=0, keepdims=True)

    def count_ge(cand):
        return count_where(lambda kt, j: kt >= cand)

    def max_le(bound):
        def body(j, mx):
            kt = tile_t(j)
            kt = jnp.where(kt <= bound, kt, INT_MIN)
            return jnp.maximum(mx, jnp.max(kt.reshape(tk // ACC_ROWS, ACC_ROWS, tq), axis=0))
        mx = lax.fori_loop(0, n_tiles, body, jnp.full((ACC_ROWS, tq), INT_MIN, jnp.int32))
        return jnp.max(mx, axis=0, keepdims=True)

    def group_max(j, g):
        return jnp.maximum(g, jnp.max(tile_t(j).reshape(tk // SEARCH_GROUPS, SEARCH_GROUPS, tq),
                                      axis=0))

    g = lax.fori_loop(0, n_tiles, group_max, jnp.full((SEARCH_GROUPS, tq), INT_MIN, jnp.int32))
    hi = jnp.max(g, axis=0, keepdims=True)
    lo = jnp.min(g, axis=0, keepdims=True)
    need = q_pos_t + 1 > topk
    lo = jnp.where(need, lo, KEY_NEG_INF + 1)
    hi = jnp.where(need, hi, KEY_NEG_INF + 1)

    def n_active(lo, hi, cnt_lo):
        return jnp.max(((lo < hi) & (cnt_lo != topk)).astype(jnp.int32))

    def halve_step(lo, hi, cnt_lo):
        mid = _sortable_key(0.5 * _sortable_key_inverse(lo) + 0.5 * _sortable_key_inverse(hi))
        mid = jnp.minimum(jnp.maximum(mid, lo + 1), hi)
        cnt = count_ge(mid)
        ok = cnt >= topk
        return jnp.where(ok, mid, lo), jnp.where(ok, hi, mid - 1), jnp.where(ok, cnt, cnt_lo)

    def snap_step(lo, hi, cnt_lo):
        mid = jnp.minimum(jnp.maximum(max_le(hi), lo + 1), hi)
        cnt = count_ge(mid)
        ok = cnt >= topk
        return jnp.where(ok, mid, lo), jnp.where(ok, mid, mid - 1), jnp.where(ok, cnt, cnt_lo)

    def narrow(carry):
        lo, hi, cnt_lo, trip, _ = carry
        lo, hi, cnt_lo = halve_step(lo, hi, cnt_lo)
        lo, hi, cnt_lo = lax.cond(trip >= SNAP_AFTER_TRIPS, snap_step, halve_step, lo, hi, cnt_lo)
        return lo, hi, cnt_lo, trip + 1, n_active(lo, hi, cnt_lo)

    cnt_lo = jnp.full((1, tq), -1, jnp.int32)
    thr, _, cnt_thr, _, _ = lax.while_loop(
        lambda c: c[4] > 0, narrow, (lo, hi, cnt_lo, jnp.int32(0), n_active(lo, hi, cnt_lo)))
    uncounted = jnp.logical_and(need, cnt_thr < 0)
    cnt_thr = lax.cond(jnp.max(uncounted.astype(jnp.int32)) > 0,
                       lambda: jnp.where(uncounted, count_ge(thr), cnt_thr), lambda: cnt_thr)
    thr = jnp.maximum(thr, KEY_NEG_INF + 1)

    tied = jnp.logical_and(need, cnt_thr > topk)
    keep_all = jnp.full((1, tq), s_len, jnp.int32)

    def tie_cut():
        n_keep = topk - count_ge(thr + 1)

        def step(_, carry):
            jlo, jhi = carry
            jm = (jlo + jhi) >> 1

            def pred(kt, j):
                kpos = j * tk + lax.broadcasted_iota(jnp.int32, (tk, tq), 0)
                return jnp.logical_and(kt == thr, kpos <= jm)

            ok = count_where(pred) >= n_keep
            return jnp.where(ok, jlo, jm + 1), jnp.where(ok, jm, jhi)

        _, jhi = lax.fori_loop(0, (s_len - 1).bit_length(), step,
                               (jnp.zeros((1, tq), jnp.int32), keep_all - 1))
        return jnp.where(tied, jhi, keep_all)

    jcut = lax.cond(jnp.max(tied.astype(jnp.int32)) > 0, tie_cut, lambda: keep_all)

    def to_column(v):
        return jnp.transpose(jnp.broadcast_to(v, (LANES, tq)))[:, 0:1]

    thr = to_column(thr)
    jcut = to_column(jcut)

    qd = qd_ref[...].reshape(DSA_HEADS * tq, DSA_HEAD_DIM)

    def attn_tile(j, carry):
        m, acc = carry
        start = pl.multiple_of(j * tka, tka)
        kt = kd_ref[pl.ds(start, tka), :]
        vt = vd_ref[pl.ds(start, tka), :]
        keys = keys_ref[:, pl.ds(start, tka)]
        col = j * tka + lax.broadcasted_iota(jnp.int32, (tq, tka), 1)
        sel = jnp.logical_or(keys > thr, jnp.logical_and(keys == thr, col <= jcut))
        s = lax.dot_general(qd, kt, _NT, preferred_element_type=F32)
        s = jnp.where(sel[None], s.reshape(DSA_HEADS, tq, tka), NEG_BIG)
        m_new = jnp.maximum(m, jnp.max(s, axis=-1, keepdims=True))
        p = jnp.exp2(s - m_new).astype(BF16)
        pv = jnp.dot(p.reshape(DSA_HEADS * tq, tka), vt, preferred_element_type=F32)
        acc = jnp.exp2(m - m_new) * acc + pv.reshape(DSA_HEADS, tq, LANES)
        return m_new, acc

    init = (jnp.full((DSA_HEADS, tq, 1), NEG_BIG, F32), jnp.zeros((DSA_HEADS, tq, LANES), F32))
    m, acc = lax.fori_loop(0, n_tiles * (tk // tka), attn_tile, init)
    for h in range(DSA_HEADS):
        out = acc[h, :, :DSA_HEAD_DIM] / acc[h, :, DSA_HEAD_DIM:DSA_HEAD_DIM + 1]
        o_ref[:, h * DSA_HEAD_DIM:(h + 1) * DSA_HEAD_DIM] = out.astype(BF16)


def _dsa_attention(qi, qd, ki, kd, vd, wi, b, s, tq, tk, tka):
    assert tk % tka == 0 and tk % SEARCH_GROUPS == 0 and SEARCH_GROUPS >= TOPK_MAX
    n = b * s
    nq = s // tq
    topk = min(TOPK_MAX, s // 4)
    hm = pl.BlockSpec((DSA_HEADS, tq, DSA_HEAD_DIM), lambda bi, qb: (0, bi * nq + qb, 0))
    kv = pl.BlockSpec((s, DSA_HEAD_DIM), lambda bi, qb: (bi, 0))
    return pl.pallas_call(
        functools.partial(_dsa_body, tq=tq, tk=tk, tka=tka, topk=topk, s_len=s),
        grid=(b, nq),
        in_specs=[hm, hm, kv, kv, pl.BlockSpec((s, LANES), lambda bi, qb: (bi, 0)),
                  pl.BlockSpec((tq, IDX_HEADS), lambda bi, qb: (bi * nq + qb, 0))],
        out_specs=pl.BlockSpec((tq, DSA_HEADS * DSA_HEAD_DIM), lambda bi, qb: (bi * nq + qb, 0)),
        out_shape=jax.ShapeDtypeStruct((n, DSA_HEADS * DSA_HEAD_DIM), BF16),
        scratch_shapes=[pltpu.VMEM((tq, s), jnp.int32), pltpu.VMEM((s, tq), jnp.int32)],
        compiler_params=_cparams(("parallel", "arbitrary")),
        name="dsa_attention",
    )(qi, qd, ki, kd, vd, wi)


HALF_D = D_MODEL // 2
RUN_ALIGN = 8


def _pack_bf16_pairs(y):
    r = pltpu.bitcast(y.astype(BF16).astype(F32), jnp.uint32)
    return r[:, :HALF_D] | (r[:, HALF_D:] >> 16)


def _unpack_bf16_pairs(p):
    hi = pltpu.bitcast(p & jnp.uint32(0xFFFF0000), F32)
    lo = pltpu.bitcast(p << 16, F32)
    return jnp.concatenate([hi, lo], axis=1)


def _split_bf16(a):
    hi = a.astype(BF16)
    lo = (a - hi.astype(F32)).astype(BF16)
    return hi, lo


def _merge_body(x_ref, ya_ref, yb_ref, ga_ref, gb_ref, wa_ref, wb_ref, wo_ref, g_ref,
                rwh_ref, rwl_ref, rb_ref, x2_ref, h_ref, logit_ref):
    ma = jnp.dot(ya_ref[...], wa_ref[...], preferred_element_type=F32)
    mb = jnp.dot(yb_ref[...], wb_ref[...], preferred_element_type=F32)
    merged = ga_ref[...].astype(F32) * ma + gb_ref[...].astype(F32) * mb
    x2 = x_ref[...] + jnp.dot(merged.astype(BF16), wo_ref[...], preferred_element_type=F32)
    x2_ref[...] = x2
    h = x2 * lax.rsqrt(jnp.mean(x2 * x2, axis=-1, keepdims=True) + EPS) * g_ref[...]
    hh, hl = _split_bf16(h)
    h_ref[...] = hh
    logit_ref[...] = (jnp.dot(hh, rwh_ref[...], preferred_element_type=F32)
                      + jnp.dot(hh, rwl_ref[...], preferred_element_type=F32)
                      + jnp.dot(hl, rwh_ref[...], preferred_element_type=F32)) + rb_ref[...]


def _merge(x2d, y_a, y_b, ga, gb, w_ba, w_bb, w_out, ffn_norm_g, router_w, router_b, tm):
    n, d = x2d.shape
    rw = jnp.pad(router_w, ((0, 0), (0, LANES - N_EXPERTS)))
    rwh = rw.astype(BF16)
    rwl = (rw - rwh.astype(F32)).astype(BF16)
    rb = jnp.pad(router_b, (0, LANES - N_EXPERTS), constant_values=NEG_BIG).reshape(1, LANES)
    consts = (w_ba.astype(BF16), w_bb.astype(BF16), w_out.astype(BF16), ffn_norm_g.reshape(1, d),
              rwh, rwl, rb)
    row = lambda w_: pl.BlockSpec((tm, w_), lambda i: (i, 0))
    full = lambda a: pl.BlockSpec(a.shape, lambda i: (0,) * a.ndim)
    out_shape = (
        jax.ShapeDtypeStruct((n, d), F32),
        jax.ShapeDtypeStruct((n, d), BF16),
        jax.ShapeDtypeStruct((n, LANES), F32),
    )
    return pl.pallas_call(
        _merge_body,
        grid=(n // tm,),
        in_specs=[row(d), row(HALF_D), row(HALF_D), row(d), row(d)] + [full(a) for a in consts],
        out_specs=(row(d), row(d), row(LANES)),
        out_shape=out_shape,
        compiler_params=_cparams(("parallel",)),
        name="merge",
    )(x2d, y_a, y_b, ga, gb, *consts)


def _router_body(logit_ref, upper_ref, e_ref, gate_ref, lp_ref, bc_ref, carry_out_ref, cnt_ref,
                 carry_ref, *, tm):
    i = pl.program_id(0)

    @pl.when(i == 0)
    def _():
        carry_ref[...] = jnp.zeros_like(carry_ref)

    lane = lax.broadcasted_iota(jnp.int32, (tm, LANES), 1)
    work = logit_ref[...]
    experts, vals = [], []
    onehot = jnp.zeros((tm, LANES), F32)
    for _ in range(TOP_K):
        mx = jnp.max(work, axis=-1, keepdims=True)
        idx = jnp.min(jnp.where(work == mx, lane, LANES), axis=-1, keepdims=True)
        hit = lane == idx
        experts.append(idx)
        vals.append(mx)
        onehot = onehot + hit.astype(F32)
        work = jnp.where(hit, -jnp.inf, work)
    ex = [jnp.exp(v - vals[0]) for v in vals]
    denom = ex[0] + ex[1] + ex[2] + ex[3]
    for r in range(TOP_K):
        e_ref[:, r:r + 1] = experts[r]
        gate_ref[:, r:r + 1] = ex[r] / denom

    rr = lax.broadcasted_iota(jnp.int32, (tm, tm), 0)
    cc = lax.broadcasted_iota(jnp.int32, (tm, tm), 1)
    lower = (cc < rr).astype(BF16)
    prefix = jnp.dot(lower, onehot.astype(BF16), preferred_element_type=F32)
    bc = jnp.sum(onehot, axis=0, keepdims=True)
    bc = jnp.floor((bc + (RUN_ALIGN - 1)) * (1.0 / RUN_ALIGN)) * RUN_ALIGN
    bc8 = jnp.broadcast_to(bc, (8, LANES))
    boff = jnp.dot(bc8.astype(BF16), upper_ref[...], preferred_element_type=F32)[0:1, :]
    local = prefix + boff
    for r in range(TOP_K):
        lp = jnp.sum(jnp.where(lane == experts[r], local, 0.0), axis=-1, keepdims=True)
        lp_ref[:, r:r + 1] = lp.astype(jnp.int32)
    bc_ref[0] = bc8
    carry_out_ref[0] = carry_ref[...]
    total = carry_ref[...] + bc8
    carry_ref[...] = total
    cnt_ref[...] = total


def _router(logits, tm):
    assert tm <= 256
    n = logits.shape[0]
    nblk = n // tm
    upper = (jnp.arange(LANES)[:, None] < jnp.arange(LANES)[None, :]).astype(BF16)
    row = lambda w_: pl.BlockSpec((tm, w_), lambda i: (i, 0))
    blk = pl.BlockSpec((1, 8, LANES), lambda i: (i, 0, 0))
    out_shape = (
        jax.ShapeDtypeStruct((n, TOP_K), jnp.int32),
        jax.ShapeDtypeStruct((n, TOP_K), F32),
        jax.ShapeDtypeStruct((n, TOP_K), jnp.int32),
        jax.ShapeDtypeStruct((nblk, 8, LANES), F32),
        jax.ShapeDtypeStruct((nblk, 8, LANES), F32),
        jax.ShapeDtypeStruct((8, LANES), F32),
    )
    return pl.pallas_call(
        functools.partial(_router_body, tm=tm),
        grid=(nblk,),
        in_specs=[row(LANES), pl.BlockSpec((LANES, LANES), lambda i: (0, 0))],
        out_specs=(row(TOP_K), row(TOP_K), row(TOP_K), blk, blk,
                   pl.BlockSpec((8, LANES), lambda i: (0, 0))),
        out_shape=out_shape,
        scratch_shapes=[pltpu.VMEM((8, LANES), F32)],
        compiler_params=_cparams(("arbitrary",)),
        name="router",
    )(logits, upper)


def _local_rows(tm):
    return TOP_K * tm + N_EXPERTS * RUN_ALIGN


def _for_each_run_chunk(tbl_ref, tm, fn):
    sizes = [s for s in (1 << k for k in range(tm.bit_length())) if RUN_ALIGN <= s <= tm]

    def per_expert(e, _):
        length = tbl_ref[0, 0, e]
        src = tbl_ref[0, 0, N_EXPERTS + e]
        dst = tbl_ref[0, 0, 2 * N_EXPERTS + e]
        for size in sizes:

            @pl.when((length & size) != 0)
            def _(size=size):
                off = length & (size - 1)
                fn(pl.multiple_of(src + off, RUN_ALIGN), pl.multiple_of(dst + off, RUN_ALIGN), size)
        return 0

    lax.fori_loop(0, N_EXPERTS, per_expert, 0)


def _dispatch_body(tbl_ref, tbl_prev_ref, lpt_ref, h_ref, xbuf_in_ref, xbuf_ref, sorted_ref, sems,
                   *, tm):
    del xbuf_in_ref
    i = pl.program_id(0)
    slot = i % 2
    h = h_ref[...]
    lpt = lpt_ref[0]
    chunk = tm
    for c in range(_local_rows(tm) // chunk):
        r_idx = c * chunk + lax.broadcasted_iota(jnp.int32, (chunk, tm), 0)
        hit = r_idx == lpt[0:1, :]
        for r in range(1, TOP_K):
            hit = jnp.logical_or(hit, r_idx == lpt[r:r + 1, :])
        sel = jnp.where(hit, 1.0, 0.0)
        rows = jnp.dot(sel.astype(BF16), h, preferred_element_type=F32)
        sorted_ref[slot, c * chunk:(c + 1) * chunk, :] = _pack_bf16_pairs(rows)

    def copy(buf, local_row, global_row, size):
        return pltpu.make_async_copy(sorted_ref.at[buf, pl.ds(local_row, size)],
                                     xbuf_ref.at[pl.ds(global_row, size)], sems.at[buf])

    _for_each_run_chunk(tbl_ref, tm, lambda s, d, n: copy(slot, s, d, n).start())

    @pl.when(i > 0)
    def _():
        _for_each_run_chunk(tbl_prev_ref, tm, lambda s, d, n: copy(1 - slot, s, d, n).wait())

    @pl.when(i == pl.num_programs(0) - 1)
    def _():
        _for_each_run_chunk(tbl_ref, tm, lambda s, d, n: copy(slot, s, d, n).wait())


def _dispatch(h2, lpt, tbl, n_rows, tm):
    n = h2.shape[0]
    xbuf0 = jnp.zeros((n_rows, HALF_D), jnp.uint32)
    smem_tbl = lambda index_map: pl.BlockSpec((1, 1, LANES), index_map, memory_space=pltpu.SMEM)
    return pl.pallas_call(
        functools.partial(_dispatch_body, tm=tm),
        grid=(n // tm,),
        in_specs=[smem_tbl(lambda i: (i, 0, 0)),
                  smem_tbl(lambda i: (jnp.maximum(i - 1, 0), 0, 0)),
                  pl.BlockSpec((1, TOP_K, tm), lambda i: (i, 0, 0)),
                  pl.BlockSpec((tm, D_MODEL), lambda i: (i, 0)),
                  pl.BlockSpec(memory_space=pl.ANY)],
        out_specs=pl.BlockSpec(memory_space=pl.ANY),
        out_shape=jax.ShapeDtypeStruct((n_rows, HALF_D), jnp.uint32),
        scratch_shapes=[pltpu.VMEM((2, _local_rows(tm), HALF_D), jnp.uint32),
                        pltpu.SemaphoreType.DMA((2,))],
        input_output_aliases={4: 0},
        compiler_params=_cparams(("arbitrary",)),
        name="moe_dispatch",
    )(tbl, tbl, lpt, h2, xbuf0)


def _expert_body(blk_e_ref, n_used_ref, x_ref, wg_ref, wl_ref, bg_ref, bl_ref, wd_ref, bd_ref,
                 y_ref, wg_s, wl_s, wd_s):
    i = pl.program_id(0)
    used = i < n_used_ref[0]

    changed = jnp.logical_or(i == 0, blk_e_ref[i] != blk_e_ref[jnp.maximum(i - 1, 0)])

    @pl.when(jnp.logical_and(used, changed))
    def _():
        wg_s[...] = wg_ref[0].astype(BF16)
        wl_s[...] = wl_ref[0].astype(BF16)
        wd_s[...] = wd_ref[0].astype(BF16)

    @pl.when(used)
    def _():
        xb = _unpack_bf16_pairs(x_ref[...]).astype(BF16)
        glu = jnp.dot(xb, wg_s[...], preferred_element_type=F32) + bg_ref[0]
        lin = jnp.dot(xb, wl_s[...], preferred_element_type=F32) + bl_ref[0]
        glu = jnp.minimum(glu, SWIGLU_LIMIT)
        lin = jnp.clip(lin, -SWIGLU_LIMIT, SWIGLU_LIMIT)
        act = glu * jax.nn.sigmoid(SWIGLU_ALPHA * glu) * (lin + 1.0)
        y = jnp.dot(act.astype(BF16), wd_s[...], preferred_element_type=F32) + bd_ref[0]
        y_ref[...] = _pack_bf16_pairs(y)

    @pl.when(jnp.logical_not(used))
    def _():
        y_ref[...] = jnp.zeros_like(y_ref)


def _experts(xbuf, blk_e, n_used, w_gate_up, b_gate_up, w_down, b_down, tb):
    n_rows = xbuf.shape[0]
    d = D_MODEL
    bgu = b_gate_up.reshape(N_EXPERTS, 1, 2 * D_FF)
    bd = b_down.reshape(N_EXPERTS, 1, d)
    grid_spec = pltpu.PrefetchScalarGridSpec(
        num_scalar_prefetch=2,
        grid=(n_rows // tb,),
        in_specs=[
            pl.BlockSpec((tb, HALF_D), lambda i, be, nu: (i, 0)),
            pl.BlockSpec((1, d, D_FF), lambda i, be, nu: (be[i], 0, 0)),
            pl.BlockSpec((1, d, D_FF), lambda i, be, nu: (be[i], 0, 1)),
            pl.BlockSpec((1, 1, D_FF), lambda i, be, nu: (be[i], 0, 0)),
            pl.BlockSpec((1, 1, D_FF), lambda i, be, nu: (be[i], 0, 1)),
            pl.BlockSpec((1, D_FF, d), lambda i, be, nu: (be[i], 0, 0)),
            pl.BlockSpec((1, 1, d), lambda i, be, nu: (be[i], 0, 0)),
        ],
        out_specs=pl.BlockSpec((tb, HALF_D), lambda i, be, nu: (i, 0)),
        scratch_shapes=[pltpu.VMEM((d, D_FF), BF16), pltpu.VMEM((d, D_FF), BF16),
                        pltpu.VMEM((D_FF, d), BF16)],
    )
    return pl.pallas_call(
        _expert_body,
        grid_spec=grid_spec,
        out_shape=jax.ShapeDtypeStruct((n_rows, HALF_D), jnp.uint32),
        compiler_params=_cparams(("arbitrary",)),
        name="moe_experts",
    )(blk_e, n_used, xbuf, w_gate_up, w_gate_up, bgu, bgu, w_down, bd)


def _combine_body(tbl_ref, tbl_next_ref, x2_ref, gate_ref, lp_ref, g_ref, ybuf_ref, o_ref, ys_ref,
                  sems, *, tm):
    i = pl.program_id(0)
    slot = i % 2
    n_local = _local_rows(tm)

    def copy(buf, local_row, global_row, size):
        return pltpu.make_async_copy(ybuf_ref.at[pl.ds(global_row, size)],
                                     ys_ref.at[buf, pl.ds(local_row, size)], sems.at[buf])

    def start_gather(table_ref, buf):
        ys_ref[buf, TOP_K * tm:, :] = jnp.zeros((n_local - TOP_K * tm, HALF_D), jnp.uint32)
        _for_each_run_chunk(table_ref, tm, lambda s, d, n: copy(buf, s, d, n).start())

    @pl.when(i == 0)
    def _():
        start_gather(tbl_ref, slot)

    @pl.when(i < pl.num_programs(0) - 1)
    def _():
        start_gather(tbl_next_ref, 1 - slot)

    _for_each_run_chunk(tbl_ref, tm, lambda s, d, n: copy(slot, s, d, n).wait())

    gate = gate_ref[...]
    lp = lp_ref[...]
    col = lax.broadcasted_iota(jnp.int32, (tm, n_local), 1)
    gmat = jnp.zeros((tm, n_local), F32)
    for r in range(TOP_K):
        gmat = gmat + jnp.where(col == lp[:, r:r + 1], gate[:, r:r + 1], 0.0)
    g_hi, g_lo = _split_bf16(gmat)
    ys = _unpack_bf16_pairs(ys_ref[slot]).astype(BF16)
    out = (x2_ref[...] + jnp.dot(g_hi, ys, preferred_element_type=F32)
           + jnp.dot(g_lo, ys, preferred_element_type=F32))
    o_ref[...] = out * lax.rsqrt(jnp.mean(out * out, axis=-1, keepdims=True) + EPS) * g_ref[...]


def _combine(x2, gate, lp, tbl, ybuf, final_norm_g, tm):
    n, d = x2.shape
    nblk = n // tm
    smem_tbl = lambda index_map: pl.BlockSpec((1, 1, LANES), index_map, memory_space=pltpu.SMEM)
    return pl.pallas_call(
        functools.partial(_combine_body, tm=tm),
        grid=(nblk,),
        in_specs=[smem_tbl(lambda i: (i, 0, 0)),
                  smem_tbl(lambda i: (jnp.minimum(i + 1, nblk - 1), 0, 0)),
                  pl.BlockSpec((tm, d), lambda i: (i, 0)),
                  pl.BlockSpec((tm, TOP_K), lambda i: (i, 0)),
                  pl.BlockSpec((tm, TOP_K), lambda i: (i, 0)),
                  pl.BlockSpec((1, d), lambda i: (0, 0)),
                  pl.BlockSpec(memory_space=pl.ANY)],
        out_specs=pl.BlockSpec((tm, d), lambda i: (i, 0)),
        out_shape=jax.ShapeDtypeStruct((n, d), F32),
        scratch_shapes=[pltpu.VMEM((2, _local_rows(tm), HALF_D), jnp.uint32),
                        pltpu.SemaphoreType.DMA((2,))],
        compiler_params=_cparams(("arbitrary",)),
        name="moe_combine",
    )(tbl, tbl, x2, gate, lp, final_norm_g.reshape(1, d), ybuf)


def _moe(x2, h2, logits, w_gate_up, b_gate_up, w_down, b_down, final_norm_g, tb, tm):
    n = x2.shape[0]
    nblk = n // tm
    max_rows = n * TOP_K + nblk * N_EXPERTS * (RUN_ALIGN - 1)
    n_blk = -(-max_rows // tb) + N_EXPERTS
    top_e, gate, lp, bcount, before, counts = _router(logits, tm)
    del top_e
    cnt = counts[0, :N_EXPERTS].astype(jnp.int32)
    padded = (cnt + tb - 1) // tb * tb
    pad_end = jnp.cumsum(padded)
    pad_start = pad_end - padded
    blk_first = jnp.arange(n_blk, dtype=jnp.int32) * tb
    blk_e = jnp.minimum(jnp.sum((pad_end[None, :] <= blk_first[:, None]).astype(jnp.int32), axis=1),
                        N_EXPERTS - 1)
    n_used = (pad_end[-1:] // tb).astype(jnp.int32)
    run_len = bcount[:, 0, :N_EXPERTS].astype(jnp.int32)
    run_src = jnp.cumsum(run_len, axis=1) - run_len
    run_dst = pad_start[None, :] + before[:, 0, :N_EXPERTS].astype(jnp.int32)
    tbl = jnp.concatenate([run_len, run_src, run_dst,
                           jnp.zeros((nblk, LANES - 3 * N_EXPERTS), jnp.int32)], axis=1)
    tbl = tbl.reshape(nblk, 1, LANES)
    lpt = lp.reshape(nblk, tm, TOP_K).transpose(0, 2, 1)
    xbuf = _dispatch(h2, lpt, tbl, n_blk * tb, tm)
    ybuf = _experts(xbuf, blk_e, n_used, w_gate_up, b_gate_up, w_down, b_down, tb)
    return _combine(x2, gate, lp, tbl, ybuf, final_norm_g, tm)


def kernel(x, positions, attn_norm_g, w_in, mla_q_norm_g, mla_w_uq, mla_kv_norm_g, mla_w_ukv,
           w_branch_mla, w_branch_dsa, w_out, ffn_norm_g, router_w, router_b,
           w_gate_up, b_gate_up, w_down, b_down, final_norm_g):
    b, s, d = x.shape
    n = b * s
    (qm, km, vm, qd, qi, kd, ki, vd, wi, ga, gb) = _in_proj(
        x.reshape(n, d), positions.reshape(n), attn_norm_g[0], w_in[0],
        mla_q_norm_g[0], mla_w_uq[0], mla_kv_norm_g[0], mla_w_ukv[0], tm=512)
    y_a = _mla_attention(qm, km, vm, b, s, tq=1024, tk=1024)
    y_b = _dsa_attention(qi, qd, ki, kd, vd, wi, b, s, tq=256, tk=1024, tka=1024)
    x2, h2, logits = _merge(
        x.reshape(n, d), y_a, y_b, ga, gb, w_branch_mla[0], w_branch_dsa[0], w_out[0],
        ffn_norm_g[0], router_w[0], router_b[0], tm=512)
    out = _moe(x2, h2, logits, w_gate_up[0], b_gate_up[0], w_down[0], b_down[0], final_norm_g,
               tb=512, tm=256)
    return out.reshape(b, s, d)
```

```python
import functools
import math

import jax
import jax.numpy as jnp
from jax import lax
from jax.experimental import pallas as pl
from jax.experimental.pallas import tpu as pltpu

F32 = jnp.float32
BF16 = jnp.bfloat16

LANES = 128
LOG2_E = 1.4426950408889634

D_MODEL = 1024
EPS = 1e-6
ROPE_THETA = 500000.0
MLA_HEADS = 8
MLA_Q_LORA = 256
MLA_KV_LORA = 128
MLA_NOPE = 64
MLA_ROPE = 32
MLA_V = 64
DSA_HEADS = 8
DSA_HEAD_DIM = 64
DSA_ROT = 16
IDX_HEADS = 8
IDX_DIM = 64
TOPK_MAX = 256
N_EXPERTS = 32
TOP_K = 4
D_FF = 1024
SWIGLU_LIMIT = 7.0
SWIGLU_ALPHA = 1.702

VMEM_LIMIT = 56 * 1024 * 1024

_OFF_CQ = 0
_OFF_CKV = _OFF_CQ + MLA_Q_LORA
_OFF_KPE = _OFF_CKV + MLA_KV_LORA
_OFF_QB = _OFF_KPE + MLA_ROPE
_OFF_KB = _OFF_QB + DSA_HEADS * DSA_HEAD_DIM
_OFF_VB = _OFF_KB + DSA_HEAD_DIM
_OFF_QI = _OFF_VB + DSA_HEAD_DIM
_OFF_KI = _OFF_QI + IDX_HEADS * IDX_DIM
_OFF_WI = _OFF_KI + IDX_DIM
_OFF_GA = _OFF_WI + IDX_HEADS
_OFF_GB = _OFF_GA + D_MODEL
_D_IN = _OFF_GB + D_MODEL


def _cparams(sem):
    return pltpu.CompilerParams(dimension_semantics=sem, vmem_limit_bytes=VMEM_LIMIT)


def _rope_group(xg, c, slo, shi, shift):
    return (xg * c + pltpu.roll(xg, LANES - shift, 1) * slo
            + pltpu.roll(xg, shift, 1) * shi)


def _in_proj_body(x_ref, g_ref, w1_ref, w2_ref, w3_ref, w5_ref, qng_ref, wuq_ref,
                  kvng_ref, wuk_ref, wuv_ref, em_ref, bm_ref, ed_ref, bd_ref, csm_ref, csd_ref,
                  qm_ref, km_ref, vm_ref, qd_ref, qi_ref, kd_ref, ki_ref, vd_ref,
                  wi_ref, ga_ref, gb_ref):
    x = x_ref[...]
    h = x * lax.rsqrt(jnp.mean(x * x, axis=-1, keepdims=True) + EPS) * g_ref[...]
    hb = h.astype(BF16)

    def spread(cs_ref, e_ref, b_ref):
        hi, lo = _split_bf16(cs_ref[...])
        t = (jnp.dot(hi, e_ref[...], preferred_element_type=F32)
             + jnp.dot(lo, e_ref[...], preferred_element_type=F32))
        return t[:, :LANES] + b_ref[...], t[:, LANES:2 * LANES], t[:, 2 * LANES:]

    cm, slm, shm = spread(csm_ref, em_ref, bm_ref)
    cd, sld, shd = spread(csd_ref, ed_ref, bd_ref)

    z1 = jnp.dot(hb, w1_ref[...], preferred_element_type=F32)
    cq = z1[:, 0:MLA_Q_LORA]
    cqn = cq * lax.rsqrt(jnp.mean(cq * cq, axis=-1, keepdims=True) + EPS) * qng_ref[...]
    q = jnp.dot(cqn.astype(BF16), wuq_ref[...], preferred_element_type=F32)
    q_scale = (MLA_NOPE + MLA_ROPE) ** -0.5 * LOG2_E
    for j in range(MLA_HEADS):
        qg = _rope_group(q[:, j * LANES:(j + 1) * LANES], cm, slm, shm, MLA_ROPE // 2)
        qm_ref[:, j * LANES:(j + 1) * LANES] = (qg * q_scale).astype(BF16)

    ckv = z1[:, MLA_Q_LORA:MLA_Q_LORA + MLA_KV_LORA]
    ckvn = (ckv * lax.rsqrt(jnp.mean(ckv * ckv, axis=-1, keepdims=True) + EPS)
            * kvng_ref[...]).astype(BF16)
    kpe = _rope_group(z1[:, MLA_Q_LORA + MLA_KV_LORA:], cm, slm, shm, MLA_ROPE // 2)
    kn = jnp.dot(ckvn, wuk_ref[...], preferred_element_type=F32)
    for j in range(MLA_HEADS):
        km_ref[:, j * LANES:(j + 1) * LANES] = (kn[:, j * LANES:(j + 1) * LANES] + kpe).astype(BF16)
    vm_ref[...] = jnp.dot(ckvn, wuv_ref[...], preferred_element_type=F32).astype(BF16)

    z2 = jnp.dot(hb, w2_ref[...], preferred_element_type=F32)
    d_scale = DSA_HEAD_DIM ** -0.5 * LOG2_E
    for j in range(DSA_HEADS // 2):
        g = _rope_group(z2[:, j * LANES:(j + 1) * LANES], cd, sld, shd, DSA_ROT // 2) * d_scale
        qd_ref[2 * j] = g[:, :DSA_HEAD_DIM].astype(BF16)
        qd_ref[2 * j + 1] = g[:, DSA_HEAD_DIM:].astype(BF16)
    base = DSA_HEADS * DSA_HEAD_DIM
    for j in range(IDX_HEADS // 2):
        g = _rope_group(z2[:, base + j * LANES:base + (j + 1) * LANES], cd, sld, shd, DSA_ROT // 2)
        qi_ref[2 * j] = g[:, :IDX_DIM].astype(BF16)
        qi_ref[2 * j + 1] = g[:, IDX_DIM:].astype(BF16)

    z3 = jnp.dot(hb, w3_ref[...], preferred_element_type=F32)
    kb = _rope_group(z3[:, :LANES], cd, sld, shd, DSA_ROT // 2)
    kd_ref[...] = kb[:, :DSA_HEAD_DIM].astype(BF16)
    ki_ref[...] = kb[:, DSA_HEAD_DIM:].astype(BF16)
    zv = z3[:, LANES:]
    vlane = lax.broadcasted_iota(jnp.int32, zv.shape, 1)
    vd_ref[...] = jnp.where(vlane < DSA_HEAD_DIM, zv,
                            jnp.where(vlane == DSA_HEAD_DIM, 1.0, 0.0)).astype(BF16)
    w_scale = IDX_HEADS ** -0.5 * IDX_DIM ** -0.5
    wi_ref[...] = z3[:, LANES + DSA_HEAD_DIM:LANES + DSA_HEAD_DIM + IDX_HEADS] * w_scale

    z5 = jnp.dot(hb, w5_ref[...], preferred_element_type=F32)
    ga_ref[...] = jax.nn.sigmoid(z5[:, :D_MODEL]).astype(BF16)
    gb_ref[...] = jax.nn.sigmoid(z5[:, D_MODEL:]).astype(BF16)


ROPE_COLS = 16


def _rope_tables(pos, rot_dim, lane_of_x1, period):
    half = rot_dim // 2
    inv_freq = ROPE_THETA ** (-jnp.arange(half, dtype=F32) / half)
    ang = pos.astype(F32)[:, None] * inv_freq
    pad = ((0, 0), (0, ROPE_COLS - half))
    cs = jnp.concatenate([jnp.pad(jnp.cos(ang), pad), jnp.pad(jnp.sin(ang), pad)], axis=1)

    lane = jnp.arange(LANES)
    in_period = lane % period - lane_of_x1
    freq = jnp.arange(ROPE_COLS)[:, None]
    on_x1 = (in_period[None, :] == freq) & (freq < half)
    on_x2 = (in_period[None, :] - half == freq) & (freq < half)
    zero = jnp.zeros((ROPE_COLS, LANES), F32)
    f = lambda m: m.astype(F32)
    spread = jnp.concatenate([
        jnp.concatenate([f(on_x1 | on_x2), zero, zero], axis=1),
        jnp.concatenate([zero, -f(on_x1), f(on_x2)], axis=1),
    ], axis=0).astype(BF16)
    rotated = (in_period >= 0) & (in_period < rot_dim)
    bias = jnp.where(rotated, 0.0, 1.0).astype(F32).reshape(1, LANES)
    return cs, spread, bias


def _head_cols(w, n_heads, widths, total):
    k = w.shape[0]
    per = sum(widths)
    w = w.reshape(k, n_heads, per)
    return jnp.pad(w, ((0, 0), (0, 0), (0, total - per))).reshape(k, n_heads * total)


def _in_proj(x2, pos, attn_norm_g, w_in, q_norm_g, w_uq, kv_norm_g, w_ukv, tm):
    n = x2.shape[0]
    d = D_MODEL
    zc = lambda k: jnp.zeros((d, k), F32)
    w1 = jnp.concatenate([w_in[:, _OFF_CQ:_OFF_KPE], zc(MLA_NOPE), w_in[:, _OFF_KPE:_OFF_QB],
                          zc(LANES - MLA_NOPE - MLA_ROPE)], axis=1).astype(BF16)
    w2 = jnp.concatenate([w_in[:, _OFF_QB:_OFF_KB], w_in[:, _OFF_QI:_OFF_KI]], axis=1).astype(BF16)
    w3 = jnp.concatenate([w_in[:, _OFF_KB:_OFF_VB], w_in[:, _OFF_KI:_OFF_WI],
                          w_in[:, _OFF_VB:_OFF_QI], w_in[:, _OFF_WI:_OFF_GA],
                          zc(LANES - DSA_HEAD_DIM - IDX_HEADS)], axis=1).astype(BF16)
    w5 = w_in[:, _OFF_GA:].astype(BF16)
    wuq = _head_cols(w_uq, MLA_HEADS, (MLA_NOPE, MLA_ROPE), LANES).astype(BF16)
    w_ukv3 = w_ukv.reshape(MLA_KV_LORA, MLA_HEADS, MLA_NOPE + MLA_V)
    wuk = jnp.pad(w_ukv3[:, :, :MLA_NOPE], ((0, 0), (0, 0), (0, LANES - MLA_NOPE))
                  ).reshape(MLA_KV_LORA, MLA_HEADS * LANES).astype(BF16)
    wuv = w_ukv3[:, :, MLA_NOPE:].reshape(MLA_KV_LORA, MLA_HEADS * MLA_V).astype(BF16)
    csm, em, bm = _rope_tables(pos, MLA_ROPE, MLA_NOPE, LANES)
    csd, ed, bdd = _rope_tables(pos, DSA_ROT, 0, DSA_HEAD_DIM)

    row = lambda w_: pl.BlockSpec((tm, w_), lambda i: (i, 0))
    full = lambda a: pl.BlockSpec(a.shape, lambda i: (0,) * a.ndim)
    hm = pl.BlockSpec((DSA_HEADS, tm, DSA_HEAD_DIM), lambda i: (0, i, 0))
    g2 = attn_norm_g.reshape(1, d)
    qng = q_norm_g.reshape(1, -1)
    kvng = kv_norm_g.reshape(1, -1)
    consts = (g2, w1, w2, w3, w5, qng, wuq, kvng, wuk, wuv, em, bm, ed, bdd)
    out_shape = (
        jax.ShapeDtypeStruct((n, MLA_HEADS * LANES), BF16),
        jax.ShapeDtypeStruct((n, MLA_HEADS * LANES), BF16),
        jax.ShapeDtypeStruct((n, MLA_HEADS * MLA_V), BF16),
        jax.ShapeDtypeStruct((DSA_HEADS, n, DSA_HEAD_DIM), BF16),
        jax.ShapeDtypeStruct((IDX_HEADS, n, IDX_DIM), BF16),
        jax.ShapeDtypeStruct((n, DSA_HEAD_DIM), BF16),
        jax.ShapeDtypeStruct((n, IDX_DIM), BF16),
        jax.ShapeDtypeStruct((n, LANES), BF16),
        jax.ShapeDtypeStruct((n, IDX_HEADS), F32),
        jax.ShapeDtypeStruct((n, D_MODEL), BF16),
        jax.ShapeDtypeStruct((n, D_MODEL), BF16),
    )
    out_specs = (row(MLA_HEADS * LANES), row(MLA_HEADS * LANES), row(MLA_HEADS * MLA_V),
                 hm, hm, row(DSA_HEAD_DIM), row(IDX_DIM), row(LANES), row(IDX_HEADS),
                 row(D_MODEL), row(D_MODEL))
    return pl.pallas_call(
        _in_proj_body,
        grid=(n // tm,),
        in_specs=[row(d)] + [full(a) for a in consts] + [row(2 * ROPE_COLS)] * 2,
        out_specs=out_specs,
        out_shape=out_shape,
        compiler_params=_cparams(("parallel",)),
        name="in_proj",
    )(x2, *consts, csm, csd)


NEG_BIG = -1e30
_NT = (((1,), (1,)), ((), ()))


def _mla_body(q_ref, k_ref, v_ref, o_ref, *, tq, tk):
    qi = pl.program_id(2)
    n_sub = tq // tk
    qs = [q_ref[:, hh * LANES:(hh + 1) * LANES] for hh in range(2)]

    def step(j, carry, masked):
        start = pl.multiple_of(j * tk, tk)
        vs = v_ref[pl.ds(start, tk), :]
        new = []
        for hh in range(2):
            m, l, acc = carry[hh]
            ks = k_ref[pl.ds(start, tk), hh * LANES:(hh + 1) * LANES]
            s = lax.dot_general(qs[hh], ks, _NT, preferred_element_type=F32)
            if masked:
                row = qi * tq + lax.broadcasted_iota(jnp.int32, (tq, tk), 0)
                col = j * tk + lax.broadcasted_iota(jnp.int32, (tq, tk), 1)
                s = jnp.where(col <= row, s, NEG_BIG)
            m_new = jnp.maximum(m, jnp.max(s, axis=-1, keepdims=True))
            alpha = jnp.exp2(m - m_new)
            p = jnp.exp2(s - m_new)
            l = alpha * l + jnp.sum(p, axis=-1, keepdims=True)
            acc = alpha * acc + jnp.dot(p.astype(BF16), vs, preferred_element_type=F32)
            new.append((m_new, l, acc))
        return tuple(new)

    one = (jnp.full((tq, 1), NEG_BIG, F32), jnp.zeros((tq, 1), F32), jnp.zeros((tq, LANES), F32))
    carry = lax.fori_loop(0, qi * n_sub, functools.partial(step, masked=False), (one, one))
    for dd in range(n_sub):
        carry = step(qi * n_sub + dd, carry, True)
    outs = [acc / l for (_, l, acc) in carry]
    lane = lax.broadcasted_iota(jnp.int32, (tq, LANES), 1)
    o_ref[...] = jnp.where(lane < MLA_V, outs[0], outs[1]).astype(BF16)


def _mla_attention(qm, km, vm, b, s, tq, tk):
    n = b * s
    nq = s // tq
    return pl.pallas_call(
        functools.partial(_mla_body, tq=tq, tk=tk),
        grid=(b, MLA_HEADS // 2, nq),
        in_specs=[
            pl.BlockSpec((tq, 2 * LANES), lambda bi, hp, qi: (bi * nq + qi, hp)),
            pl.BlockSpec((s, 2 * LANES), lambda bi, hp, qi: (bi, hp)),
            pl.BlockSpec((s, 2 * MLA_V), lambda bi, hp, qi: (bi, hp)),
        ],
        out_specs=pl.BlockSpec((tq, 2 * MLA_V), lambda bi, hp, qi: (bi * nq + qi, hp)),
        out_shape=jax.ShapeDtypeStruct((n, MLA_HEADS * MLA_V), BF16),
        compiler_params=_cparams(("parallel", "parallel", "arbitrary")),
        name="mla_attention",
    )(qm, km, vm)


INT_MIN = -2 ** 31
ACC_ROWS = 64
SEARCH_GROUPS = 256
SNAP_AFTER_TRIPS = 6
KEY_NEG_INF = (0xFF800000 - 2 ** 32) ^ 0x7FFFFFFF


def _sortable_key(score):
    bits = pltpu.bitcast(score, jnp.int32)
    return bits ^ ((bits >> 31) & 0x7FFFFFFF)


def _sortable_key_inverse(key):
    return pltpu.bitcast(key ^ ((key >> 31) & 0x7FFFFFFF), F32)


def _dsa_body(qi_ref, qd_ref, ki_ref, kd_ref, vd_ref, w_ref, o_ref, keys_ref, keyst_ref, *,
              tq, tk, tka, topk, s_len):
    qb = pl.program_id(1)
    n_tiles = (qb * tq + tq + tk - 1) // tk
    q_pos = qb * tq + lax.broadcasted_iota(jnp.int32, (tq, 1), 0)
    q_pos_t = qb * tq + lax.broadcasted_iota(jnp.int32, (1, tq), 1)

    qidx = qi_ref[...].reshape(IDX_HEADS * tq, IDX_DIM)
    w = w_ref[...]
    wcols = [jnp.broadcast_to(w[:, h:h + 1], (tq, tk)) for h in range(IDX_HEADS)]

    def score_tile(j, _):
        start = pl.multiple_of(j * tk, tk)
        kt = ki_ref[pl.ds(start, tk), :]
        sh = lax.dot_general(qidx, kt, _NT, preferred_element_type=F32)
        sc = jnp.zeros((tq, tk), F32)
        for h in range(IDX_HEADS):
            sc = sc + jnp.maximum(sh[h * tq:(h + 1) * tq], 0.0) * wcols[h]
        col = j * tk + lax.broadcasted_iota(jnp.int32, (tq, tk), 1)
        sc = jnp.where(col <= q_pos, sc, -jnp.inf)
        keys = _sortable_key(sc)
        keys_ref[:, pl.ds(start, tk)] = keys
        keyst_ref[pl.ds(start, tk), :] = keys.T
        return 0

    lax.fori_loop(0, n_tiles, score_tile, 0)

    def tile_t(j):
        return keyst_ref[pl.ds(pl.multiple_of(j * tk, tk), tk), :]

    def count_where(pred):
        def body(j, cnt):
            hit = pred(tile_t(j), j).astype(jnp.int32)
            return cnt + jnp.sum(hit.reshape(tk // ACC_ROWS, ACC_ROWS, tq), axis=0)
        cnt = lax.fori_loop(0, n_tiles, body, jnp.zeros((ACC_ROWS, tq), jnp.int32))
        return jnp.sum(cnt, axis=0, keepdims=True)

    def count_ge(cand):
        return count_where(lambda kt, j: kt >= cand)

    def max_le(bound):
        def body(j, mx):
            kt = tile_t(j)
            kt = jnp.where(kt <= bound, kt, INT_MIN)
            return jnp.maximum(mx, jnp.max(kt.reshape(tk // ACC_ROWS, ACC_ROWS, tq), axis=0))
        mx = lax.fori_loop(0, n_tiles, body, jnp.full((ACC_ROWS, tq), INT_MIN, jnp.int32))
        return jnp.max(mx, axis=0, keepdims=True)

    def group_max(j, g):
        return jnp.maximum(g, jnp.max(tile_t(j).reshape(tk // SEARCH_GROUPS, SEARCH_GROUPS, tq),
                                      axis=0))

    g = lax.fori_loop(0, n_tiles, group_max, jnp.full((SEARCH_GROUPS, tq), INT_MIN, jnp.int32))
    hi = jnp.max(g, axis=0, keepdims=True)
    lo = jnp.min(g, axis=0, keepdims=True)
    need = q_pos_t + 1 > topk
    lo = jnp.where(need, lo, KEY_NEG_INF + 1)
    hi = jnp.where(need, hi, KEY_NEG_INF + 1)

    def n_active(lo, hi, cnt_lo):
        return jnp.max(((lo < hi) & (cnt_lo != topk)).astype(jnp.int32))

    def halve_step(lo, hi, cnt_lo):
        mid = _sortable_key(0.5 * _sortable_key_inverse(lo) + 0.5 * _sortable_key_inverse(hi))
        mid = jnp.minimum(jnp.maximum(mid, lo + 1), hi)
        cnt = count_ge(mid)
        ok = cnt >= topk
        return jnp.where(ok, mid, lo), jnp.where(ok, hi, mid - 1), jnp.where(ok, cnt, cnt_lo)

    def snap_step(lo, hi, cnt_lo):
        mid = jnp.minimum(jnp.maximum(max_le(hi), lo + 1), hi)
        cnt = count_ge(mid)
        ok = cnt >= topk
        return jnp.where(ok, mid, lo), jnp.where(ok, mid, mid - 1), jnp.where(ok, cnt, cnt_lo)

    def narrow(carry):
        lo, hi, cnt_lo, trip, _ = carry
        lo, hi, cnt_lo = halve_step(lo, hi, cnt_lo)
        lo, hi, cnt_lo = lax.cond(trip >= SNAP_AFTER_TRIPS, snap_step, halve_step, lo, hi, cnt_lo)
        return lo, hi, cnt_lo, trip + 1, n_active(lo, hi, cnt_lo)

    cnt_lo = jnp.full((1, tq), -1, jnp.int32)
    thr, _, cnt_thr, _, _ = lax.while_loop(
        lambda c: c[4] > 0, narrow, (lo, hi, cnt_lo, jnp.int32(0), n_active(lo, hi, cnt_lo)))
    uncounted = jnp.logical_and(need, cnt_thr < 0)
    cnt_thr = lax.cond(jnp.max(uncounted.astype(jnp.int32)) > 0,
                       lambda: jnp.where(uncounted, count_ge(thr), cnt_thr), lambda: cnt_thr)
    thr = jnp.maximum(thr, KEY_NEG_INF + 1)

    tied = jnp.logical_and(need, cnt_thr > topk)
    keep_all = jnp.full((1, tq), s_len, jnp.int32)

    def tie_cut():
        n_keep = topk - count_ge(thr + 1)

        def step(_, carry):
            jlo, jhi = carry
            jm = (jlo + jhi) >> 1

            def pred(kt, j):
                kpos = j * tk + lax.broadcasted_iota(jnp.int32, (tk, tq), 0)
                return jnp.logical_and(kt == thr, kpos <= jm)

            ok = count_where(pred) >= n_keep
            return jnp.where(ok, jlo, jm + 1), jnp.where(ok, jm, jhi)

        _, jhi = lax.fori_loop(0, (s_len - 1).bit_length(), step,
                               (jnp.zeros((1, tq), jnp.int32), keep_all - 1))
        return jnp.where(tied, jhi, keep_all)

    jcut = lax.cond(jnp.max(tied.astype(jnp.int32)) > 0, tie_cut, lambda: keep_all)

    def to_column(v):
        return jnp.transpose(jnp.broadcast_to(v, (LANES, tq)))[:, 0:1]

    thr = to_column(thr)
    jcut = to_column(jcut)

    qd = qd_ref[...].reshape(DSA_HEADS * tq, DSA_HEAD_DIM)

    def attn_tile(j, carry):
        m, acc = carry
        start = pl.multiple_of(j * tka, tka)
        kt = kd_ref[pl.ds(start, tka), :]
        vt = vd_ref[pl.ds(start, tka), :]
        keys = keys_ref[:, pl.ds(start, tka)]
        col = j * tka + lax.broadcasted_iota(jnp.int32, (tq, tka), 1)
        sel = jnp.logical_or(keys > thr, jnp.logical_and(keys == thr, col <= jcut))
        s = lax.dot_general(qd, kt, _NT, preferred_element_type=F32)
        s = jnp.where(sel[None], s.reshape(DSA_HEADS, tq, tka), NEG_BIG)
        m_new = jnp.maximum(m, jnp.max(s, axis=-1, keepdims=True))
        p = jnp.exp2(s - m_new).astype(BF16)
        pv = jnp.dot(p.reshape(DSA_HEADS * tq, tka), vt, preferred_element_type=F32)
        acc = jnp.exp2(m - m_new) * acc + pv.reshape(DSA_HEADS, tq, LANES)
        return m_new, acc

    init = (jnp.full((DSA_HEADS, tq, 1), NEG_BIG, F32), jnp.zeros((DSA_HEADS, tq, LANES), F32))
    m, acc = lax.fori_loop(0, n_tiles * (tk // tka), attn_tile, init)
    for h in range(DSA_HEADS):
        out = acc[h, :, :DSA_HEAD_DIM] / acc[h, :, DSA_HEAD_DIM:DSA_HEAD_DIM + 1]
        o_ref[:, h * DSA_HEAD_DIM:(h + 1) * DSA_HEAD_DIM] = out.astype(BF16)


def _dsa_attention(qi, qd, ki, kd, vd, wi, b, s, tq, tk, tka):
    assert tk % tka == 0 and tk % SEARCH_GROUPS == 0 and SEARCH_GROUPS >= TOPK_MAX
    n = b * s
    nq = s // tq
    topk = min(TOPK_MAX, s // 4)
    hm = pl.BlockSpec((DSA_HEADS, tq, DSA_HEAD_DIM), lambda bi, qb: (0, bi * nq + qb, 0))
    kv = pl.BlockSpec((s, DSA_HEAD_DIM), lambda bi, qb: (bi, 0))
    return pl.pallas_call(
        functools.partial(_dsa_body, tq=tq, tk=tk, tka=tka, topk=topk, s_len=s),
        grid=(b, nq),
        in_specs=[hm, hm, kv, kv, pl.BlockSpec((s, LANES), lambda bi, qb: (bi, 0)),
                  pl.BlockSpec((tq, IDX_HEADS), lambda bi, qb: (bi * nq + qb, 0))],
        out_specs=pl.BlockSpec((tq, DSA_HEADS * DSA_HEAD_DIM), lambda bi, qb: (bi * nq + qb, 0)),
        out_shape=jax.ShapeDtypeStruct((n, DSA_HEADS * DSA_HEAD_DIM), BF16),
        scratch_shapes=[pltpu.VMEM((tq, s), jnp.int32), pltpu.VMEM((s, tq), jnp.int32)],
        compiler_params=_cparams(("parallel", "arbitrary")),
        name="dsa_attention",
    )(qi, qd, ki, kd, vd, wi)


HALF_D = D_MODEL // 2
RUN_ALIGN = 8


def _pack_bf16_pairs(y):
    r = pltpu.bitcast(y.astype(BF16).astype(F32), jnp.uint32)
    return r[:, :HALF_D] | (r[:, HALF_D:] >> 16)


def _unpack_bf16_pairs(p):
    hi = pltpu.bitcast(p & jnp.uint32(0xFFFF0000), F32)
    lo = pltpu.bitcast(p << 16, F32)
    return jnp.concatenate([hi, lo], axis=1)


def _split_bf16(a):
    hi = a.astype(BF16)
    lo = (a - hi.astype(F32)).astype(BF16)
    return hi, lo


def _merge_body(x_ref, ya_ref, yb_ref, ga_ref, gb_ref, wa_ref, wb_ref, wo_ref, g_ref,
                rwh_ref, rwl_ref, rb_ref, x2_ref, h_ref, logit_ref):
    ma = jnp.dot(ya_ref[...], wa_ref[...], preferred_element_type=F32)
    mb = jnp.dot(yb_ref[...], wb_ref[...], preferred_element_type=F32)
    merged = ga_ref[...].astype(F32) * ma + gb_ref[...].astype(F32) * mb
    x2 = x_ref[...] + jnp.dot(merged.astype(BF16), wo_ref[...], preferred_element_type=F32)
    x2_ref[...] = x2
    h = x2 * lax.rsqrt(jnp.mean(x2 * x2, axis=-1, keepdims=True) + EPS) * g_ref[...]
    hh, hl = _split_bf16(h)
    h_ref[...] = hh
    logit_ref[...] = (jnp.dot(hh, rwh_ref[...], preferred_element_type=F32)
                      + jnp.dot(hh, rwl_ref[...], preferred_element_type=F32)
                      + jnp.dot(hl, rwh_ref[...], preferred_element_type=F32)) + rb_ref[...]


def _merge(x2d, y_a, y_b, ga, gb, w_ba, w_bb, w_out, ffn_norm_g, router_w, router_b, tm):
    n, d = x2d.shape
    rw = jnp.pad(router_w, ((0, 0), (0, LANES - N_EXPERTS)))
    rwh = rw.astype(BF16)
    rwl = (rw - rwh.astype(F32)).astype(BF16)
    rb = jnp.pad(router_b, (0, LANES - N_EXPERTS), constant_values=NEG_BIG).reshape(1, LANES)
    consts = (w_ba.astype(BF16), w_bb.astype(BF16), w_out.astype(BF16), ffn_norm_g.reshape(1, d),
              rwh, rwl, rb)
    row = lambda w_: pl.BlockSpec((tm, w_), lambda i: (i, 0))
    full = lambda a: pl.BlockSpec(a.shape, lambda i: (0,) * a.ndim)
    out_shape = (
        jax.ShapeDtypeStruct((n, d), F32),
        jax.ShapeDtypeStruct((n, d), BF16),
        jax.ShapeDtypeStruct((n, LANES), F32),
    )
    return pl.pallas_call(
        _merge_body,
        grid=(n // tm,),
        in_specs=[row(d), row(HALF_D), row(HALF_D), row(d), row(d)] + [full(a) for a in consts],
        out_specs=(row(d), row(d), row(LANES)),
        out_shape=out_shape,
        compiler_params=_cparams(("parallel",)),
        name="merge",
    )(x2d, y_a, y_b, ga, gb, *consts)


def _router_body(logit_ref, upper_ref, e_ref, gate_ref, lp_ref, bc_ref, carry_out_ref, cnt_ref,
                 carry_ref, *, tm):
    i = pl.program_id(0)

    @pl.when(i == 0)
    def _():
        carry_ref[...] = jnp.zeros_like(carry_ref)

    lane = lax.broadcasted_iota(jnp.int32, (tm, LANES), 1)
    work = logit_ref[...]
    experts, vals = [], []
    onehot = jnp.zeros((tm, LANES), F32)
    for _ in range(TOP_K):
        mx = jnp.max(work, axis=-1, keepdims=True)
        idx = jnp.min(jnp.where(work == mx, lane, LANES), axis=-1, keepdims=True)
        hit = lane == idx
        experts.append(idx)
        vals.append(mx)
        onehot = onehot + hit.astype(F32)
        work = jnp.where(hit, -jnp.inf, work)
    ex = [jnp.exp(v - vals[0]) for v in vals]
    denom = ex[0] + ex[1] + ex[2] + ex[3]
    for r in range(TOP_K):
        e_ref[:, r:r + 1] = experts[r]
        gate_ref[:, r:r + 1] = ex[r] / denom

    rr = lax.broadcasted_iota(jnp.int32, (tm, tm), 0)
    cc = lax.broadcasted_iota(jnp.int32, (tm, tm), 1)
    lower = (cc < rr).astype(BF16)
    prefix = jnp.dot(lower, onehot.astype(BF16), preferred_element_type=F32)
    bc = jnp.sum(onehot, axis=0, keepdims=True)
    bc = jnp.floor((bc + (RUN_ALIGN - 1)) * (1.0 / RUN_ALIGN)) * RUN_ALIGN
    bc8 = jnp.broadcast_to(bc, (8, LANES))
    boff = jnp.dot(bc8.astype(BF16), upper_ref[...], preferred_element_type=F32)[0:1, :]
    local = prefix + boff
    for r in range(TOP_K):
        lp = jnp.sum(jnp.where(lane == experts[r], local, 0.0), axis=-1, keepdims=True)
        lp_ref[:, r:r + 1] = lp.astype(jnp.int32)
    bc_ref[0] = bc8
    carry_out_ref[0] = carry_ref[...]
    total = carry_ref[...] + bc8
    carry_ref[...] = total
    cnt_ref[...] = total


def _router(logits, tm):
    assert tm <= 256
    n = logits.shape[0]
    nblk = n // tm
    upper = (jnp.arange(LANES)[:, None] < jnp.arange(LANES)[None, :]).astype(BF16)
    row = lambda w_: pl.BlockSpec((tm, w_), lambda i: (i, 0))
    blk = pl.BlockSpec((1, 8, LANES), lambda i: (i, 0, 0))
    out_shape = (
        jax.ShapeDtypeStruct((n, TOP_K), jnp.int32),
        jax.ShapeDtypeStruct((n, TOP_K), F32),
        jax.ShapeDtypeStruct((n, TOP_K), jnp.int32),
        jax.ShapeDtypeStruct((nblk, 8, LANES), F32),
        jax.ShapeDtypeStruct((nblk, 8, LANES), F32),
        jax.ShapeDtypeStruct((8, LANES), F32),
    )
    return pl.pallas_call(
        functools.partial(_router_body, tm=tm),
        grid=(nblk,),
        in_specs=[row(LANES), pl.BlockSpec((LANES, LANES), lambda i: (0, 0))],
        out_specs=(row(TOP_K), row(TOP_K), row(TOP_K), blk, blk,
                   pl.BlockSpec((8, LANES), lambda i: (0, 0))),
        out_shape=out_shape,
        scratch_shapes=[pltpu.VMEM((8, LANES), F32)],
        compiler_params=_cparams(("arbitrary",)),
        name="router",
    )(logits, upper)


def _local_rows(tm):
    return TOP_K * tm + N_EXPERTS * RUN_ALIGN


def _for_each_run_chunk(tbl_ref, tm, fn):
    sizes = [s for s in (1 << k for k in range(tm.bit_length())) if RUN_ALIGN <= s <= tm]

    def per_expert(e, _):
        length = tbl_ref[0, 0, e]
        src = tbl_ref[0, 0, N_EXPERTS + e]
        dst = tbl_ref[0, 0, 2 * N_EXPERTS + e]
        for piece, size in enumerate(sizes):

            @pl.when((length & size) != 0)
            def _(piece=piece, size=size):
                off = length & (size - 1)
                fn(pl.multiple_of(src + off, RUN_ALIGN), pl.multiple_of(dst + off, RUN_ALIGN), size,
                   piece)
        return 0

    lax.fori_loop(0, N_EXPERTS, per_expert, 0)


def _dispatch_body(tbl_ref, tbl_prev_ref, tail_ref, lpt_ref, h_ref, xbuf_ref, sorted_ref, zero_ref,
                   sems, *, tm):
    i = pl.program_id(0)
    slot = i % 2
    h = h_ref[...]
    lpt = lpt_ref[0]
    chunk = tm
    for c in range(_local_rows(tm) // chunk):
        r_idx = c * chunk + lax.broadcasted_iota(jnp.int32, (chunk, tm), 0)
        hit = r_idx == lpt[0:1, :]
        for r in range(1, TOP_K):
            hit = jnp.logical_or(hit, r_idx == lpt[r:r + 1, :])
        sel = jnp.where(hit, 1.0, 0.0)
        rows = jnp.dot(sel.astype(BF16), h, preferred_element_type=F32)
        sorted_ref[slot, c * chunk:(c + 1) * chunk, :] = _pack_bf16_pairs(rows)

    def copy(buf, local_row, global_row, size):
        return pltpu.make_async_copy(sorted_ref.at[buf, pl.ds(local_row, size)],
                                     xbuf_ref.at[pl.ds(global_row, size)], sems.at[buf])

    _for_each_run_chunk(tbl_ref, tm, lambda s, d, n, k: copy(slot, s, d, n).start(priority=k % 2))

    @pl.when(i > 0)
    def _():
        _for_each_run_chunk(tbl_prev_ref, tm, lambda s, d, n, k: copy(1 - slot, s, d, n).wait())

    @pl.when(i == pl.num_programs(0) - 1)
    def _():
        _for_each_run_chunk(tbl_ref, tm, lambda s, d, n, k: copy(slot, s, d, n).wait())
        zero_ref[...] = jnp.zeros_like(zero_ref)

        def zero_copy(local_row, global_row, size):
            return pltpu.make_async_copy(zero_ref.at[pl.ds(local_row, size)],
                                         xbuf_ref.at[pl.ds(global_row, size)], sems.at[2])

        _for_each_run_chunk(tail_ref, tm, lambda s, d, n, k: zero_copy(s, d, n).start())
        _for_each_run_chunk(tail_ref, tm, lambda s, d, n, k: zero_copy(s, d, n).wait())

        tb = zero_ref.shape[0]
        first_unused = tail_ref[0, 0, 3 * N_EXPERTS]

        def block_copy(blk):
            return zero_copy(0, pl.multiple_of(blk * tb, tb), tb)

        n_blocks = xbuf_ref.shape[0] // tb
        lax.fori_loop(first_unused, n_blocks, lambda blk, _: (block_copy(blk).start(), 0)[1], 0)
        lax.fori_loop(first_unused, n_blocks, lambda blk, _: (block_copy(blk).wait(), 0)[1], 0)


def _dispatch(h2, lpt, tbl, tail_tbl, n_rows, tb, tm):
    assert tb <= 2 * tm
    n = h2.shape[0]
    smem_tbl = lambda index_map: pl.BlockSpec((1, 1, LANES), index_map, memory_space=pltpu.SMEM)
    return pl.pallas_call(
        functools.partial(_dispatch_body, tm=tm),
        grid=(n // tm,),
        in_specs=[smem_tbl(lambda i: (i, 0, 0)),
                  smem_tbl(lambda i: (jnp.maximum(i - 1, 0), 0, 0)),
                  smem_tbl(lambda i: (0, 0, 0)),
                  pl.BlockSpec((1, TOP_K, tm), lambda i: (i, 0, 0)),
                  pl.BlockSpec((tm, D_MODEL), lambda i: (i, 0))],
        out_specs=pl.BlockSpec(memory_space=pl.ANY),
        out_shape=jax.ShapeDtypeStruct((n_rows, HALF_D), jnp.uint32),
        scratch_shapes=[pltpu.VMEM((2, _local_rows(tm), HALF_D), jnp.uint32),
                        pltpu.VMEM((tb, HALF_D), jnp.uint32),
                        pltpu.SemaphoreType.DMA((3,))],
        compiler_params=_cparams(("arbitrary",)),
        name="moe_dispatch",
    )(tbl, tbl, tail_tbl, lpt, h2)


def _expert_body(blk_e_ref, n_used_ref, x_ref, wg_ref, wl_ref, bg_ref, bl_ref, wd_ref, bd_ref,
                 y_ref, wg_s, wl_s, wd_s):
    i = pl.program_id(0)
    used = i < n_used_ref[0]

    changed = jnp.logical_or(i == 0, blk_e_ref[i] != blk_e_ref[jnp.maximum(i - 1, 0)])

    @pl.when(jnp.logical_and(used, changed))
    def _():
        wg_s[...] = wg_ref[0].astype(BF16)
        wl_s[...] = wl_ref[0].astype(BF16)
        wd_s[...] = wd_ref[0].astype(BF16)

    @pl.when(used)
    def _():
        xb = _unpack_bf16_pairs(x_ref[...]).astype(BF16)
        glu = jnp.dot(xb, wg_s[...], preferred_element_type=F32) + bg_ref[0]
        lin = jnp.dot(xb, wl_s[...], preferred_element_type=F32) + bl_ref[0]
        glu = jnp.minimum(glu, SWIGLU_LIMIT)
        lin = jnp.clip(lin, -SWIGLU_LIMIT, SWIGLU_LIMIT)
        act = glu * jax.nn.sigmoid(SWIGLU_ALPHA * glu) * (lin + 1.0)
        y = jnp.dot(act.astype(BF16), wd_s[...], preferred_element_type=F32) + bd_ref[0]
        y_ref[...] = _pack_bf16_pairs(y)

    @pl.when(jnp.logical_not(used))
    def _():
        y_ref[...] = jnp.zeros_like(y_ref)


def _experts(xbuf, blk_e, n_used, w_gate_up, b_gate_up, w_down, b_down, tb):
    n_rows = xbuf.shape[0]
    d = D_MODEL
    bgu = b_gate_up.reshape(N_EXPERTS, 1, 2 * D_FF)
    bd = b_down.reshape(N_EXPERTS, 1, d)
    grid_spec = pltpu.PrefetchScalarGridSpec(
        num_scalar_prefetch=2,
        grid=(n_rows // tb,),
        in_specs=[
            pl.BlockSpec((tb, HALF_D), lambda i, be, nu: (jnp.minimum(i, nu[0] - 1), 0)),
            pl.BlockSpec((1, d, D_FF), lambda i, be, nu: (be[i], 0, 0)),
            pl.BlockSpec((1, d, D_FF), lambda i, be, nu: (be[i], 0, 1)),
            pl.BlockSpec((1, 1, D_FF), lambda i, be, nu: (be[i], 0, 0)),
            pl.BlockSpec((1, 1, D_FF), lambda i, be, nu: (be[i], 0, 1)),
            pl.BlockSpec((1, D_FF, d), lambda i, be, nu: (be[i], 0, 0)),
            pl.BlockSpec((1, 1, d), lambda i, be, nu: (be[i], 0, 0)),
        ],
        out_specs=pl.BlockSpec((tb, HALF_D), lambda i, be, nu: (i, 0)),
        scratch_shapes=[pltpu.VMEM((d, D_FF), BF16), pltpu.VMEM((d, D_FF), BF16),
                        pltpu.VMEM((D_FF, d), BF16)],
    )
    return pl.pallas_call(
        _expert_body,
        grid_spec=grid_spec,
        out_shape=jax.ShapeDtypeStruct((n_rows, HALF_D), jnp.uint32),
        compiler_params=_cparams(("arbitrary",)),
        name="moe_experts",
    )(blk_e, n_used, xbuf, w_gate_up, w_gate_up, bgu, bgu, w_down, bd)


def _combine_body(tbl_ref, tbl_next_ref, x2_ref, gate_ref, lp_ref, g_ref, ybuf_ref, o_ref, ys_ref,
                  sems, *, tm):
    i = pl.program_id(0)
    slot = i % 2
    n_local = _local_rows(tm)

    def copy(buf, local_row, global_row, size):
        return pltpu.make_async_copy(ybuf_ref.at[pl.ds(global_row, size)],
                                     ys_ref.at[buf, pl.ds(local_row, size)], sems.at[buf])

    def start_gather(table_ref, buf):
        ys_ref[buf, TOP_K * tm:, :] = jnp.zeros((n_local - TOP_K * tm, HALF_D), jnp.uint32)
        _for_each_run_chunk(table_ref, tm,
                            lambda s, d, n, k: copy(buf, s, d, n).start(priority=k % 2))

    @pl.when(i == 0)
    def _():
        start_gather(tbl_ref, slot)

    @pl.when(i < pl.num_programs(0) - 1)
    def _():
        start_gather(tbl_next_ref, 1 - slot)

    _for_each_run_chunk(tbl_ref, tm, lambda s, d, n, k: copy(slot, s, d, n).wait())

    gate = gate_ref[...]
    lp = lp_ref[...]
    col = lax.broadcasted_iota(jnp.int32, (tm, n_local), 1)
    gmat = jnp.zeros((tm, n_local), F32)
    for r in range(TOP_K):
        gmat = gmat + jnp.where(col == lp[:, r:r + 1], gate[:, r:r + 1], 0.0)
    g_hi, g_lo = _split_bf16(gmat)
    ys = _unpack_bf16_pairs(ys_ref[slot]).astype(BF16)
    out = (x2_ref[...] + jnp.dot(g_hi, ys, preferred_element_type=F32)
           + jnp.dot(g_lo, ys, preferred_element_type=F32))
    o_ref[...] = out * lax.rsqrt(jnp.mean(out * out, axis=-1, keepdims=True) + EPS) * g_ref[...]


def _combine(x2, gate, lp, tbl, ybuf, final_norm_g, tm):
    n, d = x2.shape
    nblk = n // tm
    smem_tbl = lambda index_map: pl.BlockSpec((1, 1, LANES), index_map, memory_space=pltpu.SMEM)
    return pl.pallas_call(
        functools.partial(_combine_body, tm=tm),
        grid=(nblk,),
        in_specs=[smem_tbl(lambda i: (i, 0, 0)),
                  smem_tbl(lambda i: (jnp.minimum(i + 1, nblk - 1), 0, 0)),
                  pl.BlockSpec((tm, d), lambda i: (i, 0)),
                  pl.BlockSpec((tm, TOP_K), lambda i: (i, 0)),
                  pl.BlockSpec((tm, TOP_K), lambda i: (i, 0)),
                  pl.BlockSpec((1, d), lambda i: (0, 0)),
                  pl.BlockSpec(memory_space=pl.ANY)],
        out_specs=pl.BlockSpec((tm, d), lambda i: (i, 0)),
        out_shape=jax.ShapeDtypeStruct((n, d), F32),
        scratch_shapes=[pltpu.VMEM((2, _local_rows(tm), HALF_D), jnp.uint32),
                        pltpu.SemaphoreType.DMA((2,))],
        compiler_params=_cparams(("arbitrary",)),
        name="moe_combine",
    )(tbl, tbl, x2, gate, lp, final_norm_g.reshape(1, d), ybuf)


def _moe(x2, h2, logits, w_gate_up, b_gate_up, w_down, b_down, final_norm_g, tb, tm):
    n = x2.shape[0]
    nblk = n // tm
    max_rows = n * TOP_K + nblk * N_EXPERTS * (RUN_ALIGN - 1)
    n_blk = -(-max_rows // tb) + N_EXPERTS
    top_e, gate, lp, bcount, before, counts = _router(logits, tm)
    del top_e
    cnt = counts[0, :N_EXPERTS].astype(jnp.int32)
    padded = (cnt + tb - 1) // tb * tb
    pad_end = jnp.cumsum(padded)
    pad_start = pad_end - padded
    blk_first = jnp.arange(n_blk, dtype=jnp.int32) * tb
    blk_e = jnp.minimum(jnp.sum((pad_end[None, :] <= blk_first[:, None]).astype(jnp.int32), axis=1),
                        N_EXPERTS - 1)
    n_used = (pad_end[-1:] // tb).astype(jnp.int32)
    run_len = bcount[:, 0, :N_EXPERTS].astype(jnp.int32)
    run_src = jnp.cumsum(run_len, axis=1) - run_len
    run_dst = pad_start[None, :] + before[:, 0, :N_EXPERTS].astype(jnp.int32)
    tbl = jnp.concatenate([run_len, run_src, run_dst,
                           jnp.zeros((nblk, LANES - 3 * N_EXPERTS), jnp.int32)], axis=1)
    tbl = tbl.reshape(nblk, 1, LANES)
    lpt = lp.reshape(nblk, tm, TOP_K).transpose(0, 2, 1)
    zeros32 = jnp.zeros((N_EXPERTS,), jnp.int32)
    tail_tbl = jnp.concatenate([padded - cnt, zeros32, pad_start + cnt, n_used,
                                jnp.zeros((LANES - 3 * N_EXPERTS - 1,), jnp.int32)]
                               ).reshape(1, 1, LANES)
    xbuf = _dispatch(h2, lpt, tbl, tail_tbl, n_blk * tb, tb, tm)
    ybuf = _experts(xbuf, blk_e, n_used, w_gate_up, b_gate_up, w_down, b_down, tb)
    return _combine(x2, gate, lp, tbl, ybuf, final_norm_g, tm)


def kernel(x, positions, attn_norm_g, w_in, mla_q_norm_g, mla_w_uq, mla_kv_norm_g, mla_w_ukv,
           w_branch_mla, w_branch_dsa, w_out, ffn_norm_g, router_w, router_b,
           w_gate_up, b_gate_up, w_down, b_down, final_norm_g):
    b, s, d = x.shape
    n = b * s
    (qm, km, vm, qd, qi, kd, ki, vd, wi, ga, gb) = _in_proj(
        x.reshape(n, d), positions.reshape(n), attn_norm_g[0], w_in[0],
        mla_q_norm_g[0], mla_w_uq[0], mla_kv_norm_g[0], mla_w_ukv[0], tm=512)
    y_a = _mla_attention(qm, km, vm, b, s, tq=1024, tk=1024)
    y_b = _dsa_attention(qi, qd, ki, kd, vd, wi, b, s, tq=256, tk=1024, tka=1024)
    x2, h2, logits = _merge(
        x.reshape(n, d), y_a, y_b, ga, gb, w_branch_mla[0], w_branch_dsa[0], w_out[0],
        ffn_norm_g[0], router_w[0], router_b[0], tm=512)
    out = _moe(x2, h2, logits, w_gate_up[0], b_gate_up[0], w_down[0], b_down[0], final_norm_g,
               tb=512, tm=256)
    return out.reshape(b, s, d)
```
